```python
import math
import jax, jax.numpy as jnp
from jax import lax
import numpy as np

D_MODEL = 1024
BATCH = 32
SEQ = 256
DEPTH = 4
DEC_BATCH = 4
DEC_SEQ = 1024
PAST_LEN = 256

GRID_W = 64
N_MIXERS = 3
N_SSM_LAYERS = (DEPTH + 2) // 3
N_GMLP_LAYERS = (DEPTH + 1) // 3
N_CONV_LAYERS = DEPTH // 3

SSM_WIDTH = D_MODEL
SSM_GROUP = 16
SSM_GROUPS = SSM_WIDTH // SSM_GROUP
SSM_STATE = 64
N_DIR = 2
DT_MIN = 1e-3
DT_MAX = 1e-1

GMLP_WIDTH = 2 * D_MODEL
GMLP_CHUNK = 128
GMLP_GROUP_DIM = 128
GMLP_GROUPS = GMLP_WIDTH // GMLP_GROUP_DIM

CONV_WIDTH = 3
FFN_HIDDEN = 4 * D_MODEL
N_MOD = 6
EPS = 1e-6

kernel_name = "hybrid_s5_gmlp_conv_diffusion_step"


def rms_norm(x, g):
    xf = x.astype(jnp.float32)
    y = xf * lax.rsqrt(jnp.mean(xf * xf, axis=-1, keepdims=True) + EPS)
    return (y * g.astype(jnp.float32)).astype(x.dtype)


def layer_norm_plain(x):
    xf = x.astype(jnp.float32)
    mu = jnp.mean(xf, axis=-1, keepdims=True)
    xc = xf - mu
    return (xc * lax.rsqrt(jnp.mean(xc * xc, axis=-1, keepdims=True) + EPS)).astype(x.dtype)


def adaln(cond, w, b):
    m = jax.nn.silu(cond) @ w + b
    return jnp.split(m[:, None, :], N_MOD, axis=-1)


def modulate(h, shift, scale):
    return h * (1.0 + scale) + shift


def grid_pos_embed(n_tokens):
    rows = n_tokens // GRID_W
    t = jnp.arange(rows * GRID_W)
    r = (t // GRID_W).astype(jnp.float32)
    col = (t % GRID_W).astype(jnp.float32)
    quarter = D_MODEL // 4
    freq = 1.0 / (10000.0 ** (jnp.arange(quarter, dtype=jnp.float32) / quarter))
    ar = r[:, None] * freq
    ac = col[:, None] * freq
    return jnp.concatenate([jnp.sin(ar), jnp.cos(ar), jnp.sin(ac), jnp.cos(ac)], axis=-1)


def cmul(ar, ai, br, bi):
    return ar * br - ai * bi, ar * bi + ai * br


def _scan_combine(e1, e2):
    a1r, a1i, b1r, b1i = e1
    a2r, a2i, b2r, b2i = e2
    ar, ai = cmul(a2r, a2i, a1r, a1i)
    br, bi = cmul(a2r, a2i, b1r, b1i)
    return ar, ai, br + b2r, bi + b2i


def zoh(lam_re, lam_im, log_dt, b_re, b_im):
    dt = jnp.exp(log_dt.astype(jnp.float32))[..., None]
    lr = lam_re.astype(jnp.float32)
    li = lam_im.astype(jnp.float32)
    mag = jnp.exp(lr * dt)
    ab_re = mag * jnp.cos(li * dt)
    ab_im = mag * jnp.sin(li * dt)
    den = lr * lr + li * li
    nr, ni = cmul(ab_re - 1.0, ab_im, lr, -li)
    f_re, f_im = nr / den, ni / den
    bb_re, bb_im = cmul(f_re[..., None], f_im[..., None], b_re.astype(jnp.float32), b_im.astype(jnp.float32))
    return ab_re, ab_im, bb_re, bb_im


def ssm_mixer(h, w_in, lam_re, lam_im, log_dt, b_re, b_im, c_re, c_im, d_skip, w_out, h0_re, h0_im):
    bsz, length, _ = h.shape
    u = (h @ w_in).astype(jnp.float32)
    ug = u.reshape(bsz, length, SSM_GROUPS, SSM_GROUP)
    ab_re, ab_im, bb_re, bb_im = zoh(lam_re, lam_im, log_dt, b_re, b_im)
    bu_re = jnp.einsum('blgp,kgnp->kblgn', ug, bb_re)
    bu_im = jnp.einsum('blgp,kgnp->kblgn', ug, bb_im)
    y = d_skip.astype(jnp.float32) * u
    states = []
    for k, rev in ((0, False), (1, True)):
        a_re = jnp.broadcast_to(ab_re[k][None, None], bu_re[k].shape)
        a_im = jnp.broadcast_to(ab_im[k][None, None], bu_im[k].shape)
        acr, aci, sr, si = lax.associative_scan(_scan_combine, (a_re, a_im, bu_re[k], bu_im[k]), reverse=rev, axis=1)
        if h0_re is not None:
            pr, pim = cmul(acr, aci, h0_re[:, k][:, None].astype(jnp.float32), h0_im[:, k][:, None].astype(jnp.float32))
            sr, si = sr + pr, si + pim
        yk = (jnp.einsum('blgn,gpn->blgp', sr, c_re[k].astype(jnp.float32))
              - jnp.einsum('blgn,gpn->blgp', si, c_im[k].astype(jnp.float32)))
        y = y + yk.reshape(bsz, length, SSM_WIDTH)
        states.append((sr, si))
    z = jax.nn.gelu(y.astype(h.dtype))
    a, g = jnp.split(z @ w_out, 2, axis=-1)
    return a * jax.nn.sigmoid(g), states[0], states[1]


def gmlp_mixer(h, w_in, w_s, b_s, w_out):
    bsz, length, _ = h.shape
    z = jax.nn.gelu(h @ w_in)
    u, v = jnp.split(z, 2, axis=-1)
    v = layer_norm_plain(v)
    vc = v.reshape(bsz, length // GMLP_CHUNK, GMLP_CHUNK, GMLP_GROUPS, GMLP_GROUP_DIM)
    s = jnp.einsum('gpq,bcqgd->bcpgd', w_s, vc) + b_s.T[None, None, :, :, None]
    return (u * s.reshape(bsz, length, GMLP_WIDTH)) @ w_out


def conv_mixer(h, w_in, conv_w, w_out):
    gb, gc, xh = jnp.split(h @ w_in, 3, axis=-1)
    y = lax.conv_general_dilated(gc * xh, conv_w[:, None, :], window_strides=(1,),
                                 padding=((CONV_WIDTH // 2, CONV_WIDTH // 2),),
                                 dimension_numbers=('NWC', 'WIO', 'NWC'),
                                 feature_group_count=D_MODEL)
    return (gb * y) @ w_out


def sqrelu_ffn(h, w1, w2):
    return jnp.square(jax.nn.relu(h @ w1)) @ w2


def setup_inputs(seed: int = 0) -> dict:
    key = jax.random.key(seed)
    ks = jax.random.split(key, 32)
    f32 = jnp.float32

    def nrm(k, shape, scale):
        return jax.random.normal(k, shape, f32) * scale

    n_idx = jnp.arange(SSM_STATE, dtype=f32)
    ssm_shape = (N_SSM_LAYERS, N_DIR, SSM_GROUPS, SSM_STATE)
    return {
        "x_prompt": nrm(ks[0], (BATCH, SEQ, D_MODEL), 1.0),
        "x_sample": nrm(ks[1], (DEC_BATCH, DEC_SEQ, D_MODEL), 1.0),
        "state_ssm_re": nrm(ks[2], (DEC_BATCH, N_SSM_LAYERS, N_DIR, SSM_GROUPS, SSM_STATE), 0.05),
        "state_ssm_im": nrm(ks[3], (DEC_BATCH, N_SSM_LAYERS, N_DIR, SSM_GROUPS, SSM_STATE), 0.05),
        "c": nrm(ks[4], (DEC_BATCH, D_MODEL), 1.0),
        "c_ctx": nrm(ks[5], (D_MODEL,), 1.0),
        "w_mod": nrm(ks[6], (DEPTH, D_MODEL, N_MOD * D_MODEL), 0.5 * D_MODEL ** -0.5),
        "b_mod": nrm(ks[7], (DEPTH, N_MOD * D_MODEL), 0.01),
        "g_mix": 1.0 + nrm(ks[8], (DEPTH, D_MODEL), 0.02),
        "g_ffn": 1.0 + nrm(ks[9], (DEPTH, D_MODEL), 0.02),
        "ffn_w1": nrm(ks[10], (DEPTH, D_MODEL, FFN_HIDDEN), D_MODEL ** -0.5),
        "ffn_w2": nrm(ks[11], (DEPTH, FFN_HIDDEN, D_MODEL), FFN_HIDDEN ** -0.5),
        "ssm_w_in": nrm(ks[12], (N_SSM_LAYERS, D_MODEL, SSM_WIDTH), D_MODEL ** -0.5),
        "ssm_lam_re": -0.5 + nrm(ks[13], ssm_shape, 0.01),
        "ssm_lam_im": math.pi * n_idx + nrm(ks[14], ssm_shape, 0.01),
        "ssm_log_dt": jax.random.uniform(ks[15], (N_SSM_LAYERS, N_DIR, SSM_GROUPS), f32,
                                         minval=math.log(DT_MIN), maxval=math.log(DT_MAX)),
        "ssm_b_re": nrm(ks[16], ssm_shape + (SSM_GROUP,), (2.0 * SSM_GROUP) ** -0.5),
        "ssm_b_im": nrm(ks[17], ssm_shape + (SSM_GROUP,), (2.0 * SSM_GROUP) ** -0.5),
        "ssm_c_re": nrm(ks[18], (N_SSM_LAYERS, N_DIR, SSM_GROUPS, SSM_GROUP, SSM_STATE), (2.0 * SSM_STATE) ** -0.5),
        "ssm_c_im": nrm(ks[19], (N_SSM_LAYERS, N_DIR, SSM_GROUPS, SSM_GROUP, SSM_STATE), (2.0 * SSM_STATE) ** -0.5),
        "ssm_d": nrm(ks[20], (N_SSM_LAYERS, SSM_WIDTH), 1.0),
        "ssm_w_out": nrm(ks[21], (N_SSM_LAYERS, SSM_WIDTH, 2 * D_MODEL), SSM_WIDTH ** -0.5),
        "gmlp_w_in": nrm(ks[22], (N_GMLP_LAYERS, D_MODEL, 2 * GMLP_WIDTH), D_MODEL ** -0.5),
        "gmlp_w_s": nrm(ks[23], (N_GMLP_LAYERS, GMLP_GROUPS, GMLP_CHUNK, GMLP_CHUNK), GMLP_CHUNK ** -0.5),
        "gmlp_b_s": 1.0 + nrm(ks[24], (N_GMLP_LAYERS, GMLP_GROUPS, GMLP_CHUNK), 0.02),
        "gmlp_w_out": nrm(ks[25], (N_GMLP_LAYERS, GMLP_WIDTH, D_MODEL), GMLP_WIDTH ** -0.5),
        "conv_w_in": nrm(ks[26], (N_CONV_LAYERS, D_MODEL, 3 * D_MODEL), D_MODEL ** -0.5),
        "conv_w": nrm(ks[27], (N_CONV_LAYERS, CONV_WIDTH, D_MODEL), CONV_WIDTH ** -0.5),
        "conv_w_out": nrm(ks[28], (N_CONV_LAYERS, D_MODEL, D_MODEL), D_MODEL ** -0.5),
        "g_final": 1.0 + nrm(ks[29], (D_MODEL,), 0.02),
    }


def reference(x_prompt, x_sample, state_ssm_re, state_ssm_im, c, c_ctx, w_mod, b_mod, g_mix, g_ffn,
              ffn_w1, ffn_w2, ssm_w_in, ssm_lam_re, ssm_lam_im, ssm_log_dt, ssm_b_re, ssm_b_im,
              ssm_c_re, ssm_c_im, ssm_d, ssm_w_out, gmlp_w_in, gmlp_w_s, gmlp_b_s, gmlp_w_out,
              conv_w_in, conv_w, conv_w_out, g_final):
    xp = x_prompt
    n_lat = x_sample.shape[1]
    xs = x_sample + grid_pos_embed(n_lat).astype(x_sample.dtype)[None]
    cond_p = c_ctx[None, :]
    new_re, new_im = [], []
    for i in range(DEPTH):
        kind, j = i % N_MIXERS, i // N_MIXERS
        mp = adaln(cond_p, w_mod[i], b_mod[i])
        ms = adaln(c, w_mod[i], b_mod[i])
        hp = modulate(rms_norm(xp, g_mix[i]), mp[0], mp[1])
        hs = modulate(rms_norm(xs, g_mix[i]), ms[0], ms[1])
        if kind == 0:
            sp = (ssm_w_in[j], ssm_lam_re[j], ssm_lam_im[j], ssm_log_dt[j], ssm_b_re[j], ssm_b_im[j],
                  ssm_c_re[j], ssm_c_im[j], ssm_d[j], ssm_w_out[j])
            op, (fr, fi), (br, bi) = ssm_mixer(hp, *sp, None, None)
            os_, _, _ = ssm_mixer(hs, *sp, state_ssm_re[:, j], state_ssm_im[:, j])
            new_re.append(jnp.stack([fr[:, -1], br[:, 0]], axis=1))
            new_im.append(jnp.stack([fi[:, -1], bi[:, 0]], axis=1))
        elif kind == 1:
            op = gmlp_mixer(hp, gmlp_w_in[j], gmlp_w_s[j], gmlp_b_s[j], gmlp_w_out[j])
            os_ = gmlp_mixer(hs, gmlp_w_in[j], gmlp_w_s[j], gmlp_b_s[j], gmlp_w_out[j])
        else:
            op = conv_mixer(hp, conv_w_in[j], conv_w[j], conv_w_out[j])
            os_ = conv_mixer(hs, conv_w_in[j], conv_w[j], conv_w_out[j])
        xp = xp + mp[2] * op.astype(xp.dtype)
        xs = xs + ms[2] * os_.astype(xs.dtype)
        xp = xp + mp[5] * sqrelu_ffn(modulate(rms_norm(xp, g_ffn[i]), mp[3], mp[4]), ffn_w1[i], ffn_w2[i])
        xs = xs + ms[5] * sqrelu_ffn(modulate(rms_norm(xs, g_ffn[i]), ms[3], ms[4]), ffn_w1[i], ffn_w2[i])
    y_prompt = rms_norm(xp, g_final)
    y_sample = rms_norm(xs, g_final)
    return (y_prompt, y_sample, jnp.stack(new_re, axis=1), jnp.stack(new_im, axis=1))
```

```python
import functools
import math

import jax
import jax.numpy as jnp
from jax import lax
from jax.experimental import pallas as pl
from jax.experimental.pallas import tpu as pltpu

EPS = 1e-6
N_MOD = 6
GRID_W = 64
SSM_GROUP = 16
SSM_CHUNK = 16
GMLP_CHUNK = 128
CONV_WIDTH = 3
SUBLANES = 8
VMEM_LIMIT = 56 * 1024 * 1024

F32 = jnp.float32
BF16 = jnp.bfloat16


def _cparams(n_axes=1, vmem=None):
    return pltpu.CompilerParams(dimension_semantics=("arbitrary",) * n_axes, vmem_limit_bytes=vmem)


def _gelu(x):
    return 0.5 * x * (1.0 + jnp.tanh(math.sqrt(2.0 / math.pi) * (x + 0.044715 * (x * x * x))))


def _norm_mod(x, g, shift, scale):
    y = x * lax.rsqrt(jnp.mean(x * x, axis=-1, keepdims=True) + EPS)
    return (y * g) * (1.0 + scale) + shift


def _cmul(ar, ai, br, bi):
    return ar * br - ai * bi, ar * bi + ai * br


def _resident(shape):
    nd = len(shape)
    return pl.BlockSpec(shape, lambda *_: (0,) * nd, pipeline_mode=pl.Buffered(1))


class _Tokens:
    def __init__(self, n_prompt_rows, n_sample_rows, sample_len, tm):
        assert n_prompt_rows % tm == 0 and sample_len % tm == 0
        self.tm = tm
        self.n_rows = n_prompt_rows + n_sample_rows
        self.n_blocks = self.n_rows // tm
        self.prompt_blocks = n_prompt_rows // tm
        self.blocks_per_sample = sample_len // tm

    def cond(self, i):
        return jnp.where(i < self.prompt_blocks, 0, 1 + (i - self.prompt_blocks) // self.blocks_per_sample)


def _mod_spec(tok, layer, d):
    return pl.BlockSpec((None, None, N_MOD, d), lambda i: (layer, tok.cond(i), 0, 0))


def _row_spec(tok, d):
    return pl.BlockSpec((tok.tm, d), lambda i: (i, 0))


def _prep_kernel(xp_ref, xs_ref, pos_ref, o_ref, *, prompt_blocks):
    i = pl.program_id(0)

    @pl.when(i < prompt_blocks)
    def _():
        o_ref[...] = xp_ref[...]

    @pl.when(i >= prompt_blocks)
    def _():
        o_ref[...] = xs_ref[...] + pos_ref[...]


def _grid_pos_embed(n_tokens, d):
    rows = n_tokens // GRID_W
    t = jnp.arange(rows * GRID_W)
    r = (t // GRID_W).astype(F32)
    col = (t % GRID_W).astype(F32)
    quarter = d // 4
    freq = 1.0 / (10000.0 ** (jnp.arange(quarter, dtype=F32) / quarter))
    ar = r[:, None] * freq
    ac = col[:, None] * freq
    return jnp.concatenate([jnp.sin(ar), jnp.cos(ar), jnp.sin(ac), jnp.cos(ac)], axis=-1)


def _prep(xp2, xs2, pos, tok):
    d = xp2.shape[1]
    pb, bps = tok.prompt_blocks, tok.blocks_per_sample
    return pl.pallas_call(
        functools.partial(_prep_kernel, prompt_blocks=pb),
        grid=(tok.n_blocks,),
        in_specs=[
            pl.BlockSpec((tok.tm, d), lambda i: (jnp.minimum(i, pb - 1), 0)),
            pl.BlockSpec((tok.tm, d), lambda i: (jnp.maximum(i - pb, 0), 0)),
            pl.BlockSpec((tok.tm, d), lambda i: (jnp.maximum(i - pb, 0) % bps, 0)),
        ],
        out_specs=_row_spec(tok, d),
        out_shape=jax.ShapeDtypeStruct((tok.n_rows, d), F32),
        compiler_params=_cparams(),
        name="prep_tokens",
    )(xp2, xs2, pos)


def _adaln_kernel(ct_ref, w_ref, b_ref, o_ref, *, n_cond):
    ct = ct_ref[...]
    s = ct * jax.nn.sigmoid(ct)
    w = w_ref[...]
    rows = [jnp.sum(s[:, r:r + 1] * w, axis=0, keepdims=True) for r in range(n_cond)]
    rows += [jnp.zeros_like(rows[0])] * (SUBLANES - n_cond)
    o_ref[...] = jnp.concatenate(rows, axis=0) + b_ref[...]


def _adaln(cond_t, w_mod, b_mod, n_cond, tn=512):
    depth, d, n = w_mod.shape
    return pl.pallas_call(
        functools.partial(_adaln_kernel, n_cond=n_cond),
        grid=(depth, n // tn),
        in_specs=[
            pl.BlockSpec((d, SUBLANES), lambda l, j: (0, 0)),
            pl.BlockSpec((None, d, tn), lambda l, j: (l, 0, j)),
            pl.BlockSpec((None, 1, tn), lambda l, j: (l, 0, j)),
        ],
        out_specs=pl.BlockSpec((None, SUBLANES, tn), lambda l, j: (l, 0, j)),
        out_shape=jax.ShapeDtypeStruct((depth, SUBLANES, n), F32),
        compiler_params=_cparams(2),
        name="adaln",
    )(cond_t, w_mod, b_mod.reshape(depth, 1, n))


def _ffn_kernel(x_ref, mod_ref, g_ref, w1_ref, w2_ref, o_ref, *, th):
    x = x_ref[...]
    mod = mod_ref[...]
    hn = _norm_mod(x, g_ref[...], mod[3:4], mod[4:5]).astype(BF16)
    acc = jnp.zeros(x.shape, F32)
    for c in range(w1_ref.shape[1] // th):
        h1 = jnp.dot(hn, w1_ref[:, c * th:(c + 1) * th], preferred_element_type=F32)
        h1 = jnp.square(jnp.maximum(h1, 0.0)).astype(BF16)
        acc = acc + jnp.dot(h1, w2_ref[c * th:(c + 1) * th, :], preferred_element_type=F32)
    o_ref[...] = x + mod[5:6] * acc


def _ffn(x, mods, layer, g, w1, w2, tok, th=1024):
    d = x.shape[1]
    return pl.pallas_call(
        functools.partial(_ffn_kernel, th=th),
        grid=(tok.n_blocks,),
        in_specs=[_row_spec(tok, d), _mod_spec(tok, layer, d), _resident((1, d)),
                  _resident(w1.shape), _resident(w2.shape)],
        out_specs=_row_spec(tok, d),
        out_shape=jax.ShapeDtypeStruct(x.shape, F32),
        compiler_params=_cparams(vmem=VMEM_LIMIT),
        name="ffn",
    )(x, mods, g, w1, w2)


def _ssm_in_kernel(x_ref, mod_ref, g_ref, w_ref, o_ref):
    mod = mod_ref[...]
    hn = _norm_mod(x_ref[...], g_ref[...], mod[0:1], mod[1:2]).astype(BF16)
    o_ref[...] = jnp.dot(hn, w_ref[...], preferred_element_type=F32)


def _ssm_in(x, mods, layer, g, w, tok):
    d = x.shape[1]
    return pl.pallas_call(
        _ssm_in_kernel,
        grid=(tok.n_blocks,),
        in_specs=[_row_spec(tok, d), _mod_spec(tok, layer, d), _resident((1, d)), _resident(w.shape)],
        out_specs=_row_spec(tok, w.shape[1]),
        out_shape=jax.ShapeDtypeStruct((tok.n_rows, w.shape[1]), F32),
        compiler_params=_cparams(vmem=VMEM_LIMIT),
        name="ssm_in",
    )(x, mods, g, w)


def _ssm_core_kernel(prow_ref, bt_ref, ct_ref, h0r_ref, h0i_ref, up_ref, us_ref,
                     yp_ref, ys_ref, fr_ref, fi_ref, p_scr, sa_scr, sb_scr,
                     *, p_rows, p_chunks, s_rows, s_chunks):
    nk = SSM_CHUNK
    half = prow_ref.shape[1] // 2
    prow = prow_ref[...]
    lr, li = prow[0:1], prow[1:2]
    dt = jnp.exp(prow[2:3])
    mag = jnp.exp(lr * dt)
    ar, ai = mag * jnp.cos(li * dt), mag * jnp.sin(li * dt)
    den = lr * lr + li * li
    nr, ni = _cmul(ar - 1.0, ai, lr, -li)
    bt = bt_ref[...]
    bbr, bbi = _cmul(nr / den, ni / den, bt[0:nk], bt[nk:2 * nk])
    ct = ct_ref[...]
    ctr, cti = ct[0:nk], ct[nk:2 * nk]

    pr, pi = [jnp.ones_like(ar)], [jnp.zeros_like(ar)]
    for _ in range(nk):
        r, i = _cmul(pr[-1], pi[-1], ar, ai)
        pr.append(r)
        pi.append(i)
    is_fwd = lax.broadcasted_iota(jnp.int32, (1, 2 * half), 1) < half

    def pw(kf, kb):
        return jnp.where(is_fwd, pr[kf], pr[kb]), jnp.where(is_fwd, pi[kf], pi[kb])

    wp_r, wp_i, ca_r, ca_i, wc_r, wc_i = [], [], [], [], [], []
    for t in range(nk):
        r, i = _cmul(bbr, bbi, *pw(nk - 1 - t, t))
        wp_r.append(r)
        wp_i.append(i)
        r, i = _cmul(ctr, cti, *pw(t, nk - 1 - t))
        ca_r.append(r)
        ca_i.append(i)
        r, i = _cmul(ctr, cti, *pw(t + 1, nk - t))
        wc_r.append(r)
        wc_i.append(-i)
    cat = jnp.concatenate
    wp = cat([cat(wp_r, 0), cat(wp_i, 0)], axis=1).astype(BF16)
    ca = cat([cat(ca_r, 0), cat(ca_i, 0)], axis=1)
    wct = cat([cat(wc_r, 0), cat(wc_i, 0)], axis=1).astype(BF16)

    zero = jnp.zeros_like(bbr)
    lhs = cat([cat([jnp.where(is_fwd, bbr, zero), jnp.where(is_fwd, -bbi, zero)], 1),
               cat([jnp.where(is_fwd, zero, bbr), jnp.where(is_fwd, zero, -bbi)], 1)], 0)
    nt = (((1,), (1,)), ((), ()))
    kk = lax.dot_general(lhs, ca, nt, precision=lax.Precision.HIGHEST, preferred_element_type=F32)
    kf, kb = kk[0:nk], kk[nk:2 * nk]
    width = nk * SSM_GROUP
    lane = lax.broadcasted_iota(jnp.int32, (nk, width), 1)
    blocks = []
    for s in range(nk):
        f = kf if s == 0 else pltpu.roll(kf, SSM_GROUP * s, 1)
        sh = (SSM_GROUP * (s + 1)) % width
        b = kb if sh == 0 else pltpu.roll(kb, sh, 1)
        blocks.append(jnp.where(lane >= SSM_GROUP * s, f, 0.0) + jnp.where(lane < SSM_GROUP * (s + 1), b, 0.0))
    tg = cat(blocks, 0).astype(BF16)

    a_r, a_i = pr[nk], pi[nk]
    lane_s = lax.broadcasted_iota(jnp.int32, (1, 4 * half), 1)
    fwd_s = (lane_s % (2 * half)) < half
    w2 = 2 * half

    def run(u_ref, y_ref, rows, chunks, cur_r, cur_i, unroll):
        u = u_ref[...]
        p_scr[...] = jnp.dot(u, wp, preferred_element_type=F32)

        def step(j, carry):
            c_r, c_i = carry
            lo, hi = j * rows, (chunks - 1 - j) * rows
            if not isinstance(j, int):
                lo, hi = pl.multiple_of(lo, rows), pl.multiple_of(hi, rows)
            sa_scr[pl.ds(lo, rows), 0:w2] = c_r
            sa_scr[pl.ds(lo, rows), w2:2 * w2] = c_i
            sb_scr[pl.ds(hi, rows), 0:w2] = c_r
            sb_scr[pl.ds(hi, rows), w2:2 * w2] = c_i
            q_r = jnp.where(is_fwd, p_scr[pl.ds(lo, rows), 0:w2], p_scr[pl.ds(hi, rows), 0:w2])
            q_i = jnp.where(is_fwd, p_scr[pl.ds(lo, rows), w2:2 * w2], p_scr[pl.ds(hi, rows), w2:2 * w2])
            n_r, n_i = _cmul(a_r, a_i, c_r, c_i)
            return n_r + q_r, n_i + q_i

        if unroll is True:
            for j in range(chunks):
                cur_r, cur_i = step(j, (cur_r, cur_i))
        else:
            cur_r, cur_i = lax.fori_loop(0, chunks, step, (cur_r, cur_i), unroll=unroll)
        s = jnp.where(fwd_s, sa_scr[...], sb_scr[...]).astype(BF16)
        y_ref[...] = (jnp.dot(u, tg, preferred_element_type=F32)
                      + lax.dot_general(s, wct, nt, preferred_element_type=F32))
        return cur_r, cur_i

    z = jnp.zeros((p_rows, w2), F32)
    f_r, f_i = run(up_ref, yp_ref, p_rows, p_chunks, z, z, True)
    fr_ref[...] = f_r
    fi_ref[...] = f_i
    run(us_ref, ys_ref, s_rows, s_chunks, h0r_ref[...], h0i_ref[...], 8)


def _ssm_core(prow, bt, ct, h0r, h0i, up, us, p_rows, s_rows):
    groups, rows_total, width = up.shape
    assert us.shape == up.shape
    p_chunks, s_chunks = rows_total // p_rows, rows_total // s_rows
    lanes = prow.shape[-1]

    def gspec(r, c):
        return pl.BlockSpec((None, r, c), lambda g: (g, 0, 0))

    return pl.pallas_call(
        functools.partial(_ssm_core_kernel, p_rows=p_rows, p_chunks=p_chunks, s_rows=s_rows, s_chunks=s_chunks),
        grid=(groups,),
        in_specs=[gspec(SUBLANES, lanes), gspec(2 * SSM_GROUP, lanes), gspec(2 * SSM_GROUP, lanes),
                  gspec(s_rows, lanes), gspec(s_rows, lanes), gspec(rows_total, width), gspec(rows_total, width)],
        out_specs=[gspec(rows_total, width), gspec(rows_total, width), gspec(p_rows, lanes), gspec(p_rows, lanes)],
        out_shape=[jax.ShapeDtypeStruct(up.shape, F32), jax.ShapeDtypeStruct(us.shape, F32),
                   jax.ShapeDtypeStruct((groups, p_rows, lanes), F32),
                   jax.ShapeDtypeStruct((groups, p_rows, lanes), F32)],
        scratch_shapes=[pltpu.VMEM((rows_total, 2 * lanes), F32)] * 3,
        compiler_params=_cparams(),
        name="ssm_core",
    )(prow, bt, ct, h0r, h0i, up, us)


def _ssm_out_kernel(x_ref, y_ref, u_ref, mod_ref, d_ref, w_ref, o_ref):
    d = x_ref.shape[1]
    z = _gelu(y_ref[...] + d_ref[...] * u_ref[...]).astype(BF16)
    ag = jnp.dot(z, w_ref[...], preferred_element_type=F32)
    o_ref[...] = x_ref[...] + mod_ref[2:3, :] * (ag[:, :d] * jax.nn.sigmoid(ag[:, d:]))


def _ssm_out(x, y, u, mods, layer, dskip, w, tok):
    d = x.shape[1]
    return pl.pallas_call(
        _ssm_out_kernel,
        grid=(tok.n_blocks,),
        in_specs=[_row_spec(tok, d), _row_spec(tok, d), _row_spec(tok, d), _mod_spec(tok, layer, d),
                  _resident((1, d)), _resident(w.shape)],
        out_specs=_row_spec(tok, d),
        out_shape=jax.ShapeDtypeStruct(x.shape, F32),
        compiler_params=_cparams(vmem=VMEM_LIMIT),
        name="ssm_out",
    )(x, y, u, mods, dskip, w)


def _ssm_layer(x, mods, layer, g_mix, w_in, lam_re, lam_im, log_dt, b_re, b_im, c_re, c_im, d_skip, w_out,
               h0_re, h0_im, tok, n_prompt, len_prompt, n_sample, len_sample):
    width = w_in.shape[1]
    groups = width // SSM_GROUP
    n_state = lam_re.shape[-1]
    nk = SSM_CHUNK
    u = _ssm_in(x, mods, layer, g_mix, w_in.astype(BF16), tok)

    s_rows = SUBLANES
    n_p = n_prompt * len_prompt

    def to_groups(v, nb, ln, pad_to):
        v = v.reshape(nb, ln // nk, nk, groups, SSM_GROUP).transpose(3, 1, 0, 2, 4)
        if pad_to > nb:
            v = jnp.pad(v, ((0, 0), (0, 0), (0, pad_to - nb), (0, 0), (0, 0)))
        return v.reshape(groups, (ln // nk) * pad_to, nk * SSM_GROUP)

    def from_groups(v, nb, ln, padded):
        v = v.reshape(groups, ln // nk, padded, nk, SSM_GROUP)[:, :, :nb]
        return v.transpose(2, 1, 3, 0, 4).reshape(nb * ln, width)

    up = to_groups(u[:n_p].astype(BF16), n_prompt, len_prompt, n_prompt)
    us = to_groups(u[n_p:].astype(BF16), n_sample, len_sample, s_rows)
    assert up.shape == us.shape

    def lanes_dir_state(v):
        return v.transpose(1, 0, 2).reshape(groups, 2 * n_state)

    prow = jnp.stack([lanes_dir_state(lam_re), lanes_dir_state(lam_im),
                      lanes_dir_state(jnp.broadcast_to(log_dt[..., None], lam_re.shape))], axis=1)
    prow = jnp.pad(prow, ((0, 0), (0, SUBLANES - 3), (0, 0)))
    bt = jnp.concatenate([b_re.transpose(1, 3, 0, 2).reshape(groups, SSM_GROUP, 2 * n_state),
                          b_im.transpose(1, 3, 0, 2).reshape(groups, SSM_GROUP, 2 * n_state)], axis=1)
    ct = jnp.concatenate([c_re.transpose(1, 2, 0, 3).reshape(groups, SSM_GROUP, 2 * n_state),
                          c_im.transpose(1, 2, 0, 3).reshape(groups, SSM_GROUP, 2 * n_state)], axis=1)

    def h0_rows(h):
        h = h.transpose(2, 0, 1, 3).reshape(groups, n_sample, 2 * n_state)
        return jnp.pad(h, ((0, 0), (0, s_rows - n_sample), (0, 0)))

    yp, ys, f_re, f_im = _ssm_core(prow, bt, ct, h0_rows(h0_re), h0_rows(h0_im), up, us, n_prompt, s_rows)
    y = jnp.concatenate([from_groups(yp, n_prompt, len_prompt, n_prompt),
                         from_groups(ys, n_sample, len_sample, s_rows)], axis=0)
    x = _ssm_out(x, y, u, mods, layer, d_skip.reshape(1, width), w_out.astype(BF16), tok)

    def final(f):
        return f.reshape(groups, n_prompt, 2, n_state).transpose(1, 2, 0, 3)

    return x, final(f_re), final(f_im)


def _gmlp_kernel(x_ref, mod_ref, g_ref, win_ref, ws_ref, bs_ref, wout_ref, o_ref, t_scr):
    x = x_ref[...]
    mod = mod_ref[...]
    hn = _norm_mod(x, g_ref[...], mod[0:1], mod[1:2]).astype(BF16)
    z = _gelu(jnp.dot(hn, win_ref[...], preferred_element_type=F32))
    wdt = z.shape[1] // 2
    u, v = z[:, :wdt], z[:, wdt:]
    vc = v - jnp.mean(v, axis=-1, keepdims=True)
    vn = (vc * lax.rsqrt(jnp.mean(vc * vc, axis=-1, keepdims=True) + EPS)).astype(BF16)
    n_groups, chunk, gd = bs_ref.shape
    for c in range(x.shape[0] // chunk):
        rows = slice(c * chunk, (c + 1) * chunk)
        for g in range(n_groups):
            cols = slice(g * gd, (g + 1) * gd)
            s = jnp.dot(ws_ref[g], vn[rows, cols], preferred_element_type=F32) + bs_ref[g]
            t_scr[rows, cols] = (u[rows, cols] * s).astype(BF16)
    out = jnp.dot(t_scr[...], wout_ref[...], preferred_element_type=F32)
    o_ref[...] = x + mod[2:3] * out


def _gmlp_layer(x, mods, layer, g_mix, w_in, w_s, b_s, w_out, tok):
    d = x.shape[1]
    n_groups, chunk, _ = w_s.shape
    gd = w_out.shape[0] // n_groups
    bs = jnp.broadcast_to(b_s[:, :, None], (n_groups, chunk, gd))
    return pl.pallas_call(
        _gmlp_kernel,
        grid=(tok.n_blocks,),
        in_specs=[_row_spec(tok, d), _mod_spec(tok, layer, d), _resident((1, d)), _resident(w_in.shape),
                  _resident(w_s.shape), _resident(bs.shape), _resident(w_out.shape)],
        out_specs=_row_spec(tok, d),
        out_shape=jax.ShapeDtypeStruct(x.shape, F32),
        scratch_shapes=[pltpu.VMEM((tok.tm, w_out.shape[0]), BF16)],
        compiler_params=_cparams(vmem=VMEM_LIMIT),
        name="gmlp",
    )(x, mods, g_mix, w_in.astype(BF16), w_s.astype(BF16), bs, w_out.astype(BF16))


def _conv_kernel(x_ref, xprev_ref, xnext_ref, mod_ref, g_ref, win_ref, cw_ref, wout_ref, o_ref,
                 *, prompt_blocks, blocks_per_prompt, blocks_per_sample):
    i = pl.program_id(0)
    x = x_ref[...]
    tm, d = x.shape
    mod = mod_ref[...]
    g = g_ref[...]
    hn = _norm_mod(x, g, mod[0:1], mod[1:2]).astype(BF16)
    z = jnp.dot(hn, win_ref[...], preferred_element_type=F32)
    gb, t = z[:, :d], z[:, d:2 * d] * z[:, 2 * d:]
    xh = jnp.concatenate([xprev_ref[...], xnext_ref[...]], axis=0)
    hh = _norm_mod(xh, g, mod[0:1], mod[1:2]).astype(BF16)
    zh = jnp.dot(hh, win_ref[:, d:], preferred_element_type=F32)
    th = zh[:, :d] * zh[:, d:]
    pos_in_seq = jnp.where(i < prompt_blocks, i % blocks_per_prompt, (i - prompt_blocks) % blocks_per_sample)
    last = jnp.where(i < prompt_blocks, blocks_per_prompt, blocks_per_sample) - 1
    t_prev = jnp.where(pos_in_seq > 0, th[SUBLANES - 1:SUBLANES], 0.0)
    t_next = jnp.where(pos_in_seq < last, th[SUBLANES:SUBLANES + 1], 0.0)
    row = lax.broadcasted_iota(jnp.int32, (tm, 1), 0)
    up = jnp.where(row == 0, t_prev, pltpu.roll(t, 1, 0))
    dn = jnp.where(row == tm - 1, t_next, pltpu.roll(t, tm - 1, 0))
    cw = cw_ref[...]
    y = cw[0:1] * up + cw[1:2] * t + cw[2:3] * dn
    out = jnp.dot((gb * y).astype(BF16), wout_ref[...], preferred_element_type=F32)
    o_ref[...] = x + mod[2:3] * out


def _conv_layer(x, mods, layer, g_mix, w_in, conv_w, w_out, tok, len_prompt, len_sample):
    d = x.shape[1]
    tm = tok.tm
    per8 = tm // SUBLANES
    last8 = tok.n_rows // SUBLANES - 1
    cw = jnp.pad(conv_w, ((0, SUBLANES - CONV_WIDTH), (0, 0)))
    return pl.pallas_call(
        functools.partial(_conv_kernel, prompt_blocks=tok.prompt_blocks,
                          blocks_per_prompt=len_prompt // tm, blocks_per_sample=len_sample // tm),
        grid=(tok.n_blocks,),
        in_specs=[_row_spec(tok, d),
                  pl.BlockSpec((SUBLANES, d), lambda i: (jnp.maximum(i * per8 - 1, 0), 0)),
                  pl.BlockSpec((SUBLANES, d), lambda i: (jnp.minimum((i + 1) * per8, last8), 0)),
                  _mod_spec(tok, layer, d), _resident((1, d)), _resident(w_in.shape),
                  _resident(cw.shape), _resident(w_out.shape)],
        out_specs=_row_spec(tok, d),
        out_shape=jax.ShapeDtypeStruct(x.shape, F32),
        compiler_params=_cparams(vmem=VMEM_LIMIT),
        name="conv_mixer",
    )(x, x, x, mods, g_mix, w_in.astype(BF16), cw, w_out.astype(BF16))


def _final_kernel(x_ref, g_ref, o_ref):
    x = x_ref[...]
    o_ref[...] = (x * lax.rsqrt(jnp.mean(x * x, axis=-1, keepdims=True) + EPS)) * g_ref[...]


def _final_norm(x, g, first_block, n_blocks, tm):
    d = x.shape[1]
    return pl.pallas_call(
        _final_kernel,
        grid=(n_blocks,),
        in_specs=[pl.BlockSpec((tm, d), lambda i: (i + first_block, 0)), _resident((1, d))],
        out_specs=pl.BlockSpec((tm, d), lambda i: (i, 0)),
        out_shape=jax.ShapeDtypeStruct((n_blocks * tm, d), F32),
        compiler_params=_cparams(),
        name="final_norm",
    )(x, g)


def kernel(x_prompt, x_sample, state_ssm_re, state_ssm_im, c, c_ctx, w_mod, b_mod, g_mix, g_ffn, ffn_w1, ffn_w2, ssm_w_in, ssm_lam_re, ssm_lam_im, ssm_log_dt, ssm_b_re, ssm_b_im, ssm_c_re, ssm_c_im, ssm_d, ssm_w_out, gmlp_w_in, gmlp_w_s, gmlp_b_s, gmlp_w_out, conv_w_in, conv_w, conv_w_out, g_final):
    n_prompt, len_prompt, d = x_prompt.shape
    n_sample, len_sample, _ = x_sample.shape
    depth = w_mod.shape[0]
    n_mixers = 3
    tok = _Tokens(n_prompt * len_prompt, n_sample * len_sample, len_sample, 512)
    tok_seq = _Tokens(n_prompt * len_prompt, n_sample * len_sample, len_sample, 256)
    assert len_prompt % tok_seq.tm == 0 and 1 + n_sample <= SUBLANES

    x = _prep(x_prompt.reshape(-1, d), x_sample.reshape(-1, d), _grid_pos_embed(len_sample, d), tok)

    cond = jnp.concatenate([c_ctx[None, :], c, jnp.zeros((SUBLANES - 1 - n_sample, d), F32)], axis=0)
    mods = _adaln(cond.T, w_mod, b_mod, 1 + n_sample).reshape(depth, SUBLANES, N_MOD, d)

    new_re, new_im = [], []
    for i in range(depth):
        kind, j = i % n_mixers, i // n_mixers
        gm = g_mix[i].reshape(1, d)
        if kind == 0:
            x, f_re, f_im = _ssm_layer(
                x, mods, i, gm, ssm_w_in[j], ssm_lam_re[j], ssm_lam_im[j], ssm_log_dt[j], ssm_b_re[j], ssm_b_im[j],
                ssm_c_re[j], ssm_c_im[j], ssm_d[j], ssm_w_out[j], state_ssm_re[:, j], state_ssm_im[:, j],
                tok, n_prompt, len_prompt, n_sample, len_sample)
            new_re.append(f_re)
            new_im.append(f_im)
        elif kind == 1:
            x = _gmlp_layer(x, mods, i, gm, gmlp_w_in[j], gmlp_w_s[j], gmlp_b_s[j], gmlp_w_out[j], tok_seq)
        else:
            x = _conv_layer(x, mods, i, gm, conv_w_in[j], conv_w[j], conv_w_out[j], tok_seq, len_prompt, len_sample)
        x = _ffn(x, mods, i, g_ffn[i].reshape(1, d), ffn_w1[i].astype(BF16), ffn_w2[i].astype(BF16), tok)

    y_prompt = _final_norm(x, g_final.reshape(1, d), 0, tok.prompt_blocks, tok.tm).reshape(x_prompt.shape)
    y_sample = _final_norm(x, g_final.reshape(1, d), tok.prompt_blocks, tok.n_blocks - tok.prompt_blocks,
                           tok.tm).reshape(x_sample.shape)
    return (y_prompt, y_sample, jnp.stack(new_re, axis=1), jnp.stack(new_im, axis=1))
```

```python
import functools
import math

import jax
import jax.numpy as jnp
from jax import lax
from jax.experimental import pallas as pl
from jax.experimental.pallas import tpu as pltpu

EPS = 1e-6
N_MOD = 6
GRID_W = 64
SSM_GROUP = 16
SSM_CHUNK = 16
CONV_WIDTH = 3
SUBLANES = 8
LANES = 128
ROW_BLOCK = 512
SEQ_CHUNKS = 16
VMEM_LIMIT = 56 * 1024 * 1024

F32 = jnp.float32
BF16 = jnp.bfloat16
NT = (((1,), (1,)), ((), ()))


def _cparams(n_axes=1, vmem=VMEM_LIMIT):
    return pltpu.CompilerParams(dimension_semantics=("arbitrary",) * n_axes, vmem_limit_bytes=vmem)


def _gelu(x):
    return 0.5 * x * (1.0 + jnp.tanh(math.sqrt(2.0 / math.pi) * (x + 0.044715 * (x * x * x))))


def _norm_mod(x, g, shift, scale):
    y = x * lax.rsqrt(jnp.mean(x * x, axis=-1, keepdims=True) + EPS)
    return (y * g) * (1.0 + scale) + shift


def _cmul(ar, ai, br, bi):
    return ar * br - ai * bi, ar * bi + ai * br


def _resident(shape):
    nd = len(shape)
    return pl.BlockSpec(shape, lambda *_: (0,) * nd, pipeline_mode=pl.Buffered(1))


class _Stream:
    def __init__(self, n_chunks, n_batch, per_batch_cond):
        self.nc, self.nb = n_chunks, n_batch
        self.n_rows = n_chunks * n_batch * SSM_CHUNK
        self.per_batch_cond = per_batch_cond
        self.row_chunks = ROW_BLOCK // SSM_CHUNK
        self.seq_blocks = n_chunks // SEQ_CHUNKS
        assert n_chunks % SEQ_CHUNKS == 0 and self.n_rows % ROW_BLOCK == 0
        assert not per_batch_cond or n_chunks % self.row_chunks == 0

    def row_grid(self):
        return (self.nb, self.nc // self.row_chunks) if self.per_batch_cond else (self.n_rows // ROW_BLOCK,)

    def row_view(self, x):
        return x if self.per_batch_cond else x.reshape(self.n_rows, x.shape[-1])

    def row_spec(self, width):
        if self.per_batch_cond:
            return pl.BlockSpec((self.row_chunks, None, SSM_CHUNK, width), lambda b, j: (j, b, 0, 0))
        return pl.BlockSpec((ROW_BLOCK, width), lambda i: (i, 0))

    def row_shape(self, width):
        return (self.nc, self.nb, SSM_CHUNK, width) if self.per_batch_cond else (self.n_rows, width)

    def unview(self, x):
        return x.reshape(self.nc, self.nb, SSM_CHUNK, x.shape[-1])

    def seq_grid(self):
        return (self.nb, self.seq_blocks)

    def seq_spec(self, width):
        return pl.BlockSpec((SEQ_CHUNKS, None, SSM_CHUNK, width), lambda b, j: (j, b, 0, 0))

    def halo_specs(self, width):
        halves = SSM_CHUNK // SUBLANES
        shape = (None, None, None, SUBLANES, width)
        prev = pl.BlockSpec(shape, lambda b, j: (jnp.maximum(j * SEQ_CHUNKS - 1, 0), b, halves - 1, 0, 0))
        nxt = pl.BlockSpec(shape, lambda b, j: (jnp.minimum((j + 1) * SEQ_CHUNKS, self.nc - 1), b, 0, 0, 0))
        return prev, nxt

    def halo_view(self, x):
        return x.reshape(self.nc, self.nb, SSM_CHUNK // SUBLANES, SUBLANES, x.shape[-1])

    def mod_spec(self, layer, d):
        if self.per_batch_cond:
            return pl.BlockSpec((None, None, N_MOD, d), lambda b, j: (layer, 1 + b, 0, 0))
        return pl.BlockSpec((None, None, N_MOD, d), lambda *_: (layer, 0, 0, 0))

    def n_axes(self, seq=False):
        return 2 if (seq or self.per_batch_cond) else 1


def _grid_pos_embed(n_tokens, d):
    rows = n_tokens // GRID_W
    t = jnp.arange(rows * GRID_W)
    r = (t // GRID_W).astype(F32)
    col = (t % GRID_W).astype(F32)
    quarter = d // 4
    freq = 1.0 / (10000.0 ** (jnp.arange(quarter, dtype=F32) / quarter))
    ar = r[:, None] * freq
    ac = col[:, None] * freq
    return jnp.concatenate([jnp.sin(ar), jnp.cos(ar), jnp.sin(ac), jnp.cos(ac)], axis=-1)


def _copy_kernel(x_ref, o_ref):
    o_ref[...] = x_ref[...]


def _add_kernel(x_ref, p_ref, o_ref):
    o_ref[...] = x_ref[...] + p_ref[...]


def _to_chunk_major(x, st, pos=None):
    d = x.shape[-1]
    xv = x.reshape(st.nb, st.nc, SSM_CHUNK, d)
    blk = (SEQ_CHUNKS, SSM_CHUNK, d)
    in_specs = [pl.BlockSpec((None,) + blk, lambda b, j: (b, j, 0, 0))]
    args = [xv]
    if pos is not None:
        in_specs.append(pl.BlockSpec(blk, lambda b, j: (j, 0, 0)))
        args.append(pos.reshape(st.nc, SSM_CHUNK, d))
    return pl.pallas_call(
        _copy_kernel if pos is None else _add_kernel,
        grid=st.seq_grid(), in_specs=in_specs, out_specs=st.seq_spec(d),
        out_shape=jax.ShapeDtypeStruct((st.nc, st.nb, SSM_CHUNK, d), F32),
        compiler_params=_cparams(2), name="to_chunk_major",
    )(*args)


def _final_kernel(x_ref, g_ref, o_ref):
    x = x_ref[...]
    o_ref[...] = (x * lax.rsqrt(jnp.mean(x * x, axis=-1, keepdims=True) + EPS)) * g_ref[...]


def _final_norm(x, g, st, out_shape):
    d = x.shape[-1]
    blk = (SEQ_CHUNKS, SSM_CHUNK, d)
    return pl.pallas_call(
        _final_kernel,
        grid=st.seq_grid(),
        in_specs=[st.seq_spec(d), _resident((1, 1, d))],
        out_specs=pl.BlockSpec((None,) + blk, lambda b, j: (b, j, 0, 0)),
        out_shape=jax.ShapeDtypeStruct((st.nb, st.nc, SSM_CHUNK, d), F32),
        compiler_params=_cparams(2), name="final_norm",
    )(x, g.reshape(1, 1, d)).reshape(out_shape)


def _adaln_kernel(ct_ref, w_ref, b_ref, o_ref, *, n_cond):
    ct = ct_ref[...]
    s = ct * jax.nn.sigmoid(ct)
    w = w_ref[...]
    rows = [jnp.sum(s[:, r:r + 1] * w, axis=0, keepdims=True) for r in range(n_cond)]
    rows += [jnp.zeros_like(rows[0])] * (SUBLANES - n_cond)
    o_ref[...] = jnp.concatenate(rows, axis=0) + b_ref[...]


def _adaln(cond_t, w_mod, b_mod, n_cond, tn=512):
    depth, d, n = w_mod.shape
    return pl.pallas_call(
        functools.partial(_adaln_kernel, n_cond=n_cond),
        grid=(depth, n // tn),
        in_specs=[
            pl.BlockSpec((d, SUBLANES), lambda l, j: (0, 0)),
            pl.BlockSpec((None, d, tn), lambda l, j: (l, 0, j)),
            pl.BlockSpec((None, 1, tn), lambda l, j: (l, 0, j)),
        ],
        out_specs=pl.BlockSpec((None, SUBLANES, tn), lambda l, j: (l, 0, j)),
        out_shape=jax.ShapeDtypeStruct((depth, SUBLANES, n), F32),
        compiler_params=_cparams(2), name="adaln",
    )(cond_t, w_mod, b_mod.reshape(depth, 1, n))


def _rows(ref):
    return ref[...].reshape(-1, ref.shape[-1])


def _ffn_kernel(x_ref, mod_ref, g_ref, w1_ref, w2_ref, o_ref, *, th):
    x = _rows(x_ref)
    mod = mod_ref[...]
    hn = _norm_mod(x, g_ref[...], mod[3:4], mod[4:5]).astype(BF16)
    acc = jnp.zeros(x.shape, F32)
    for c in range(w1_ref.shape[1] // th):
        h1 = jnp.dot(hn, w1_ref[:, c * th:(c + 1) * th], preferred_element_type=F32)
        h1 = jnp.square(jnp.maximum(h1, 0.0)).astype(BF16)
        acc = acc + jnp.dot(h1, w2_ref[c * th:(c + 1) * th, :], preferred_element_type=F32)
    o_ref[...] = (x + mod[5:6] * acc).reshape(o_ref.shape)


def _ffn(x, st, mods, layer, g, w1, w2, th=1024):
    d = x.shape[-1]
    out = pl.pallas_call(
        functools.partial(_ffn_kernel, th=th),
        grid=st.row_grid(),
        in_specs=[st.row_spec(d), st.mod_spec(layer, d), _resident((1, d)), _resident(w1.shape), _resident(w2.shape)],
        out_specs=st.row_spec(d),
        out_shape=jax.ShapeDtypeStruct(st.row_shape(d), F32),
        compiler_params=_cparams(st.n_axes()), name="ffn",
    )(st.row_view(x), mods, g, w1, w2)
    return st.unview(out)


def _ssm_in_kernel(x_ref, mod_ref, g_ref, w_ref, o_ref):
    mod = mod_ref[...]
    hn = _norm_mod(_rows(x_ref), g_ref[...], mod[0:1], mod[1:2]).astype(BF16)
    o_ref[...] = jnp.dot(hn, w_ref[...], preferred_element_type=F32).reshape(o_ref.shape)


def _ssm_in(x, st, mods, layer, g, w):
    d, width = x.shape[-1], w.shape[1]
    out = pl.pallas_call(
        _ssm_in_kernel,
        grid=st.row_grid(),
        in_specs=[st.row_spec(d), st.mod_spec(layer, d), _resident((1, d)), _resident(w.shape)],
        out_specs=st.row_spec(width),
        out_shape=jax.ShapeDtypeStruct(st.row_shape(width), F32),
        compiler_params=_cparams(st.n_axes()), name="ssm_in",
    )(st.row_view(x), mods, g, w)
    return out.reshape(st.n_rows, width)


def _ssm_out_kernel(x_ref, y_ref, mod_ref, w_ref, o_ref):
    d = x_ref.shape[-1]
    z = _gelu(_rows(y_ref)).astype(BF16)
    ag = jnp.dot(z, w_ref[...], preferred_element_type=F32)
    o_ref[...] = (_rows(x_ref) + mod_ref[2:3, :] * (ag[:, :d] * jax.nn.sigmoid(ag[:, d:]))).reshape(o_ref.shape)


def _ssm_out(x, y, st, mods, layer, w):
    d = x.shape[-1]
    out = pl.pallas_call(
        _ssm_out_kernel,
        grid=st.row_grid(),
        in_specs=[st.row_spec(d), st.row_spec(d), st.mod_spec(layer, d), _resident(w.shape)],
        out_specs=st.row_spec(d),
        out_shape=jax.ShapeDtypeStruct(st.row_shape(d), F32),
        compiler_params=_cparams(st.n_axes()), name="ssm_out",
    )(st.row_view(x), st.row_view(st.unview(y)), mods, w)
    return st.unview(out)


def _ssm_operators(prow, bt, ct):
    nk = SSM_CHUNK
    half = prow.shape[1] // 2
    lr, li = prow[0:1], prow[1:2]
    dt = jnp.exp(prow[2:3])
    mag = jnp.exp(lr * dt)
    ar, ai = mag * jnp.cos(li * dt), mag * jnp.sin(li * dt)
    den = lr * lr + li * li
    nr, ni = _cmul(ar - 1.0, ai, lr, -li)
    bbr, bbi = _cmul(nr / den, ni / den, bt[0:nk], bt[nk:2 * nk])
    ctr, cti = ct[0:nk], ct[nk:2 * nk]

    pr, pi = [jnp.ones_like(ar)], [jnp.zeros_like(ar)]
    for _ in range(nk):
        r, i = _cmul(pr[-1], pi[-1], ar, ai)
        pr.append(r)
        pi.append(i)
    is_fwd = lax.broadcasted_iota(jnp.int32, (1, 2 * half), 1) < half

    def pw(kf, kb):
        return jnp.where(is_fwd, pr[kf], pr[kb]), jnp.where(is_fwd, pi[kf], pi[kb])

    wp_r, wp_i, ca_r, ca_i, wc_r, wc_i = [], [], [], [], [], []
    for t in range(nk):
        r, i = _cmul(bbr, bbi, *pw(nk - 1 - t, t))
        wp_r.append(r)
        wp_i.append(i)
        r, i = _cmul(ctr, cti, *pw(t, nk - 1 - t))
        ca_r.append(r)
        ca_i.append(i)
        r, i = _cmul(ctr, cti, *pw(t + 1, nk - t))
        wc_r.append(r)
        wc_i.append(-i)
    cat = jnp.concatenate
    wp = cat([cat(wp_r, 0), cat(wp_i, 0)], axis=1).astype(BF16)
    ca = cat([cat(ca_r, 0), cat(ca_i, 0)], axis=1)
    wct = cat([cat(wc_r, 0), cat(wc_i, 0)], axis=1).astype(BF16)

    zero = jnp.zeros_like(bbr)
    lhs = cat([cat([jnp.where(is_fwd, bbr, zero), jnp.where(is_fwd, -bbi, zero)], 1),
               cat([jnp.where(is_fwd, zero, bbr), jnp.where(is_fwd, zero, -bbi)], 1)], 0)
    kk = lax.dot_general(lhs, ca, NT, precision=lax.Precision.HIGHEST, preferred_element_type=F32)
    width = nk * SSM_GROUP
    lane = lax.broadcasted_iota(jnp.int32, (nk, width), 1)
    row = lax.broadcasted_iota(jnp.int32, (nk, width), 0)
    d_lanes = cat([prow[3:4], jnp.zeros_like(prow[3:4])], axis=1)
    kf = kk[0:nk] + jnp.where(lane == row, d_lanes, 0.0)
    kb = kk[nk:2 * nk]
    blocks = []
    for s in range(nk):
        f = kf if s == 0 else pltpu.roll(kf, SSM_GROUP * s, 1)
        sh = (SSM_GROUP * (s + 1)) % width
        b = kb if sh == 0 else pltpu.roll(kb, sh, 1)
        blocks.append(jnp.where(lane >= SSM_GROUP * s, f, 0.0) + jnp.where(lane < SSM_GROUP * (s + 1), b, 0.0))
    tg = cat(blocks, 0).astype(BF16)
    return tg, wp, wct, pr[nk], pi[nk]


def _ssm_core_kernel(prow_ref, bt_ref, ct_ref, h0r_ref, h0i_ref, up_ref, us_ref,
                     yp_ref, ys_ref, fr_ref, fi_ref, atp_scr, ats_scr, p_scr, q_scr, sa_scr, sb_scr,
                     *, p_batch, s_batch):
    nk = SSM_CHUNK
    n_groups = LANES // SSM_GROUP
    cb_p, cb_s = atp_scr.shape[2], ats_scr.shape[2]
    w2 = prow_ref.shape[2]
    for t in range(nk):
        atp_scr[t] = up_ref[pl.ds(t, cb_p, stride=nk), :].T
        ats_scr[t] = us_ref[pl.ds(t, cb_s, stride=nk), :].T

    lane1 = lax.broadcasted_iota(jnp.int32, (1, w2), 1)
    is_fwd = lane1 < (w2 // 2)
    fwd_s = (lax.broadcasted_iota(jnp.int32, (1, 2 * w2), 1) % w2) < (w2 // 2)

    def scan(n_tiles, rows, src_scr, cur_r, cur_i, m_r, m_i, unroll):
        def step(j, carry):
            c_r, c_i = carry
            lo, hi = j * rows, (n_tiles - 1 - j) * rows
            if not isinstance(j, int):
                lo, hi = pl.multiple_of(lo, rows), pl.multiple_of(hi, rows)
            sa_scr[pl.ds(lo, rows), 0:w2] = c_r
            sa_scr[pl.ds(lo, rows), w2:2 * w2] = c_i
            sb_scr[pl.ds(hi, rows), 0:w2] = c_r
            sb_scr[pl.ds(hi, rows), w2:2 * w2] = c_i
            s_r = jnp.where(is_fwd, src_scr[pl.ds(lo, rows), 0:w2], src_scr[pl.ds(hi, rows), 0:w2])
            s_i = jnp.where(is_fwd, src_scr[pl.ds(lo, rows), w2:2 * w2], src_scr[pl.ds(hi, rows), w2:2 * w2])
            n_r, n_i = _cmul(m_r, m_i, c_r, c_i)
            return n_r + s_r, n_i + s_i

        if unroll is True:
            carry = (cur_r, cur_i)
            for j in range(n_tiles):
                carry = step(j, carry)
            return carry
        return lax.fori_loop(0, n_tiles, step, (cur_r, cur_i), unroll=unroll)

    def group(g, _):
        r0 = pl.multiple_of(g * SSM_GROUP, SSM_GROUP)
        tg, wp, wct, a_r, a_i = _ssm_operators(prow_ref[g], bt_ref[g], ct_ref[g])

        def mix(at_scr, cb, carry_fn):
            a = at_scr[:, pl.ds(r0, SSM_GROUP), :].reshape(nk * SSM_GROUP, cb)
            u = a.T.astype(BF16)
            p_scr[0:cb, :] = jnp.dot(u, wp, preferred_element_type=F32)
            fin = carry_fn(cb)
            s = jnp.where(fwd_s, sa_scr[0:cb, :], sb_scr[0:cb, :]).astype(BF16)
            y = jnp.dot(u, tg, preferred_element_type=F32) + lax.dot_general(s, wct, NT, preferred_element_type=F32)
            at_scr[:, pl.ds(r0, SSM_GROUP), :] = y.T.reshape(nk, SSM_GROUP, cb)
            return fin

        def prompt_carry(cb):
            z = jnp.zeros((p_batch, w2), F32)
            return scan(cb // p_batch, p_batch, p_scr, z, z, a_r, a_i, True)

        def sample_carry(cb):
            assert 2 * s_batch == SUBLANES
            p_r, p_i = p_scr[0:cb, 0:w2], p_scr[0:cb, w2:2 * w2]
            ap_r, ap_i = _cmul(a_r, a_i, p_r, p_i)
            q_scr[0:cb, 0:w2] = jnp.where(is_fwd, pltpu.roll(p_r, cb - s_batch, 0), pltpu.roll(p_r, s_batch, 0)) + ap_r
            q_scr[0:cb, w2:2 * w2] = jnp.where(is_fwd, pltpu.roll(p_i, cb - s_batch, 0), pltpu.roll(p_i, s_batch, 0)) + ap_i
            h_r, h_i = h0r_ref[g], h0i_ref[g]
            row = lax.broadcasted_iota(jnp.int32, (SUBLANES, w2), 0)
            keep = jnp.where(is_fwd, 1, 0) == jnp.where(row < s_batch, 1, 0)
            e_r = pltpu.roll(jnp.where(is_fwd, p_r[0:SUBLANES], p_r[cb - SUBLANES:cb]), s_batch, 0)
            e_i = pltpu.roll(jnp.where(is_fwd, p_i[0:SUBLANES], p_i[cb - SUBLANES:cb]), s_batch, 0)
            ah_r, ah_i = _cmul(a_r, a_i, h_r, h_i)
            c_r = jnp.where(keep, h_r, ah_r + e_r)
            c_i = jnp.where(keep, h_i, ah_i + e_i)
            a2_r, a2_i = _cmul(a_r, a_i, a_r, a_i)
            return scan(cb // SUBLANES, SUBLANES, q_scr, c_r, c_i, a2_r, a2_i, 8)

        f_r, f_i = mix(atp_scr, cb_p, prompt_carry)
        fr_ref[g] = f_r
        fi_ref[g] = f_i
        mix(ats_scr, cb_s, sample_carry)
        return 0

    lax.fori_loop(0, n_groups, group, 0)
    for t in range(nk):
        yp_ref[pl.ds(t, cb_p, stride=nk), :] = atp_scr[t].T
        ys_ref[pl.ds(t, cb_s, stride=nk), :] = ats_scr[t].T


def _ssm_core(prow, bt, ct, h0r, h0i, up, us, p_batch, s_batch):
    groups, _, w2 = prow.shape
    gpb = LANES // SSM_GROUP
    np_rows, ns_rows = up.shape[0], us.shape[0]
    cb_p, cb_s = np_rows // SSM_CHUNK, ns_rows // SSM_CHUNK
    cb_max = max(cb_p, cb_s)

    def gspec(r, c):
        return pl.BlockSpec((gpb, r, c), lambda o: (o, 0, 0))

    def lane_spec(n):
        return pl.BlockSpec((n, LANES), lambda o: (0, o))

    return pl.pallas_call(
        functools.partial(_ssm_core_kernel, p_batch=p_batch, s_batch=s_batch),
        grid=(groups // gpb,),
        in_specs=[gspec(SUBLANES, w2), gspec(2 * SSM_GROUP, w2), gspec(2 * SSM_GROUP, w2),
                  gspec(SUBLANES, w2), gspec(SUBLANES, w2), lane_spec(np_rows), lane_spec(ns_rows)],
        out_specs=[lane_spec(np_rows), lane_spec(ns_rows), gspec(p_batch, w2), gspec(p_batch, w2)],
        out_shape=[jax.ShapeDtypeStruct(up.shape, F32), jax.ShapeDtypeStruct(us.shape, F32),
                   jax.ShapeDtypeStruct((groups, p_batch, w2), F32),
                   jax.ShapeDtypeStruct((groups, p_batch, w2), F32)],
        scratch_shapes=[pltpu.VMEM((SSM_CHUNK, LANES, cb_p), F32), pltpu.VMEM((SSM_CHUNK, LANES, cb_s), F32)]
        + [pltpu.VMEM((cb_max, 2 * w2), F32)] * 4,
        compiler_params=_cparams(), name="ssm_core",
    )(prow, bt, ct, h0r, h0i, up, us)


def _ssm_layer(xp, xs, stp, sts, mods, layer, g_mix, w_in, lam_re, lam_im, log_dt, b_re, b_im, c_re, c_im, d_skip,
               w_out, h0_re, h0_im):
    width = w_in.shape[1]
    groups = width // SSM_GROUP
    n_state = lam_re.shape[-1]
    w_in = w_in.astype(BF16)
    up = _ssm_in(xp, stp, mods, layer, g_mix, w_in)
    us = _ssm_in(xs, sts, mods, layer, g_mix, w_in)

    def lanes_dir_state(v):
        return v.transpose(1, 0, 2).reshape(groups, 2 * n_state)

    d_rows = jnp.pad(d_skip.reshape(groups, SSM_GROUP), ((0, 0), (0, 2 * n_state - SSM_GROUP)))
    prow = jnp.stack([lanes_dir_state(lam_re), lanes_dir_state(lam_im),
                      lanes_dir_state(jnp.broadcast_to(log_dt[..., None], lam_re.shape)), d_rows], axis=1)
    prow = jnp.pad(prow, ((0, 0), (0, SUBLANES - 4), (0, 0)))
    bt = jnp.concatenate([b_re.transpose(1, 3, 0, 2).reshape(groups, SSM_GROUP, 2 * n_state),
                          b_im.transpose(1, 3, 0, 2).reshape(groups, SSM_GROUP, 2 * n_state)], axis=1)
    ct = jnp.concatenate([c_re.transpose(1, 2, 0, 3).reshape(groups, SSM_GROUP, 2 * n_state),
                          c_im.transpose(1, 2, 0, 3).reshape(groups, SSM_GROUP, 2 * n_state)], axis=1)

    def h0_rows(h):
        h = h.transpose(2, 0, 1, 3).reshape(groups, sts.nb, 2 * n_state)
        return jnp.concatenate([h] * (SUBLANES // sts.nb), axis=1)

    yp, ys, f_re, f_im = _ssm_core(prow, bt, ct, h0_rows(h0_re), h0_rows(h0_im), up, us, stp.nb, sts.nb)
    w_out = w_out.astype(BF16)
    xp = _ssm_out(xp, yp, stp, mods, layer, w_out)
    xs = _ssm_out(xs, ys, sts, mods, layer, w_out)

    def final(f):
        return f.reshape(groups, stp.nb, 2, n_state).transpose(1, 2, 0, 3)

    return xp, xs, final(f_re), final(f_im)


def _gmlp_kernel(x_ref, mod_ref, g_ref, win_ref, ws_ref, bs_ref, wout_ref, o_ref, t_scr):
    x = _rows(x_ref)
    mod = mod_ref[...]
    hn = _norm_mod(x, g_ref[...], mod[0:1], mod[1:2]).astype(BF16)
    z = _gelu(jnp.dot(hn, win_ref[...], preferred_element_type=F32))
    wdt = z.shape[1] // 2
    u, v = z[:, :wdt], z[:, wdt:]
    vc = v - jnp.mean(v, axis=-1, keepdims=True)
    vn = (vc * lax.rsqrt(jnp.mean(vc * vc, axis=-1, keepdims=True) + EPS)).astype(BF16)
    n_groups, chunk, gd = bs_ref.shape
    for c in range(x.shape[0] // chunk):
        rows = slice(c * chunk, (c + 1) * chunk)
        for g in range(n_groups):
            cols = slice(g * gd, (g + 1) * gd)
            s = jnp.dot(ws_ref[g], vn[rows, cols], preferred_element_type=F32) + bs_ref[g]
            t_scr[rows, cols] = (u[rows, cols] * s).astype(BF16)
    out = jnp.dot(t_scr[...], wout_ref[...], preferred_element_type=F32)
    o_ref[...] = (x + mod[2:3] * out).reshape(o_ref.shape)


def _gmlp_layer(x, st, mods, layer, g_mix, w_in, w_s, bs, w_out):
    d = x.shape[-1]
    return pl.pallas_call(
        _gmlp_kernel,
        grid=st.seq_grid(),
        in_specs=[st.seq_spec(d), st.mod_spec(layer, d), _resident((1, d)), _resident(w_in.shape),
                  _resident(w_s.shape), _resident(bs.shape), _resident(w_out.shape)],
        out_specs=st.seq_spec(d),
        out_shape=jax.ShapeDtypeStruct(x.shape, F32),
        scratch_shapes=[pltpu.VMEM((SEQ_CHUNKS * SSM_CHUNK, w_out.shape[0]), BF16)],
        compiler_params=_cparams(2), name="gmlp",
    )(x, mods, g_mix, w_in, w_s, bs, w_out)


def _conv_kernel(x_ref, xprev_ref, xnext_ref, mod_ref, g_ref, win_ref, cw_ref, wout_ref, o_ref, *, seq_blocks):
    j = pl.program_id(1)
    x = _rows(x_ref)
    tm, d = x.shape
    mod = mod_ref[...]
    g = g_ref[...]
    hn = _norm_mod(x, g, mod[0:1], mod[1:2]).astype(BF16)
    z = jnp.dot(hn, win_ref[...], preferred_element_type=F32)
    gb, t = z[:, :d], z[:, d:2 * d] * z[:, 2 * d:]
    xh = jnp.concatenate([xprev_ref[...], xnext_ref[...]], axis=0)
    hh = _norm_mod(xh, g, mod[0:1], mod[1:2]).astype(BF16)
    zh = jnp.dot(hh, win_ref[:, d:], preferred_element_type=F32)
    th = zh[:, :d] * zh[:, d:]
    t_prev = jnp.where(j > 0, th[SUBLANES - 1:SUBLANES], 0.0)
    t_next = jnp.where(j < seq_blocks - 1, th[SUBLANES:SUBLANES + 1], 0.0)
    row = lax.broadcasted_iota(jnp.int32, (tm, 1), 0)
    up = jnp.where(row == 0, t_prev, pltpu.roll(t, 1, 0))
    dn = jnp.where(row == tm - 1, t_next, pltpu.roll(t, tm - 1, 0))
    cw = cw_ref[...]
    y = cw[0:1] * up + cw[1:2] * t + cw[2:3] * dn
    out = jnp.dot((gb * y).astype(BF16), wout_ref[...], preferred_element_type=F32)
    o_ref[...] = (x + mod[2:3] * out).reshape(o_ref.shape)


def _conv_layer(x, st, mods, layer, g_mix, w_in, cw, w_out):
    d = x.shape[-1]
    prev, nxt = st.halo_specs(d)
    xh = st.halo_view(x)
    return pl.pallas_call(
        functools.partial(_conv_kernel, seq_blocks=st.seq_blocks),
        grid=st.seq_grid(),
        in_specs=[st.seq_spec(d), prev, nxt, st.mod_spec(layer, d), _resident((1, d)), _resident(w_in.shape),
                  _resident(cw.shape), _resident(w_out.shape)],
        out_specs=st.seq_spec(d),
        out_shape=jax.ShapeDtypeStruct(x.shape, F32),
        compiler_params=_cparams(2), name="conv_mixer",
    )(x, xh, xh, mods, g_mix, w_in, cw, w_out)


def kernel(x_prompt, x_sample, state_ssm_re, state_ssm_im, c, c_ctx, w_mod, b_mod, g_mix, g_ffn, ffn_w1, ffn_w2, ssm_w_in, ssm_lam_re, ssm_lam_im, ssm_log_dt, ssm_b_re, ssm_b_im, ssm_c_re, ssm_c_im, ssm_d, ssm_w_out, gmlp_w_in, gmlp_w_s, gmlp_b_s, gmlp_w_out, conv_w_in, conv_w, conv_w_out, g_final):
    n_prompt, len_prompt, d = x_prompt.shape
    n_sample, len_sample, _ = x_sample.shape
    depth = w_mod.shape[0]
    n_mixers = 3
    assert 1 + n_sample <= SUBLANES
    stp = _Stream(len_prompt // SSM_CHUNK, n_prompt, per_batch_cond=False)
    sts = _Stream(len_sample // SSM_CHUNK, n_sample, per_batch_cond=True)

    xp = _to_chunk_major(x_prompt, stp)
    xs = _to_chunk_major(x_sample, sts, _grid_pos_embed(len_sample, d))

    cond = jnp.concatenate([c_ctx[None, :], c, jnp.zeros((SUBLANES - 1 - n_sample, d), F32)], axis=0)
    mods = _adaln(cond.T, w_mod, b_mod, 1 + n_sample).reshape(depth, SUBLANES, N_MOD, d)

    new_re, new_im = [], []
    for i in range(depth):
        kind, j = i % n_mixers, i // n_mixers
        gm = g_mix[i].reshape(1, d)
        if kind == 0:
            xp, xs, f_re, f_im = _ssm_layer(
                xp, xs, stp, sts, mods, i, gm, ssm_w_in[j], ssm_lam_re[j], ssm_lam_im[j], ssm_log_dt[j],
                ssm_b_re[j], ssm_b_im[j], ssm_c_re[j], ssm_c_im[j], ssm_d[j], ssm_w_out[j],
                state_ssm_re[:, j], state_ssm_im[:, j])
            new_re.append(f_re)
            new_im.append(f_im)
        elif kind == 1:
            n_groups, chunk, _ = gmlp_w_s[j].shape
            gd = gmlp_w_out[j].shape[0] // n_groups
            bs = jnp.broadcast_to(gmlp_b_s[j][:, :, None], (n_groups, chunk, gd))
            ws = (gmlp_w_in[j].astype(BF16), gmlp_w_s[j].astype(BF16), bs, gmlp_w_out[j].astype(BF16))
            xp = _gmlp_layer(xp, stp, mods, i, gm, *ws)
            xs = _gmlp_layer(xs, sts, mods, i, gm, *ws)
        else:
            ws = (conv_w_in[j].astype(BF16), jnp.pad(conv_w[j], ((0, SUBLANES - CONV_WIDTH), (0, 0))),
                  conv_w_out[j].astype(BF16))
            xp = _conv_layer(xp, stp, mods, i, gm, *ws)
            xs = _conv_layer(xs, sts, mods, i, gm, *ws)
        gf, w1, w2 = g_ffn[i].reshape(1, d), ffn_w1[i].astype(BF16), ffn_w2[i].astype(BF16)
        xp = _ffn(xp, stp, mods, i, gf, w1, w2)
        xs = _ffn(xs, sts, mods, i, gf, w1, w2)

    y_prompt = _final_norm(xp, g_final, stp, x_prompt.shape)
    y_sample = _final_norm(xs, g_final, sts, x_sample.shape)
    return (y_prompt, y_sample, jnp.stack(new_re, axis=1), jnp.stack(new_im, axis=1))
```

```python
import functools
import math

import jax
import jax.numpy as jnp
from jax import lax
from jax.experimental import pallas as pl
from jax.experimental.pallas import tpu as pltpu

EPS = 1e-6
N_MOD = 6
GRID_W = 64
SSM_GROUP = 16
SSM_CHUNK = 16
CONV_WIDTH = 3
SUBLANES = 8
LANES = 128
ROW_BLOCK = 512
SEQ_CHUNKS = 16
GROUP_UNROLL = 2
VMEM_LIMIT = 56 * 1024 * 1024

F32 = jnp.float32
BF16 = jnp.bfloat16
NT = (((1,), (1,)), ((), ()))


def _cparams(n_axes=1, vmem=VMEM_LIMIT):
    return pltpu.CompilerParams(dimension_semantics=("arbitrary",) * n_axes, vmem_limit_bytes=vmem)


def _gelu(x):
    return 0.5 * x * (1.0 + jnp.tanh(math.sqrt(2.0 / math.pi) * (x + 0.044715 * (x * x * x))))


def _norm_mod(x, g, shift, scale):
    y = x * lax.rsqrt(jnp.mean(x * x, axis=-1, keepdims=True) + EPS)
    return (y * g) * (1.0 + scale) + shift


def _cmul(ar, ai, br, bi):
    return ar * br - ai * bi, ar * bi + ai * br


def _resident(shape):
    nd = len(shape)
    return pl.BlockSpec(shape, lambda *_: (0,) * nd, pipeline_mode=pl.Buffered(1))


class _Stream:
    def __init__(self, n_chunks, n_batch, per_batch_cond):
        self.nc, self.nb = n_chunks, n_batch
        self.n_rows = n_chunks * n_batch * SSM_CHUNK
        self.per_batch_cond = per_batch_cond
        self.row_chunks = ROW_BLOCK // SSM_CHUNK
        self.seq_blocks = n_chunks // SEQ_CHUNKS
        assert n_chunks % SEQ_CHUNKS == 0 and self.n_rows % ROW_BLOCK == 0
        assert not per_batch_cond or n_chunks % self.row_chunks == 0

    def row_grid(self):
        return (self.nb, self.nc // self.row_chunks) if self.per_batch_cond else (self.n_rows // ROW_BLOCK,)

    def row_view(self, x):
        return x if self.per_batch_cond else x.reshape(self.n_rows, x.shape[-1])

    def row_spec(self, width):
        if self.per_batch_cond:
            return pl.BlockSpec((self.row_chunks, None, SSM_CHUNK, width), lambda b, j: (j, b, 0, 0))
        return pl.BlockSpec((ROW_BLOCK, width), lambda i: (i, 0))

    def row_shape(self, width):
        return (self.nc, self.nb, SSM_CHUNK, width) if self.per_batch_cond else (self.n_rows, width)

    def unview(self, x):
        return x.reshape(self.nc, self.nb, SSM_CHUNK, x.shape[-1])

    def nat_view(self, x):
        return x.reshape(self.nb, self.nc, SSM_CHUNK, x.shape[-1])

    def nat_spec(self, width):
        if self.per_batch_cond:
            return pl.BlockSpec((None, self.row_chunks, SSM_CHUNK, width), lambda b, j: (b, j, 0, 0))
        assert self.nb * SSM_CHUNK == ROW_BLOCK
        return pl.BlockSpec((self.nb, None, SSM_CHUNK, width), lambda i: (0, i, 0, 0))

    def pos_spec(self, width):
        return pl.BlockSpec((self.row_chunks, SSM_CHUNK, width), lambda b, j: (j, 0, 0))

    def seq_grid(self):
        return (self.nb, self.seq_blocks)

    def seq_spec(self, width):
        return pl.BlockSpec((SEQ_CHUNKS, None, SSM_CHUNK, width), lambda b, j: (j, b, 0, 0))

    def halo_specs(self, width):
        halves = SSM_CHUNK // SUBLANES
        shape = (None, None, None, SUBLANES, width)
        prev = pl.BlockSpec(shape, lambda b, j: (jnp.maximum(j * SEQ_CHUNKS - 1, 0), b, halves - 1, 0, 0))
        nxt = pl.BlockSpec(shape, lambda b, j: (jnp.minimum((j + 1) * SEQ_CHUNKS, self.nc - 1), b, 0, 0, 0))
        return prev, nxt

    def halo_view(self, x):
        return x.reshape(self.nc, self.nb, SSM_CHUNK // SUBLANES, SUBLANES, x.shape[-1])

    def mod_spec(self, layer, d):
        if self.per_batch_cond:
            return pl.BlockSpec((None, None, N_MOD, d), lambda b, j: (layer, 1 + b, 0, 0))
        return pl.BlockSpec((None, None, N_MOD, d), lambda *_: (layer, 0, 0, 0))

    def n_axes(self, seq=False):
        return 2 if (seq or self.per_batch_cond) else 1


def _grid_pos_embed(n_tokens, d):
    rows = n_tokens // GRID_W
    t = jnp.arange(rows * GRID_W)
    r = (t // GRID_W).astype(F32)
    col = (t % GRID_W).astype(F32)
    quarter = d // 4
    freq = 1.0 / (10000.0 ** (jnp.arange(quarter, dtype=F32) / quarter))
    ar = r[:, None] * freq
    ac = col[:, None] * freq
    return jnp.concatenate([jnp.sin(ar), jnp.cos(ar), jnp.sin(ac), jnp.cos(ac)], axis=-1)


def _adaln_kernel(ct_ref, w_ref, b_ref, o_ref, *, n_cond):
    ct = ct_ref[...]
    s = ct * jax.nn.sigmoid(ct)
    w = w_ref[...]
    rows = [jnp.sum(s[:, r:r + 1] * w, axis=0, keepdims=True) for r in range(n_cond)]
    rows += [jnp.zeros_like(rows[0])] * (SUBLANES - n_cond)
    o_ref[...] = jnp.concatenate(rows, axis=0) + b_ref[...]


def _adaln(cond_t, w_mod, b_mod, n_cond, tn=512):
    depth, d, n = w_mod.shape
    return pl.pallas_call(
        functools.partial(_adaln_kernel, n_cond=n_cond),
        grid=(depth, n // tn),
        in_specs=[
            pl.BlockSpec((d, SUBLANES), lambda l, j: (0, 0)),
            pl.BlockSpec((None, d, tn), lambda l, j: (l, 0, j)),
            pl.BlockSpec((None, 1, tn), lambda l, j: (l, 0, j)),
        ],
        out_specs=pl.BlockSpec((None, SUBLANES, tn), lambda l, j: (l, 0, j)),
        out_shape=jax.ShapeDtypeStruct((depth, SUBLANES, n), F32),
        compiler_params=_cparams(2), name="adaln",
    )(cond_t, w_mod, b_mod.reshape(depth, 1, n))


def _rows(ref):
    return ref[...].reshape(-1, ref.shape[-1])


def _rms(x, g):
    return (x * lax.rsqrt(jnp.mean(x * x, axis=-1, keepdims=True) + EPS)) * g


def _ffn_kernel(x_ref, mod_ref, g_ref, w1_ref, w2_ref, *rest, th):
    o_ref = rest[-1]
    x = _rows(x_ref)
    mod = mod_ref[...]
    hn = _norm_mod(x, g_ref[...], mod[3:4], mod[4:5]).astype(BF16)
    acc = jnp.zeros(x.shape, F32)
    for c in range(w1_ref.shape[1] // th):
        h1 = jnp.dot(hn, w1_ref[:, c * th:(c + 1) * th], preferred_element_type=F32)
        h1 = jnp.square(jnp.maximum(h1, 0.0)).astype(BF16)
        acc = acc + jnp.dot(h1, w2_ref[c * th:(c + 1) * th, :], preferred_element_type=F32)
    out = x + mod[5:6] * acc
    if len(rest) == 2:
        out = _rms(out, rest[0][...])
    o_ref[...] = out.reshape(o_ref.shape)


def _ffn(x, st, mods, layer, g, w1, w2, g_final=None, th=1024):
    d = x.shape[-1]
    last = g_final is not None
    out = pl.pallas_call(
        functools.partial(_ffn_kernel, th=th),
        grid=st.row_grid(),
        in_specs=[st.row_spec(d), st.mod_spec(layer, d), _resident((1, d)), _resident(w1.shape), _resident(w2.shape)]
        + ([_resident((1, d))] if last else []),
        out_specs=st.nat_spec(d) if last else st.row_spec(d),
        out_shape=jax.ShapeDtypeStruct((st.nb, st.nc, SSM_CHUNK, d) if last else st.row_shape(d), F32),
        compiler_params=_cparams(st.n_axes()), name="ffn",
    )(st.row_view(x), mods, g, w1, w2, *([g_final] if last else []))
    return out if last else st.unview(out)


def _x_operands(x, st, pos):
    d = x.shape[-1]
    if x.ndim == 4:
        return [st.row_spec(d)], [st.row_view(x)]
    specs, args = [st.nat_spec(d)], [st.nat_view(x)]
    if pos is not None:
        specs.append(st.pos_spec(d))
        args.append(pos.reshape(st.nc, SSM_CHUNK, d))
    return specs, args


def _ssm_in_kernel(x_ref, *rest):
    *pos_ref, mod_ref, g_ref, w_ref, o_ref = rest
    x = _rows(x_ref) + _rows(pos_ref[0]) if pos_ref else _rows(x_ref)
    mod = mod_ref[...]
    hn = _norm_mod(x, g_ref[...], mod[0:1], mod[1:2]).astype(BF16)
    o_ref[...] = jnp.dot(hn, w_ref[...], preferred_element_type=F32).reshape(o_ref.shape)


def _ssm_in(x, st, mods, layer, g, w, pos=None):
    d, width = x.shape[-1], w.shape[1]
    x_specs, x_args = _x_operands(x, st, pos)
    out = pl.pallas_call(
        _ssm_in_kernel,
        grid=st.row_grid(),
        in_specs=x_specs + [st.mod_spec(layer, d), _resident((1, d)), _resident(w.shape)],
        out_specs=st.row_spec(width),
        out_shape=jax.ShapeDtypeStruct(st.row_shape(width), F32),
        compiler_params=_cparams(st.n_axes()), name="ssm_in",
    )(*x_args, mods, g, w)
    return out.reshape(st.n_rows, width)


def _ssm_out_kernel(x_ref, *rest):
    *pos_ref, y_ref, mod_ref, w_ref, o_ref = rest
    x = _rows(x_ref) + _rows(pos_ref[0]) if pos_ref else _rows(x_ref)
    d = x.shape[-1]
    z = _gelu(_rows(y_ref)).astype(BF16)
    ag = jnp.dot(z, w_ref[...], preferred_element_type=F32)
    o_ref[...] = (x + mod_ref[2:3, :] * (ag[:, :d] * jax.nn.sigmoid(ag[:, d:]))).reshape(o_ref.shape)


def _ssm_out(x, y, st, mods, layer, w, pos=None):
    d = x.shape[-1]
    x_specs, x_args = _x_operands(x, st, pos)
    out = pl.pallas_call(
        _ssm_out_kernel,
        grid=st.row_grid(),
        in_specs=x_specs + [st.row_spec(d), st.mod_spec(layer, d), _resident(w.shape)],
        out_specs=st.row_spec(d),
        out_shape=jax.ShapeDtypeStruct(st.row_shape(d), F32),
        compiler_params=_cparams(st.n_axes()), name="ssm_out",
    )(*x_args, st.row_view(st.unview(y)), mods, w)
    return st.unview(out)


def _ssm_operators(prow, bt, ct):
    nk = SSM_CHUNK
    half = prow.shape[1] // 2
    lr, li = prow[0:1], prow[1:2]
    dt = jnp.exp(prow[2:3])
    mag = jnp.exp(lr * dt)
    ar, ai = mag * jnp.cos(li * dt), mag * jnp.sin(li * dt)
    den = lr * lr + li * li
    nr, ni = _cmul(ar - 1.0, ai, lr, -li)
    bbr, bbi = _cmul(nr / den, ni / den, bt[0:nk], bt[nk:2 * nk])
    ctr, cti = ct[0:nk], ct[nk:2 * nk]

    pr, pi = [jnp.ones_like(ar)], [jnp.zeros_like(ar)]
    for _ in range(nk):
        r, i = _cmul(pr[-1], pi[-1], ar, ai)
        pr.append(r)
        pi.append(i)
    is_fwd = lax.broadcasted_iota(jnp.int32, (1, 2 * half), 1) < half

    def pw(kf, kb):
        return jnp.where(is_fwd, pr[kf], pr[kb]), jnp.where(is_fwd, pi[kf], pi[kb])

    wp_r, wp_i, ca_r, ca_i, wc_r, wc_i = [], [], [], [], [], []
    for t in range(nk):
        r, i = _cmul(bbr, bbi, *pw(nk - 1 - t, t))
        wp_r.append(r)
        wp_i.append(i)
        r, i = _cmul(ctr, cti, *pw(t, nk - 1 - t))
        ca_r.append(r)
        ca_i.append(i)
        r, i = _cmul(ctr, cti, *pw(t + 1, nk - t))
        wc_r.append(r)
        wc_i.append(-i)
    cat = jnp.concatenate
    wp = cat([cat(wp_r, 0), cat(wp_i, 0)], axis=1).astype(BF16)
    ca = cat([cat(ca_r, 0), cat(ca_i, 0)], axis=1)
    wct = cat([cat(wc_r, 0), cat(wc_i, 0)], axis=1).astype(BF16)

    zero = jnp.zeros_like(bbr)
    lhs = cat([cat([jnp.where(is_fwd, bbr, zero), jnp.where(is_fwd, -bbi, zero)], 1),
               cat([jnp.where(is_fwd, zero, bbr), jnp.where(is_fwd, zero, -bbi)], 1)], 0)
    kk = lax.dot_general(lhs, ca, NT, precision=lax.Precision.HIGHEST, preferred_element_type=F32)
    width = nk * SSM_GROUP
    lane = lax.broadcasted_iota(jnp.int32, (nk, width), 1)
    row = lax.broadcasted_iota(jnp.int32, (nk, width), 0)
    d_lanes = cat([prow[3:4], jnp.zeros_like(prow[3:4])], axis=1)
    kf = kk[0:nk] + jnp.where(lane == row, d_lanes, 0.0)
    kb = kk[nk:2 * nk]
    blocks = []
    for s in range(nk):
        f = kf if s == 0 else pltpu.roll(kf, SSM_GROUP * s, 1)
        sh = (SSM_GROUP * (s + 1)) % width
        b = kb if sh == 0 else pltpu.roll(kb, sh, 1)
        blocks.append(jnp.where(lane >= SSM_GROUP * s, f, 0.0) + jnp.where(lane < SSM_GROUP * (s + 1), b, 0.0))
    tg = cat(blocks, 0).astype(BF16)
    return tg, wp, wct, pr[nk], pi[nk]


def _ssm_core_kernel(prow_ref, bt_ref, ct_ref, h0r_ref, h0i_ref, up_ref, us_ref,
                     yp_ref, ys_ref, fr_ref, fi_ref, atp_scr, ats_scr, p_scr, q_scr, sa_scr, sb_scr,
                     *, p_batch, s_batch):
    nk = SSM_CHUNK
    n_groups = LANES // SSM_GROUP
    cb_p, cb_s = atp_scr.shape[2], ats_scr.shape[2]
    w2 = prow_ref.shape[2]
    for t in range(nk):
        atp_scr[t] = up_ref[pl.ds(t, cb_p, stride=nk), :].T
        ats_scr[t] = us_ref[pl.ds(t, cb_s, stride=nk), :].T

    lane1 = lax.broadcasted_iota(jnp.int32, (1, w2), 1)
    is_fwd = lane1 < (w2 // 2)
    fwd_s = (lax.broadcasted_iota(jnp.int32, (1, 2 * w2), 1) % w2) < (w2 // 2)

    def scan(n_tiles, rows, src_scr, cur_r, cur_i, m_r, m_i, unroll):
        def step(j, carry):
            c_r, c_i = carry
            lo, hi = j * rows, (n_tiles - 1 - j) * rows
            if not isinstance(j, int):
                lo, hi = pl.multiple_of(lo, rows), pl.multiple_of(hi, rows)
            sa_scr[pl.ds(lo, rows), 0:w2] = c_r
            sa_scr[pl.ds(lo, rows), w2:2 * w2] = c_i
            sb_scr[pl.ds(hi, rows), 0:w2] = c_r
            sb_scr[pl.ds(hi, rows), w2:2 * w2] = c_i
            s_r = jnp.where(is_fwd, src_scr[pl.ds(lo, rows), 0:w2], src_scr[pl.ds(hi, rows), 0:w2])
            s_i = jnp.where(is_fwd, src_scr[pl.ds(lo, rows), w2:2 * w2], src_scr[pl.ds(hi, rows), w2:2 * w2])
            n_r, n_i = _cmul(m_r, m_i, c_r, c_i)
            return n_r + s_r, n_i + s_i

        if unroll is True:
            carry = (cur_r, cur_i)
            for j in range(n_tiles):
                carry = step(j, carry)
            return carry
        return lax.fori_loop(0, n_tiles, step, (cur_r, cur_i), unroll=unroll)

    def group(g, _):
        r0 = pl.multiple_of(g * SSM_GROUP, SSM_GROUP)
        tg, wp, wct, a_r, a_i = _ssm_operators(prow_ref[g], bt_ref[g], ct_ref[g])

        def mix(at_scr, cb, carry_fn):
            a = at_scr[:, pl.ds(r0, SSM_GROUP), :].reshape(nk * SSM_GROUP, cb)
            u = a.T.astype(BF16)
            p_scr[0:cb, :] = jnp.dot(u, wp, preferred_element_type=F32)
            fin = carry_fn(cb)
            s = jnp.where(fwd_s, sa_scr[0:cb, :], sb_scr[0:cb, :]).astype(BF16)
            y = jnp.dot(u, tg, preferred_element_type=F32) + lax.dot_general(s, wct, NT, preferred_element_type=F32)
            at_scr[:, pl.ds(r0, SSM_GROUP), :] = y.T.reshape(nk, SSM_GROUP, cb)
            return fin

        def prompt_carry(cb):
            z = jnp.zeros((p_batch, w2), F32)
            return scan(cb // p_batch, p_batch, p_scr, z, z, a_r, a_i, True)

        def sample_carry(cb):
            assert 2 * s_batch == SUBLANES
            p_r, p_i = p_scr[0:cb, 0:w2], p_scr[0:cb, w2:2 * w2]
            ap_r, ap_i = _cmul(a_r, a_i, p_r, p_i)
            q_scr[0:cb, 0:w2] = jnp.where(is_fwd, pltpu.roll(p_r, cb - s_batch, 0), pltpu.roll(p_r, s_batch, 0)) + ap_r
            q_scr[0:cb, w2:2 * w2] = jnp.where(is_fwd, pltpu.roll(p_i, cb - s_batch, 0), pltpu.roll(p_i, s_batch, 0)) + ap_i
            h_r, h_i = h0r_ref[g], h0i_ref[g]
            row = lax.broadcasted_iota(jnp.int32, (SUBLANES, w2), 0)
            keep = jnp.where(is_fwd, 1, 0) == jnp.where(row < s_batch, 1, 0)
            e_r = pltpu.roll(jnp.where(is_fwd, p_r[0:SUBLANES], p_r[cb - SUBLANES:cb]), s_batch, 0)
            e_i = pltpu.roll(jnp.where(is_fwd, p_i[0:SUBLANES], p_i[cb - SUBLANES:cb]), s_batch, 0)
            ah_r, ah_i = _cmul(a_r, a_i, h_r, h_i)
            c_r = jnp.where(keep, h_r, ah_r + e_r)
            c_i = jnp.where(keep, h_i, ah_i + e_i)
            a2_r, a2_i = _cmul(a_r, a_i, a_r, a_i)
            return scan(cb // SUBLANES, SUBLANES, q_scr, c_r, c_i, a2_r, a2_i, True)

        f_r, f_i = mix(atp_scr, cb_p, prompt_carry)
        fr_ref[g] = f_r
        fi_ref[g] = f_i
        mix(ats_scr, cb_s, sample_carry)
        return 0

    lax.fori_loop(0, n_groups, group, 0, unroll=GROUP_UNROLL)
    for t in range(nk):
        yp_ref[pl.ds(t, cb_p, stride=nk), :] = atp_scr[t].T
        ys_ref[pl.ds(t, cb_s, stride=nk), :] = ats_scr[t].T


def _ssm_core(prow, bt, ct, h0r, h0i, up, us, p_batch, s_batch):
    groups, _, w2 = prow.shape
    gpb = LANES // SSM_GROUP
    np_rows, ns_rows = up.shape[0], us.shape[0]
    cb_p, cb_s = np_rows // SSM_CHUNK, ns_rows // SSM_CHUNK
    cb_max = max(cb_p, cb_s)

    def gspec(r, c):
        return pl.BlockSpec((gpb, r, c), lambda o: (o, 0, 0))

    def lane_spec(n):
        return pl.BlockSpec((n, LANES), lambda o: (0, o))

    return pl.pallas_call(
        functools.partial(_ssm_core_kernel, p_batch=p_batch, s_batch=s_batch),
        grid=(groups // gpb,),
        in_specs=[gspec(SUBLANES, w2), gspec(2 * SSM_GROUP, w2), gspec(2 * SSM_GROUP, w2),
                  gspec(SUBLANES, w2), gspec(SUBLANES, w2), lane_spec(np_rows), lane_spec(ns_rows)],
        out_specs=[lane_spec(np_rows), lane_spec(ns_rows), gspec(p_batch, w2), gspec(p_batch, w2)],
        out_shape=[jax.ShapeDtypeStruct(up.shape, F32), jax.ShapeDtypeStruct(us.shape, F32),
                   jax.ShapeDtypeStruct((groups, p_batch, w2), F32),
                   jax.ShapeDtypeStruct((groups, p_batch, w2), F32)],
        scratch_shapes=[pltpu.VMEM((SSM_CHUNK, LANES, cb_p), F32), pltpu.VMEM((SSM_CHUNK, LANES, cb_s), F32)]
        + [pltpu.VMEM((cb_max, 2 * w2), F32)] * 4,
        compiler_params=_cparams(), name="ssm_core",
    )(prow, bt, ct, h0r, h0i, up, us)


def _ssm_layer(xp, xs, stp, sts, mods, layer, g_mix, w_in, lam_re, lam_im, log_dt, b_re, b_im, c_re, c_im, d_skip,
               w_out, h0_re, h0_im, pos=None):
    width = w_in.shape[1]
    groups = width // SSM_GROUP
    n_state = lam_re.shape[-1]
    w_in = w_in.astype(BF16)
    up = _ssm_in(xp, stp, mods, layer, g_mix, w_in)
    us = _ssm_in(xs, sts, mods, layer, g_mix, w_in, pos)

    def lanes_dir_state(v):
        return v.transpose(1, 0, 2).reshape(groups, 2 * n_state)

    d_rows = jnp.pad(d_skip.reshape(groups, SSM_GROUP), ((0, 0), (0, 2 * n_state - SSM_GROUP)))
    prow = jnp.stack([lanes_dir_state(lam_re), lanes_dir_state(lam_im),
                      lanes_dir_state(jnp.broadcast_to(log_dt[..., None], lam_re.shape)), d_rows], axis=1)
    prow = jnp.pad(prow, ((0, 0), (0, SUBLANES - 4), (0, 0)))
    bt = jnp.concatenate([b_re.transpose(1, 3, 0, 2).reshape(groups, SSM_GROUP, 2 * n_state),
                          b_im.transpose(1, 3, 0, 2).reshape(groups, SSM_GROUP, 2 * n_state)], axis=1)
    ct = jnp.concatenate([c_re.transpose(1, 2, 0, 3).reshape(groups, SSM_GROUP, 2 * n_state),
                          c_im.transpose(1, 2, 0, 3).reshape(groups, SSM_GROUP, 2 * n_state)], axis=1)

    def h0_rows(h):
        h = h.transpose(2, 0, 1, 3).reshape(groups, sts.nb, 2 * n_state)
        return jnp.concatenate([h] * (SUBLANES // sts.nb), axis=1)

    yp, ys, f_re, f_im = _ssm_core(prow, bt, ct, h0_rows(h0_re), h0_rows(h0_im), up, us, stp.nb, sts.nb)
    w_out = w_out.astype(BF16)
    xp = _ssm_out(xp, yp, stp, mods, layer, w_out)
    xs = _ssm_out(xs, ys, sts, mods, layer, w_out, pos)

    def final(f):
        return f.reshape(groups, stp.nb, 2, n_state).transpose(1, 2, 0, 3)

    return xp, xs, final(f_re), final(f_im)


def _gmlp_kernel(x_ref, mod_ref, g_ref, win_ref, ws_ref, bs_ref, wout_ref, o_ref, t_scr):
    x = _rows(x_ref)
    mod = mod_ref[...]
    hn = _norm_mod(x, g_ref[...], mod[0:1], mod[1:2]).astype(BF16)
    z = _gelu(jnp.dot(hn, win_ref[...], preferred_element_type=F32))
    wdt = z.shape[1] // 2
    u, v = z[:, :wdt], z[:, wdt:]
    vc = v - jnp.mean(v, axis=-1, keepdims=True)
    vn = (vc * lax.rsqrt(jnp.mean(vc * vc, axis=-1, keepdims=True) + EPS)).astype(BF16)
    n_groups, chunk, gd = bs_ref.shape
    for c in range(x.shape[0] // chunk):
        rows = slice(c * chunk, (c + 1) * chunk)
        for g in range(n_groups):
            cols = slice(g * gd, (g + 1) * gd)
            s = jnp.dot(ws_ref[g], vn[rows, cols], preferred_element_type=F32) + bs_ref[g]
            t_scr[rows, cols] = (u[rows, cols] * s).astype(BF16)
    out = jnp.dot(t_scr[...], wout_ref[...], preferred_element_type=F32)
    o_ref[...] = (x + mod[2:3] * out).reshape(o_ref.shape)


def _gmlp_layer(x, st, mods, layer, g_mix, w_in, w_s, bs, w_out):
    d = x.shape[-1]
    return pl.pallas_call(
        _gmlp_kernel,
        grid=st.seq_grid(),
        in_specs=[st.seq_spec(d), st.mod_spec(layer, d), _resident((1, d)), _resident(w_in.shape),
                  _resident(w_s.shape), _resident(bs.shape), _resident(w_out.shape)],
        out_specs=st.seq_spec(d),
        out_shape=jax.ShapeDtypeStruct(x.shape, F32),
        scratch_shapes=[pltpu.VMEM((SEQ_CHUNKS * SSM_CHUNK, w_out.shape[0]), BF16)],
        compiler_params=_cparams(2), name="gmlp",
    )(x, mods, g_mix, w_in, w_s, bs, w_out)


def _conv_kernel(x_ref, xprev_ref, xnext_ref, mod_ref, g_ref, win_ref, cw_ref, wout_ref, o_ref, *, seq_blocks):
    j = pl.program_id(1)
    x = _rows(x_ref)
    tm, d = x.shape
    mod = mod_ref[...]
    g = g_ref[...]
    hn = _norm_mod(x, g, mod[0:1], mod[1:2]).astype(BF16)
    z = jnp.dot(hn, win_ref[...], preferred_element_type=F32)
    gb, t = z[:, :d], z[:, d:2 * d] * z[:, 2 * d:]
    xh = jnp.concatenate([xprev_ref[...], xnext_ref[...]], axis=0)
    hh = _norm_mod(xh, g, mod[0:1], mod[1:2]).astype(BF16)
    zh = jnp.dot(hh, win_ref[:, d:], preferred_element_type=F32)
    th = zh[:, :d] * zh[:, d:]
    t_prev = jnp.where(j > 0, th[SUBLANES - 1:SUBLANES], 0.0)
    t_next = jnp.where(j < seq_blocks - 1, th[SUBLANES:SUBLANES + 1], 0.0)
    row = lax.broadcasted_iota(jnp.int32, (tm, 1), 0)
    up = jnp.where(row == 0, t_prev, pltpu.roll(t, 1, 0))
    dn = jnp.where(row == tm - 1, t_next, pltpu.roll(t, tm - 1, 0))
    cw = cw_ref[...]
    y = cw[0:1] * up + cw[1:2] * t + cw[2:3] * dn
    out = jnp.dot((gb * y).astype(BF16), wout_ref[...], preferred_element_type=F32)
    o_ref[...] = (x + mod[2:3] * out).reshape(o_ref.shape)


def _conv_layer(x, st, mods, layer, g_mix, w_in, cw, w_out):
    d = x.shape[-1]
    prev, nxt = st.halo_specs(d)
    xh = st.halo_view(x)
    return pl.pallas_call(
        functools.partial(_conv_kernel, seq_blocks=st.seq_blocks),
        grid=st.seq_grid(),
        in_specs=[st.seq_spec(d), prev, nxt, st.mod_spec(layer, d), _resident((1, d)), _resident(w_in.shape),
                  _resident(cw.shape), _resident(w_out.shape)],
        out_specs=st.seq_spec(d),
        out_shape=jax.ShapeDtypeStruct(x.shape, F32),
        compiler_params=_cparams(2), name="conv_mixer",
    )(x, xh, xh, mods, g_mix, w_in, cw, w_out)


def kernel(x_prompt, x_sample, state_ssm_re, state_ssm_im, c, c_ctx, w_mod, b_mod, g_mix, g_ffn, ffn_w1, ffn_w2, ssm_w_in, ssm_lam_re, ssm_lam_im, ssm_log_dt, ssm_b_re, ssm_b_im, ssm_c_re, ssm_c_im, ssm_d, ssm_w_out, gmlp_w_in, gmlp_w_s, gmlp_b_s, gmlp_w_out, conv_w_in, conv_w, conv_w_out, g_final):
    n_prompt, len_prompt, d = x_prompt.shape
    n_sample, len_sample, _ = x_sample.shape
    depth = w_mod.shape[0]
    n_mixers = 3
    assert 1 + n_sample <= SUBLANES
    stp = _Stream(len_prompt // SSM_CHUNK, n_prompt, per_batch_cond=False)
    sts = _Stream(len_sample // SSM_CHUNK, n_sample, per_batch_cond=True)

    xp, xs = x_prompt, x_sample
    pos = _grid_pos_embed(len_sample, d)

    cond = jnp.concatenate([c_ctx[None, :], c, jnp.zeros((SUBLANES - 1 - n_sample, d), F32)], axis=0)
    mods = _adaln(cond.T, w_mod, b_mod, 1 + n_sample).reshape(depth, SUBLANES, N_MOD, d)

    new_re, new_im = [], []
    for i in range(depth):
        kind, j = i % n_mixers, i // n_mixers
        gm = g_mix[i].reshape(1, d)
        if kind == 0:
            xp, xs, f_re, f_im = _ssm_layer(
                xp, xs, stp, sts, mods, i, gm, ssm_w_in[j], ssm_lam_re[j], ssm_lam_im[j], ssm_log_dt[j],
                ssm_b_re[j], ssm_b_im[j], ssm_c_re[j], ssm_c_im[j], ssm_d[j], ssm_w_out[j],
                state_ssm_re[:, j], state_ssm_im[:, j], pos if i == 0 else None)
            new_re.append(f_re)
            new_im.append(f_im)
        elif kind == 1:
            n_groups, chunk, _ = gmlp_w_s[j].shape
            gd = gmlp_w_out[j].shape[0] // n_groups
            bs = jnp.broadcast_to(gmlp_b_s[j][:, :, None], (n_groups, chunk, gd))
            ws = (gmlp_w_in[j].astype(BF16), gmlp_w_s[j].astype(BF16), bs, gmlp_w_out[j].astype(BF16))
            xp = _gmlp_layer(xp, stp, mods, i, gm, *ws)
            xs = _gmlp_layer(xs, sts, mods, i, gm, *ws)
        else:
            ws = (conv_w_in[j].astype(BF16), jnp.pad(conv_w[j], ((0, SUBLANES - CONV_WIDTH), (0, 0))),
                  conv_w_out[j].astype(BF16))
            xp = _conv_layer(xp, stp, mods, i, gm, *ws)
            xs = _conv_layer(xs, sts, mods, i, gm, *ws)
        gf, w1, w2 = g_ffn[i].reshape(1, d), ffn_w1[i].astype(BF16), ffn_w2[i].astype(BF16)
        g_last = g_final.reshape(1, d) if i == depth - 1 else None
        xp = _ffn(xp, stp, mods, i, gf, w1, w2, g_last)
        xs = _ffn(xs, sts, mods, i, gf, w1, w2, g_last)

    return (xp.reshape(x_prompt.shape), xs.reshape(x_sample.shape),
            jnp.stack(new_re, axis=1), jnp.stack(new_im, axis=1))
```

```python
import functools
import math

import jax
import jax.numpy as jnp
from jax import lax
from jax.experimental import pallas as pl
from jax.experimental.pallas import tpu as pltpu

EPS = 1e-6
N_MOD = 6
GRID_W = 64
SSM_GROUP = 16
SSM_CHUNK = 16
CONV_WIDTH = 3
SUBLANES = 8
LANES = 128
ROW_BLOCK = 512
SEQ_CHUNKS = 16
SEQ_PAIR = 2
GROUP_UNROLL = 2
VMEM_LIMIT = 56 * 1024 * 1024

F32 = jnp.float32
BF16 = jnp.bfloat16
NT = (((1,), (1,)), ((), ()))


def _cparams(n_axes=1, vmem=VMEM_LIMIT):
    return pltpu.CompilerParams(dimension_semantics=("arbitrary",) * n_axes, vmem_limit_bytes=vmem)


def _gelu(x):
    return 0.5 * x * (1.0 + jnp.tanh(math.sqrt(2.0 / math.pi) * (x + 0.044715 * (x * x * x))))


def _norm_mod(x, g, shift, scale):
    y = x * lax.rsqrt(jnp.mean(x * x, axis=-1, keepdims=True) + EPS)
    return (y * g) * (1.0 + scale) + shift


def _cmul(ar, ai, br, bi):
    return ar * br - ai * bi, ar * bi + ai * br


def _resident(shape):
    nd = len(shape)
    return pl.BlockSpec(shape, lambda *_: (0,) * nd, pipeline_mode=pl.Buffered(1))


class _Stream:
    def __init__(self, n_chunks, n_batch, per_batch_cond):
        self.nc, self.nb = n_chunks, n_batch
        self.n_rows = n_chunks * n_batch * SSM_CHUNK
        self.per_batch_cond = per_batch_cond
        self.row_chunks = ROW_BLOCK // SSM_CHUNK
        self.seq_blocks = n_chunks // SEQ_CHUNKS
        assert n_chunks % SEQ_CHUNKS == 0 and self.n_rows % ROW_BLOCK == 0 and n_batch % SEQ_PAIR == 0
        assert not per_batch_cond or n_chunks % self.row_chunks == 0

    def row_grid(self):
        return (self.nb, self.nc // self.row_chunks) if self.per_batch_cond else (self.n_rows // ROW_BLOCK,)

    def row_view(self, x):
        return x if self.per_batch_cond else x.reshape(self.n_rows, x.shape[-1])

    def row_spec(self, width):
        if self.per_batch_cond:
            return pl.BlockSpec((self.row_chunks, None, SSM_CHUNK, width), lambda b, j: (j, b, 0, 0))
        return pl.BlockSpec((ROW_BLOCK, width), lambda i: (i, 0))

    def row_shape(self, width):
        return (self.nc, self.nb, SSM_CHUNK, width) if self.per_batch_cond else (self.n_rows, width)

    def unview(self, x):
        return x.reshape(self.nc, self.nb, SSM_CHUNK, x.shape[-1])

    def nat_view(self, x):
        return x.reshape(self.nb, self.nc, SSM_CHUNK, x.shape[-1])

    def nat_spec(self, width):
        if self.per_batch_cond:
            return pl.BlockSpec((None, self.row_chunks, SSM_CHUNK, width), lambda b, j: (b, j, 0, 0))
        assert self.nb * SSM_CHUNK == ROW_BLOCK
        return pl.BlockSpec((self.nb, None, SSM_CHUNK, width), lambda i: (0, i, 0, 0))

    def pos_spec(self, width):
        return pl.BlockSpec((self.row_chunks, SSM_CHUNK, width), lambda b, j: (j, 0, 0))

    def seq_grid(self):
        return (self.nb // SEQ_PAIR, self.seq_blocks)

    def seq_spec(self, width):
        return pl.BlockSpec((SEQ_CHUNKS, SEQ_PAIR, SSM_CHUNK, width), lambda b, j: (j, b, 0, 0))

    def halo_specs(self, width):
        halves = SSM_CHUNK // SUBLANES
        shape = (None, SEQ_PAIR, None, SUBLANES, width)
        prev = pl.BlockSpec(shape, lambda b, j: (jnp.maximum(j * SEQ_CHUNKS - 1, 0), b, halves - 1, 0, 0))
        nxt = pl.BlockSpec(shape, lambda b, j: (jnp.minimum((j + 1) * SEQ_CHUNKS, self.nc - 1), b, 0, 0, 0))
        return prev, nxt

    def halo_view(self, x):
        return x.reshape(self.nc, self.nb, SSM_CHUNK // SUBLANES, SUBLANES, x.shape[-1])

    def mod_spec(self, layer, d):
        if self.per_batch_cond:
            return pl.BlockSpec((None, None, N_MOD, d), lambda b, j: (layer, SEQ_PAIR + b, 0, 0))
        return pl.BlockSpec((None, None, N_MOD, d), lambda *_: (layer, 0, 0, 0))

    def seq_mod_spec(self, layer, d):
        if self.per_batch_cond:
            return pl.BlockSpec((None, SEQ_PAIR, N_MOD, d), lambda b, j: (layer, 1 + b, 0, 0))
        return pl.BlockSpec((None, SEQ_PAIR, N_MOD, d), lambda *_: (layer, 0, 0, 0))

    def n_axes(self, seq=False):
        return 2 if (seq or self.per_batch_cond) else 1


def _grid_pos_embed(n_tokens, d):
    rows = n_tokens // GRID_W
    t = jnp.arange(rows * GRID_W)
    r = (t // GRID_W).astype(F32)
    col = (t % GRID_W).astype(F32)
    quarter = d // 4
    freq = 1.0 / (10000.0 ** (jnp.arange(quarter, dtype=F32) / quarter))
    ar = r[:, None] * freq
    ac = col[:, None] * freq
    return jnp.concatenate([jnp.sin(ar), jnp.cos(ar), jnp.sin(ac), jnp.cos(ac)], axis=-1)


def _adaln_kernel(ct_ref, w_ref, b_ref, o_ref, *, n_cond):
    ct = ct_ref[...]
    s = ct * jax.nn.sigmoid(ct)
    w = w_ref[...]
    rows = [jnp.sum(s[:, r:r + 1] * w, axis=0, keepdims=True) for r in range(n_cond)]
    rows += [jnp.zeros_like(rows[0])] * (SUBLANES - n_cond)
    o_ref[...] = jnp.concatenate(rows, axis=0) + b_ref[...]


def _adaln(cond_t, w_mod, b_mod, n_cond, tn=512):
    depth, d, n = w_mod.shape
    return pl.pallas_call(
        functools.partial(_adaln_kernel, n_cond=n_cond),
        grid=(depth, n // tn),
        in_specs=[
            pl.BlockSpec((d, SUBLANES), lambda l, j: (0, 0)),
            pl.BlockSpec((None, d, tn), lambda l, j: (l, 0, j)),
            pl.BlockSpec((None, 1, tn), lambda l, j: (l, 0, j)),
        ],
        out_specs=pl.BlockSpec((None, SUBLANES, tn), lambda l, j: (l, 0, j)),
        out_shape=jax.ShapeDtypeStruct((depth, SUBLANES, n), F32),
        compiler_params=_cparams(2), name="adaln",
    )(cond_t, w_mod, b_mod.reshape(depth, 1, n))


def _rows(ref):
    return ref[...].reshape(-1, ref.shape[-1])


def _rms(x, g):
    return (x * lax.rsqrt(jnp.mean(x * x, axis=-1, keepdims=True) + EPS)) * g


def _ffn_blocks(xs, mods, g, w1_ref, w2_ref, th, g_final=None):
    hn = jnp.concatenate([_norm_mod(x, g, m[3:4], m[4:5]).astype(BF16) for x, m in zip(xs, mods)], axis=0)
    acc = jnp.zeros((hn.shape[0], w2_ref.shape[1]), F32)
    for c in range(w1_ref.shape[1] // th):
        h1 = jnp.dot(hn, w1_ref[:, c * th:(c + 1) * th], preferred_element_type=F32)
        h1 = jnp.square(jnp.maximum(h1, 0.0)).astype(BF16)
        acc = acc + jnp.dot(h1, w2_ref[c * th:(c + 1) * th, :], preferred_element_type=F32)
    outs, r0 = [], 0
    for x, m in zip(xs, mods):
        out = x + m[5:6] * acc[r0:r0 + x.shape[0]]
        outs.append(out if g_final is None else _rms(out, g_final))
        r0 += x.shape[0]
    return outs


def _ffn_kernel(x_ref, mod_ref, g_ref, w1_ref, w2_ref, o_ref, *, th):
    out, = _ffn_blocks([_rows(x_ref)], [mod_ref[...]], g_ref[...], w1_ref, w2_ref, th)
    o_ref[...] = out.reshape(o_ref.shape)


def _ffn(x, st, mods, layer, g, w1, w2, th=1024):
    d = x.shape[-1]
    out = pl.pallas_call(
        functools.partial(_ffn_kernel, th=th),
        grid=st.row_grid(),
        in_specs=[st.row_spec(d), st.mod_spec(layer, d), _resident((1, d)), _resident(w1.shape), _resident(w2.shape)],
        out_specs=st.row_spec(d),
        out_shape=jax.ShapeDtypeStruct(st.row_shape(d), F32),
        compiler_params=_cparams(st.n_axes()), name="ffn",
    )(st.row_view(x), mods, g, w1, w2)
    return st.unview(out)


def _x_operands(x, st, pos):
    d = x.shape[-1]
    if x.ndim == 4:
        return [st.row_spec(d)], [st.row_view(x)]
    specs, args = [st.nat_spec(d)], [st.nat_view(x)]
    if pos is not None:
        specs.append(st.pos_spec(d))
        args.append(pos.reshape(st.nc, SSM_CHUNK, d))
    return specs, args


def _ssm_in_kernel(x_ref, *rest):
    *pos_ref, mod_ref, g_ref, w_ref, o_ref = rest
    x = _rows(x_ref) + _rows(pos_ref[0]) if pos_ref else _rows(x_ref)
    mod = mod_ref[...]
    hn = _norm_mod(x, g_ref[...], mod[0:1], mod[1:2]).astype(BF16)
    o_ref[...] = jnp.dot(hn, w_ref[...], preferred_element_type=F32).reshape(o_ref.shape)


def _ssm_in(x, st, mods, layer, g, w, pos=None):
    d, width = x.shape[-1], w.shape[1]
    x_specs, x_args = _x_operands(x, st, pos)
    out = pl.pallas_call(
        _ssm_in_kernel,
        grid=st.row_grid(),
        in_specs=x_specs + [st.mod_spec(layer, d), _resident((1, d)), _resident(w.shape)],
        out_specs=st.row_spec(width),
        out_shape=jax.ShapeDtypeStruct(st.row_shape(width), F32),
        compiler_params=_cparams(st.n_axes()), name="ssm_in",
    )(*x_args, mods, g, w)
    return out.reshape(st.n_rows, width)


def _ssm_out_ffn_kernel(x_ref, *rest, has_pos, has_final, th):
    pos_ref = rest[0] if has_pos else None
    y_ref, mod_ref, wout_ref, gf_ref, w1_ref, w2_ref = rest[has_pos:has_pos + 6]
    g_final = rest[-2][...] if has_final else None
    o_ref = rest[-1]
    x = _rows(x_ref) + _rows(pos_ref) if has_pos else _rows(x_ref)
    y = _rows(y_ref)
    mod = mod_ref[...]
    d = x.shape[-1]
    half = x.shape[0] // 2
    mids = []
    for k in range(2):
        rows = slice(k * half, (k + 1) * half)
        ag = jnp.dot(_gelu(y[rows]).astype(BF16), wout_ref[...], preferred_element_type=F32)
        mids.append(x[rows] + mod[2:3] * (ag[:, :d] * jax.nn.sigmoid(ag[:, d:])))
    outs = _ffn_blocks(mids, [mod, mod], gf_ref[...], w1_ref, w2_ref, th, g_final)
    o_ref[...] = jnp.concatenate(outs, axis=0).reshape(o_ref.shape)


def _ssm_out_ffn(x, y, st, mods, layer, w_out, gf, w1, w2, pos=None, g_final=None, th=1024):
    d = x.shape[-1]
    x_specs, x_args = _x_operands(x, st, pos)
    last = g_final is not None
    out = pl.pallas_call(
        functools.partial(_ssm_out_ffn_kernel, has_pos=len(x_args) == 2, has_final=last, th=th),
        grid=st.row_grid(),
        in_specs=x_specs + [st.row_spec(d), st.mod_spec(layer, d), _resident(w_out.shape), _resident((1, d)),
                            _resident(w1.shape), _resident(w2.shape)] + ([_resident((1, d))] if last else []),
        out_specs=st.nat_spec(d) if last else st.row_spec(d),
        out_shape=jax.ShapeDtypeStruct((st.nb, st.nc, SSM_CHUNK, d) if last else st.row_shape(d), F32),
        compiler_params=_cparams(st.n_axes()), name="ssm_out_ffn",
    )(*x_args, st.row_view(st.unview(y)), mods, w_out, gf, w1, w2, *([g_final] if last else []))
    return out if last else st.unview(out)


def _ssm_operators(prow, bt, ct):
    nk = SSM_CHUNK
    half = prow.shape[1] // 2
    lr, li = prow[0:1], prow[1:2]
    dt = jnp.exp(prow[2:3])
    mag = jnp.exp(lr * dt)
    ar, ai = mag * jnp.cos(li * dt), mag * jnp.sin(li * dt)
    den = lr * lr + li * li
    nr, ni = _cmul(ar - 1.0, ai, lr, -li)
    bbr, bbi = _cmul(nr / den, ni / den, bt[0:nk], bt[nk:2 * nk])
    ctr, cti = ct[0:nk], ct[nk:2 * nk]

    pr, pi = [jnp.ones_like(ar)], [jnp.zeros_like(ar)]
    for _ in range(nk):
        r, i = _cmul(pr[-1], pi[-1], ar, ai)
        pr.append(r)
        pi.append(i)
    is_fwd = lax.broadcasted_iota(jnp.int32, (1, 2 * half), 1) < half

    def pw(kf, kb):
        return jnp.where(is_fwd, pr[kf], pr[kb]), jnp.where(is_fwd, pi[kf], pi[kb])

    wp_r, wp_i, ca_r, ca_i, wc_r, wc_i = [], [], [], [], [], []
    for t in range(nk):
        r, i = _cmul(bbr, bbi, *pw(nk - 1 - t, t))
        wp_r.append(r)
        wp_i.append(i)
        r, i = _cmul(ctr, cti, *pw(t, nk - 1 - t))
        ca_r.append(r)
        ca_i.append(i)
        r, i = _cmul(ctr, cti, *pw(t + 1, nk - t))
        wc_r.append(r)
        wc_i.append(-i)
    cat = jnp.concatenate
    wp = cat([cat(wp_r, 0), cat(wp_i, 0)], axis=1).astype(BF16)
    ca = cat([cat(ca_r, 0), cat(ca_i, 0)], axis=1)
    wct = cat([cat(wc_r, 0), cat(wc_i, 0)], axis=1).astype(BF16)

    zero = jnp.zeros_like(bbr)
    lhs = cat([cat([jnp.where(is_fwd, bbr, zero), jnp.where(is_fwd, -bbi, zero)], 1),
               cat([jnp.where(is_fwd, zero, bbr), jnp.where(is_fwd, zero, -bbi)], 1)], 0)
    kk = lax.dot_general(lhs, ca, NT, precision=lax.Precision.HIGHEST, preferred_element_type=F32)
    width = nk * SSM_GROUP
    lane = lax.broadcasted_iota(jnp.int32, (nk, width), 1)
    row = lax.broadcasted_iota(jnp.int32, (nk, width), 0)
    d_lanes = cat([prow[3:4], jnp.zeros_like(prow[3:4])], axis=1)
    kf = kk[0:nk] + jnp.where(lane == row, d_lanes, 0.0)
    kb = kk[nk:2 * nk]
    blocks = []
    for s in range(nk):
        f = kf if s == 0 else pltpu.roll(kf, SSM_GROUP * s, 1)
        sh = (SSM_GROUP * (s + 1)) % width
        b = kb if sh == 0 else pltpu.roll(kb, sh, 1)
        blocks.append(jnp.where(lane >= SSM_GROUP * s, f, 0.0) + jnp.where(lane < SSM_GROUP * (s + 1), b, 0.0))
    tg = cat(blocks, 0).astype(BF16)
    return tg, wp, wct, pr[nk], pi[nk]


def _ssm_core_kernel(prow_ref, bt_ref, ct_ref, h0r_ref, h0i_ref, up_ref, us_ref,
                     yp_ref, ys_ref, fr_ref, fi_ref, atp_scr, ats_scr, p_scr, q_scr, sa_scr, sb_scr,
                     *, p_batch, s_batch):
    nk = SSM_CHUNK
    n_groups = LANES // SSM_GROUP
    cb_p, cb_s = atp_scr.shape[2], ats_scr.shape[2]
    w2 = prow_ref.shape[2]
    for t in range(nk):
        atp_scr[t] = up_ref[pl.ds(t, cb_p, stride=nk), :].T
        ats_scr[t] = us_ref[pl.ds(t, cb_s, stride=nk), :].T

    lane1 = lax.broadcasted_iota(jnp.int32, (1, w2), 1)
    is_fwd = lane1 < (w2 // 2)
    fwd_s = (lax.broadcasted_iota(jnp.int32, (1, 2 * w2), 1) % w2) < (w2 // 2)

    def scan(n_tiles, rows, src_scr, cur_r, cur_i, m_r, m_i, unroll):
        def step(j, carry):
            c_r, c_i = carry
            lo, hi = j * rows, (n_tiles - 1 - j) * rows
            if not isinstance(j, int):
                lo, hi = pl.multiple_of(lo, rows), pl.multiple_of(hi, rows)
            sa_scr[pl.ds(lo, rows), 0:w2] = c_r
            sa_scr[pl.ds(lo, rows), w2:2 * w2] = c_i
            sb_scr[pl.ds(hi, rows), 0:w2] = c_r
            sb_scr[pl.ds(hi, rows), w2:2 * w2] = c_i
            s_r = jnp.where(is_fwd, src_scr[pl.ds(lo, rows), 0:w2], src_scr[pl.ds(hi, rows), 0:w2])
            s_i = jnp.where(is_fwd, src_scr[pl.ds(lo, rows), w2:2 * w2], src_scr[pl.ds(hi, rows), w2:2 * w2])
            n_r, n_i = _cmul(m_r, m_i, c_r, c_i)
            return n_r + s_r, n_i + s_i

        if unroll is True:
            carry = (cur_r, cur_i)
            for j in range(n_tiles):
                carry = step(j, carry)
            return carry
        return lax.fori_loop(0, n_tiles, step, (cur_r, cur_i), unroll=unroll)

    def group(g, _):
        r0 = pl.multiple_of(g * SSM_GROUP, SSM_GROUP)
        tg, wp, wct, a_r, a_i = _ssm_operators(prow_ref[g], bt_ref[g], ct_ref[g])

        def mix(at_scr, cb, carry_fn):
            a = at_scr[:, pl.ds(r0, SSM_GROUP), :].reshape(nk * SSM_GROUP, cb)
            u = a.T.astype(BF16)
            p_scr[0:cb, :] = jnp.dot(u, wp, preferred_element_type=F32)
            fin = carry_fn(cb)
            s = jnp.where(fwd_s, sa_scr[0:cb, :], sb_scr[0:cb, :]).astype(BF16)
            y = jnp.dot(u, tg, preferred_element_type=F32) + lax.dot_general(s, wct, NT, preferred_element_type=F32)
            at_scr[:, pl.ds(r0, SSM_GROUP), :] = y.T.reshape(nk, SSM_GROUP, cb)
            return fin

        def prompt_carry(cb):
            z = jnp.zeros((p_batch, w2), F32)
            return scan(cb // p_batch, p_batch, p_scr, z, z, a_r, a_i, True)

        def sample_carry(cb):
            assert 2 * s_batch == SUBLANES
            p_r, p_i = p_scr[0:cb, 0:w2], p_scr[0:cb, w2:2 * w2]
            ap_r, ap_i = _cmul(a_r, a_i, p_r, p_i)
            q_scr[0:cb, 0:w2] = jnp.where(is_fwd, pltpu.roll(p_r, cb - s_batch, 0), pltpu.roll(p_r, s_batch, 0)) + ap_r
            q_scr[0:cb, w2:2 * w2] = jnp.where(is_fwd, pltpu.roll(p_i, cb - s_batch, 0), pltpu.roll(p_i, s_batch, 0)) + ap_i
            h_r, h_i = h0r_ref[g], h0i_ref[g]
            row = lax.broadcasted_iota(jnp.int32, (SUBLANES, w2), 0)
            keep = jnp.where(is_fwd, 1, 0) == jnp.where(row < s_batch, 1, 0)
            e_r = pltpu.roll(jnp.where(is_fwd, p_r[0:SUBLANES], p_r[cb - SUBLANES:cb]), s_batch, 0)
            e_i = pltpu.roll(jnp.where(is_fwd, p_i[0:SUBLANES], p_i[cb - SUBLANES:cb]), s_batch, 0)
            ah_r, ah_i = _cmul(a_r, a_i, h_r, h_i)
            c_r = jnp.where(keep, h_r, ah_r + e_r)
            c_i = jnp.where(keep, h_i, ah_i + e_i)
            a2_r, a2_i = _cmul(a_r, a_i, a_r, a_i)
            return scan(cb // SUBLANES, SUBLANES, q_scr, c_r, c_i, a2_r, a2_i, True)

        f_r, f_i = mix(atp_scr, cb_p, prompt_carry)
        fr_ref[g] = f_r
        fi_ref[g] = f_i
        mix(ats_scr, cb_s, sample_carry)
        return 0

    lax.fori_loop(0, n_groups, group, 0, unroll=GROUP_UNROLL)
    for t in range(nk):
        yp_ref[pl.ds(t, cb_p, stride=nk), :] = atp_scr[t].T
        ys_ref[pl.ds(t, cb_s, stride=nk), :] = ats_scr[t].T


def _ssm_core(prow, bt, ct, h0r, h0i, up, us, p_batch, s_batch):
    groups, _, w2 = prow.shape
    gpb = LANES // SSM_GROUP
    np_rows, ns_rows = up.shape[0], us.shape[0]
    cb_p, cb_s = np_rows // SSM_CHUNK, ns_rows // SSM_CHUNK
    cb_max = max(cb_p, cb_s)

    def gspec(r, c):
        return pl.BlockSpec((gpb, r, c), lambda o: (o, 0, 0))

    def lane_spec(n):
        return pl.BlockSpec((n, LANES), lambda o: (0, o))

    return pl.pallas_call(
        functools.partial(_ssm_core_kernel, p_batch=p_batch, s_batch=s_batch),
        grid=(groups // gpb,),
        in_specs=[gspec(SUBLANES, w2), gspec(2 * SSM_GROUP, w2), gspec(2 * SSM_GROUP, w2),
                  gspec(SUBLANES, w2), gspec(SUBLANES, w2), lane_spec(np_rows), lane_spec(ns_rows)],
        out_specs=[lane_spec(np_rows), lane_spec(ns_rows), gspec(p_batch, w2), gspec(p_batch, w2)],
        out_shape=[jax.ShapeDtypeStruct(up.shape, F32), jax.ShapeDtypeStruct(us.shape, F32),
                   jax.ShapeDtypeStruct((groups, p_batch, w2), F32),
                   jax.ShapeDtypeStruct((groups, p_batch, w2), F32)],
        scratch_shapes=[pltpu.VMEM((SSM_CHUNK, LANES, cb_p), F32), pltpu.VMEM((SSM_CHUNK, LANES, cb_s), F32)]
        + [pltpu.VMEM((cb_max, 2 * w2), F32)] * 4,
        compiler_params=_cparams(), name="ssm_core",
    )(prow, bt, ct, h0r, h0i, up, us)


def _ssm_layer(xp, xs, stp, sts, mods, layer, g_mix, w_in, lam_re, lam_im, log_dt, b_re, b_im, c_re, c_im, d_skip,
               w_out, h0_re, h0_im, ffn, pos=None, g_final=None):
    width = w_in.shape[1]
    groups = width // SSM_GROUP
    n_state = lam_re.shape[-1]
    w_in = w_in.astype(BF16)
    up = _ssm_in(xp, stp, mods, layer, g_mix, w_in)
    us = _ssm_in(xs, sts, mods, layer, g_mix, w_in, pos)

    def lanes_dir_state(v):
        return v.transpose(1, 0, 2).reshape(groups, 2 * n_state)

    d_rows = jnp.pad(d_skip.reshape(groups, SSM_GROUP), ((0, 0), (0, 2 * n_state - SSM_GROUP)))
    prow = jnp.stack([lanes_dir_state(lam_re), lanes_dir_state(lam_im),
                      lanes_dir_state(jnp.broadcast_to(log_dt[..., None], lam_re.shape)), d_rows], axis=1)
    prow = jnp.pad(prow, ((0, 0), (0, SUBLANES - 4), (0, 0)))
    bt = jnp.concatenate([b_re.transpose(1, 3, 0, 2).reshape(groups, SSM_GROUP, 2 * n_state),
                          b_im.transpose(1, 3, 0, 2).reshape(groups, SSM_GROUP, 2 * n_state)], axis=1)
    ct = jnp.concatenate([c_re.transpose(1, 2, 0, 3).reshape(groups, SSM_GROUP, 2 * n_state),
                          c_im.transpose(1, 2, 0, 3).reshape(groups, SSM_GROUP, 2 * n_state)], axis=1)

    def h0_rows(h):
        h = h.transpose(2, 0, 1, 3).reshape(groups, sts.nb, 2 * n_state)
        return jnp.concatenate([h] * (SUBLANES // sts.nb), axis=1)

    yp, ys, f_re, f_im = _ssm_core(prow, bt, ct, h0_rows(h0_re), h0_rows(h0_im), up, us, stp.nb, sts.nb)
    w_out = w_out.astype(BF16)
    xp = _ssm_out_ffn(xp, yp, stp, mods, layer, w_out, *ffn, g_final=g_final)
    xs = _ssm_out_ffn(xs, ys, sts, mods, layer, w_out, *ffn, pos=pos, g_final=g_final)

    def final(f):
        return f.reshape(groups, stp.nb, 2, n_state).transpose(1, 2, 0, 3)

    return xp, xs, final(f_re), final(f_im)


def _seq_rows(ref, k):
    return ref[:, k].reshape(-1, ref.shape[-1])


def _gmlp_kernel(x_ref, mod_ref, g_ref, win_ref, ws_ref, bs_ref, wout_ref, o_ref, t_scr):
    n_groups, chunk, gd = bs_ref.shape
    for k in range(SEQ_PAIR):
        x = _seq_rows(x_ref, k)
        mod = mod_ref[k]
        hn = _norm_mod(x, g_ref[...], mod[0:1], mod[1:2]).astype(BF16)
        z = _gelu(jnp.dot(hn, win_ref[...], preferred_element_type=F32))
        wdt = z.shape[1] // 2
        u, v = z[:, :wdt], z[:, wdt:]
        vc = v - jnp.mean(v, axis=-1, keepdims=True)
        vn = (vc * lax.rsqrt(jnp.mean(vc * vc, axis=-1, keepdims=True) + EPS)).astype(BF16)
        for c in range(x.shape[0] // chunk):
            rows = slice(c * chunk, (c + 1) * chunk)
            for g in range(n_groups):
                cols = slice(g * gd, (g + 1) * gd)
                s = jnp.dot(ws_ref[g], vn[rows, cols], preferred_element_type=F32) + bs_ref[g]
                t_scr[k, rows, cols] = (u[rows, cols] * s).astype(BF16)
        out = jnp.dot(t_scr[k], wout_ref[...], preferred_element_type=F32)
        o_ref[:, k] = (x + mod[2:3] * out).reshape(o_ref.shape[0], o_ref.shape[2], o_ref.shape[3])


def _gmlp_layer(x, st, mods, layer, g_mix, w_in, w_s, bs, w_out):
    d = x.shape[-1]
    return pl.pallas_call(
        _gmlp_kernel,
        grid=st.seq_grid(),
        in_specs=[st.seq_spec(d), st.seq_mod_spec(layer, d), _resident((1, d)), _resident(w_in.shape),
                  _resident(w_s.shape), _resident(bs.shape), _resident(w_out.shape)],
        out_specs=st.seq_spec(d),
        out_shape=jax.ShapeDtypeStruct(x.shape, F32),
        scratch_shapes=[pltpu.VMEM((SEQ_PAIR, SEQ_CHUNKS * SSM_CHUNK, w_out.shape[0]), BF16)],
        compiler_params=_cparams(2), name="gmlp",
    )(x, mods, g_mix, w_in, w_s, bs, w_out)


def _conv_ffn_kernel(x_ref, xprev_ref, xnext_ref, mod_ref, g_ref, win_ref, cw_ref, wout_ref, gf_ref, w1_ref, w2_ref,
                     o_ref, *, seq_blocks, th):
    j = pl.program_id(1)
    g = g_ref[...]
    cw = cw_ref[...]
    d = x_ref.shape[-1]
    xs = [_seq_rows(x_ref, k) for k in range(SEQ_PAIR)]
    mods = [mod_ref[k] for k in range(SEQ_PAIR)]
    hh = jnp.concatenate([_norm_mod(jnp.concatenate([xprev_ref[k], xnext_ref[k]], axis=0), g, m[0:1], m[1:2])
                          for k, m in enumerate(mods)], axis=0).astype(BF16)
    zh = jnp.dot(hh, win_ref[:, d:], preferred_element_type=F32)
    th_all = zh[:, :d] * zh[:, d:]
    mids = []
    for k, (x, mod) in enumerate(zip(xs, mods)):
        tm = x.shape[0]
        hn = _norm_mod(x, g, mod[0:1], mod[1:2]).astype(BF16)
        z = jnp.dot(hn, win_ref[...], preferred_element_type=F32)
        gb, t = z[:, :d], z[:, d:2 * d] * z[:, 2 * d:]
        r0 = 2 * SUBLANES * k
        t_prev = jnp.where(j > 0, th_all[r0 + SUBLANES - 1:r0 + SUBLANES], 0.0)
        t_next = jnp.where(j < seq_blocks - 1, th_all[r0 + SUBLANES:r0 + SUBLANES + 1], 0.0)
        row = lax.broadcasted_iota(jnp.int32, (tm, 1), 0)
        up = jnp.where(row == 0, t_prev, pltpu.roll(t, 1, 0))
        dn = jnp.where(row == tm - 1, t_next, pltpu.roll(t, tm - 1, 0))
        y = cw[0:1] * up + cw[1:2] * t + cw[2:3] * dn
        out = jnp.dot((gb * y).astype(BF16), wout_ref[...], preferred_element_type=F32)
        mids.append(x + mod[2:3] * out)
    outs = _ffn_blocks(mids, mods, gf_ref[...], w1_ref, w2_ref, th)
    for k, out in enumerate(outs):
        o_ref[:, k] = out.reshape(o_ref.shape[0], o_ref.shape[2], o_ref.shape[3])


def _conv_ffn_layer(x, st, mods, layer, g_mix, w_in, cw, w_out, gf, w1, w2, th=1024):
    d = x.shape[-1]
    prev, nxt = st.halo_specs(d)
    xh = st.halo_view(x)
    return pl.pallas_call(
        functools.partial(_conv_ffn_kernel, seq_blocks=st.seq_blocks, th=th),
        grid=st.seq_grid(),
        in_specs=[st.seq_spec(d), prev, nxt, st.seq_mod_spec(layer, d), _resident((1, d)), _resident(w_in.shape),
                  _resident(cw.shape), _resident(w_out.shape), _resident((1, d)), _resident(w1.shape),
                  _resident(w2.shape)],
        out_specs=st.seq_spec(d),
        out_shape=jax.ShapeDtypeStruct(x.shape, F32),
        compiler_params=_cparams(2), name="conv_ffn",
    )(x, xh, xh, mods, g_mix, w_in, cw, w_out, gf, w1, w2)


def kernel(x_prompt, x_sample, state_ssm_re, state_ssm_im, c, c_ctx, w_mod, b_mod, g_mix, g_ffn, ffn_w1, ffn_w2, ssm_w_in, ssm_lam_re, ssm_lam_im, ssm_log_dt, ssm_b_re, ssm_b_im, ssm_c_re, ssm_c_im, ssm_d, ssm_w_out, gmlp_w_in, gmlp_w_s, gmlp_b_s, gmlp_w_out, conv_w_in, conv_w, conv_w_out, g_final):
    n_prompt, len_prompt, d = x_prompt.shape
    n_sample, len_sample, _ = x_sample.shape
    depth = w_mod.shape[0]
    n_mixers = 3
    assert 1 + n_sample <= SUBLANES
    stp = _Stream(len_prompt // SSM_CHUNK, n_prompt, per_batch_cond=False)
    sts = _Stream(len_sample // SSM_CHUNK, n_sample, per_batch_cond=True)

    xp, xs = x_prompt, x_sample
    pos = _grid_pos_embed(len_sample, d)

    cond = jnp.concatenate([c_ctx[None, :], c, jnp.zeros((SUBLANES - 1 - n_sample, d), F32)], axis=0)
    mods = _adaln(cond.T, w_mod, b_mod, 1 + n_sample)
    table = [0] * SEQ_PAIR + list(range(1, 1 + n_sample))
    mods = mods[:, jnp.array(table)].reshape(depth, len(table), N_MOD, d)

    assert (depth - 1) % n_mixers == 0, "the last layer's kernel writes the natural-order outputs"
    new_re, new_im = [], []
    for i in range(depth):
        kind, j = i % n_mixers, i // n_mixers
        gm = g_mix[i].reshape(1, d)
        ffn = (g_ffn[i].reshape(1, d), ffn_w1[i].astype(BF16), ffn_w2[i].astype(BF16))
        if kind == 0:
            xp, xs, f_re, f_im = _ssm_layer(
                xp, xs, stp, sts, mods, i, gm, ssm_w_in[j], ssm_lam_re[j], ssm_lam_im[j], ssm_log_dt[j],
                ssm_b_re[j], ssm_b_im[j], ssm_c_re[j], ssm_c_im[j], ssm_d[j], ssm_w_out[j],
                state_ssm_re[:, j], state_ssm_im[:, j], ffn, pos if i == 0 else None,
                g_final.reshape(1, d) if i == depth - 1 else None)
            new_re.append(f_re)
            new_im.append(f_im)
        elif kind == 1:
            n_groups, chunk, _ = gmlp_w_s[j].shape
            gd = gmlp_w_out[j].shape[0] // n_groups
            bs = jnp.broadcast_to(gmlp_b_s[j][:, :, None], (n_groups, chunk, gd))
            ws = (gmlp_w_in[j].astype(BF16), gmlp_w_s[j].astype(BF16), bs, gmlp_w_out[j].astype(BF16))
            xp = _ffn(_gmlp_layer(xp, stp, mods, i, gm, *ws), stp, mods, i, *ffn)
            xs = _ffn(_gmlp_layer(xs, sts, mods, i, gm, *ws), sts, mods, i, *ffn)
        else:
            ws = (conv_w_in[j].astype(BF16), jnp.pad(conv_w[j], ((0, SUBLANES - CONV_WIDTH), (0, 0))),
                  conv_w_out[j].astype(BF16))
            xp = _conv_ffn_layer(xp, stp, mods, i, gm, *ws, *ffn)
            xs = _conv_ffn_layer(xs, sts, mods, i, gm, *ws, *ffn)

    return (xp.reshape(x_prompt.shape), xs.reshape(x_sample.shape),
            jnp.stack(new_re, axis=1), jnp.stack(new_im, axis=1))
```

```python
import functools
import math

import jax
import jax.numpy as jnp
from jax import lax
from jax.experimental import pallas as pl
from jax.experimental.pallas import tpu as pltpu

EPS = 1e-6
N_MOD = 6
GRID_W = 64
SSM_GROUP = 16
SSM_CHUNK = 16
CONV_WIDTH = 3
SUBLANES = 8
LANES = 128
ROW_BLOCK = 512
SEQ_CHUNKS = 16
SEQ_PAIR = 2
GROUP_UNROLL = 2
VMEM_LIMIT = 56 * 1024 * 1024

F32 = jnp.float32
BF16 = jnp.bfloat16
NT = (((1,), (1,)), ((), ()))


def _cparams(n_axes=1, vmem=VMEM_LIMIT):
    return pltpu.CompilerParams(dimension_semantics=("arbitrary",) * n_axes, vmem_limit_bytes=vmem)


def _gelu(x):
    return 0.5 * x * (1.0 + jnp.tanh(math.sqrt(2.0 / math.pi) * (x + 0.044715 * (x * x * x))))


def _norm_mod(x, g, shift, scale):
    y = x * lax.rsqrt(jnp.mean(x * x, axis=-1, keepdims=True) + EPS)
    return (y * g) * (1.0 + scale) + shift


def _cmul(ar, ai, br, bi):
    return ar * br - ai * bi, ar * bi + ai * br


def _resident(shape):
    nd = len(shape)
    return pl.BlockSpec(shape, lambda *_: (0,) * nd, pipeline_mode=pl.Buffered(1))


def _cast_plan(srcs, n_steps, step_of):
    ins, outs, shapes = [], [], []
    for stack, layer in srcs:
        _, r, c = stack.shape
        blk = r // n_steps
        assert r % n_steps == 0 and blk % (2 * SUBLANES) == 0
        ins.append(pl.BlockSpec((None, blk, c), lambda *ids, layer=layer: (layer, step_of(*ids), 0)))
        outs.append(pl.BlockSpec((blk, c), lambda *ids: (step_of(*ids), 0)))
        shapes.append(jax.ShapeDtypeStruct((r, c), BF16))
    return ins, outs, shapes


def _cast_refs(src_refs, dst_refs):
    for src, dst in zip(src_refs, dst_refs):
        dst[...] = src[...].astype(BF16)


class _Stream:
    def __init__(self, n_chunks, n_batch, per_batch_cond):
        self.nc, self.nb = n_chunks, n_batch
        self.n_rows = n_chunks * n_batch * SSM_CHUNK
        self.per_batch_cond = per_batch_cond
        self.row_chunks = ROW_BLOCK // SSM_CHUNK
        self.seq_blocks = n_chunks // SEQ_CHUNKS
        assert n_chunks % SEQ_CHUNKS == 0 and self.n_rows % ROW_BLOCK == 0 and n_batch % SEQ_PAIR == 0
        assert not per_batch_cond or n_chunks % self.row_chunks == 0

    def row_grid(self):
        return (self.nb, self.nc // self.row_chunks) if self.per_batch_cond else (self.n_rows // ROW_BLOCK,)

    def row_view(self, x):
        return x if self.per_batch_cond else x.reshape(self.n_rows, x.shape[-1])

    def row_spec(self, width):
        if self.per_batch_cond:
            return pl.BlockSpec((self.row_chunks, None, SSM_CHUNK, width), lambda b, j: (j, b, 0, 0))
        return pl.BlockSpec((ROW_BLOCK, width), lambda i: (i, 0))

    def row_shape(self, width):
        return (self.nc, self.nb, SSM_CHUNK, width) if self.per_batch_cond else (self.n_rows, width)

    def unview(self, x):
        return x.reshape(self.nc, self.nb, SSM_CHUNK, x.shape[-1])

    def nat_view(self, x):
        return x.reshape(self.nb, self.nc, SSM_CHUNK, x.shape[-1])

    def nat_spec(self, width):
        if self.per_batch_cond:
            return pl.BlockSpec((None, self.row_chunks, SSM_CHUNK, width), lambda b, j: (b, j, 0, 0))
        assert self.nb * SSM_CHUNK == ROW_BLOCK
        return pl.BlockSpec((self.nb, None, SSM_CHUNK, width), lambda i: (0, i, 0, 0))

    def pos_spec(self, width):
        return pl.BlockSpec((self.row_chunks, SSM_CHUNK, width), lambda b, j: (j, 0, 0))

    def seq_grid(self):
        return (self.nb // SEQ_PAIR, self.seq_blocks)

    def seq_spec(self, width):
        return pl.BlockSpec((SEQ_CHUNKS, SEQ_PAIR, SSM_CHUNK, width), lambda b, j: (j, b, 0, 0))

    def halo_specs(self, width):
        halves = SSM_CHUNK // SUBLANES
        shape = (None, SEQ_PAIR, None, SUBLANES, width)
        prev = pl.BlockSpec(shape, lambda b, j: (jnp.maximum(j * SEQ_CHUNKS - 1, 0), b, halves - 1, 0, 0))
        nxt = pl.BlockSpec(shape, lambda b, j: (jnp.minimum((j + 1) * SEQ_CHUNKS, self.nc - 1), b, 0, 0, 0))
        return prev, nxt

    def halo_view(self, x):
        return x.reshape(self.nc, self.nb, SSM_CHUNK // SUBLANES, SUBLANES, x.shape[-1])

    def mod_spec(self, layer, d):
        if self.per_batch_cond:
            return pl.BlockSpec((None, None, N_MOD, d), lambda b, j: (layer, SEQ_PAIR + b, 0, 0))
        return pl.BlockSpec((None, None, N_MOD, d), lambda *_: (layer, 0, 0, 0))

    def seq_mod_spec(self, layer, d):
        if self.per_batch_cond:
            return pl.BlockSpec((None, SEQ_PAIR, N_MOD, d), lambda b, j: (layer, 1 + b, 0, 0))
        return pl.BlockSpec((None, SEQ_PAIR, N_MOD, d), lambda *_: (layer, 0, 0, 0))

    def n_axes(self, seq=False):
        return 2 if (seq or self.per_batch_cond) else 1

    def cast_plan(self, srcs, seq=False):
        if seq:
            return _cast_plan(srcs, (self.nb // SEQ_PAIR) * self.seq_blocks, lambda b, j: b * self.seq_blocks + j)
        assert not srcs or not self.per_batch_cond
        return _cast_plan(srcs, self.n_rows // ROW_BLOCK, lambda i: i)


def _grid_pos_embed(n_tokens, d):
    rows = n_tokens // GRID_W
    t = jnp.arange(rows * GRID_W)
    r = (t // GRID_W).astype(F32)
    col = (t % GRID_W).astype(F32)
    quarter = d // 4
    freq = 1.0 / (10000.0 ** (jnp.arange(quarter, dtype=F32) / quarter))
    ar = r[:, None] * freq
    ac = col[:, None] * freq
    return jnp.concatenate([jnp.sin(ar), jnp.cos(ar), jnp.sin(ac), jnp.cos(ac)], axis=-1)


def _adaln_kernel(ct_ref, w_ref, b_ref, o_ref, *, n_cond):
    ct = ct_ref[...]
    s = ct * jax.nn.sigmoid(ct)
    w = w_ref[...]
    rows = [jnp.sum(s[:, r:r + 1] * w, axis=0, keepdims=True) for r in range(n_cond)]
    rows += [jnp.zeros_like(rows[0])] * (SUBLANES - n_cond)
    o_ref[...] = jnp.concatenate(rows, axis=0) + b_ref[...]


def _adaln(cond_t, w_mod, b_mod, n_cond, tn=1024):
    depth, d, n = w_mod.shape
    return pl.pallas_call(
        functools.partial(_adaln_kernel, n_cond=n_cond),
        grid=(depth, n // tn),
        in_specs=[
            pl.BlockSpec((d, SUBLANES), lambda l, j: (0, 0)),
            pl.BlockSpec((None, d, tn), lambda l, j: (l, 0, j)),
            pl.BlockSpec((None, 1, tn), lambda l, j: (l, 0, j)),
        ],
        out_specs=pl.BlockSpec((None, SUBLANES, tn), lambda l, j: (l, 0, j)),
        out_shape=jax.ShapeDtypeStruct((depth, SUBLANES, n), F32),
        compiler_params=_cparams(2), name="adaln",
    )(cond_t, w_mod, b_mod.reshape(depth, 1, n))


def _rows(ref):
    return ref[...].reshape(-1, ref.shape[-1])


def _rms(x, g):
    return (x * lax.rsqrt(jnp.mean(x * x, axis=-1, keepdims=True) + EPS)) * g


def _ffn_blocks(xs, mods, g, w1_ref, w2_ref, th, g_final=None):
    hn = jnp.concatenate([_norm_mod(x, g, m[3:4], m[4:5]).astype(BF16) for x, m in zip(xs, mods)], axis=0)
    acc = jnp.zeros((hn.shape[0], w2_ref.shape[1]), F32)
    for c in range(w1_ref.shape[1] // th):
        h1 = jnp.dot(hn, w1_ref[:, c * th:(c + 1) * th], preferred_element_type=F32)
        h1 = jnp.square(jnp.maximum(h1, 0.0)).astype(BF16)
        acc = acc + jnp.dot(h1, w2_ref[c * th:(c + 1) * th, :], preferred_element_type=F32)
    outs, r0 = [], 0
    for x, m in zip(xs, mods):
        out = x + m[5:6] * acc[r0:r0 + x.shape[0]]
        outs.append(out if g_final is None else _rms(out, g_final))
        r0 += x.shape[0]
    return outs


def _ffn_kernel(x_ref, mod_ref, g_ref, w1_ref, w2_ref, *rest, th, n_cast):
    o_ref = rest[n_cast]
    out, = _ffn_blocks([_rows(x_ref)], [mod_ref[...]], g_ref[...], w1_ref, w2_ref, th)
    o_ref[...] = out.reshape(o_ref.shape)
    _cast_refs(rest[:n_cast], rest[n_cast + 1:])


def _ffn(x, st, mods, layer, g, w1, w2, casts=(), th=1024):
    d = x.shape[-1]
    c_in, c_out, c_shapes = st.cast_plan(casts)
    out, *cast = pl.pallas_call(
        functools.partial(_ffn_kernel, th=th, n_cast=len(casts)),
        grid=st.row_grid(),
        in_specs=[st.row_spec(d), st.mod_spec(layer, d), _resident((1, d)), _resident(w1.shape), _resident(w2.shape)]
        + c_in,
        out_specs=[st.row_spec(d)] + c_out,
        out_shape=[jax.ShapeDtypeStruct(st.row_shape(d), F32)] + c_shapes,
        compiler_params=_cparams(st.n_axes()), name="ffn",
    )(st.row_view(x), mods, g, w1, w2, *[stack for stack, _ in casts])
    return st.unview(out), cast


def _x_operands(x, st, pos):
    d = x.shape[-1]
    if x.ndim == 4:
        return [st.row_spec(d)], [st.row_view(x)]
    specs, args = [st.nat_spec(d)], [st.nat_view(x)]
    if pos is not None:
        specs.append(st.pos_spec(d))
        args.append(pos.reshape(st.nc, SSM_CHUNK, d))
    return specs, args


def _ssm_in_kernel(x_ref, *rest):
    *pos_ref, mod_ref, g_ref, w_ref, o_ref = rest
    x = _rows(x_ref) + _rows(pos_ref[0]) if pos_ref else _rows(x_ref)
    mod = mod_ref[...]
    hn = _norm_mod(x, g_ref[...], mod[0:1], mod[1:2]).astype(BF16)
    o_ref[...] = jnp.dot(hn, w_ref[...], preferred_element_type=F32).reshape(o_ref.shape)


def _ssm_in(x, st, mods, layer, g, w, pos=None):
    d, width = x.shape[-1], w.shape[1]
    x_specs, x_args = _x_operands(x, st, pos)
    out = pl.pallas_call(
        _ssm_in_kernel,
        grid=st.row_grid(),
        in_specs=x_specs + [st.mod_spec(layer, d), _resident((1, d)), _resident(w.shape)],
        out_specs=st.row_spec(width),
        out_shape=jax.ShapeDtypeStruct(st.row_shape(width), F32),
        compiler_params=_cparams(st.n_axes()), name="ssm_in",
    )(*x_args, mods, g, w)
    return out.reshape(st.n_rows, width)


def _ssm_out_ffn_kernel(x_ref, *rest, has_pos, has_final, th, n_cast):
    pos_ref = rest[0] if has_pos else None
    y_ref, mod_ref, wout_ref, gf_ref, w1_ref, w2_ref = rest[has_pos:has_pos + 6]
    n_in = has_pos + 6 + has_final
    g_final = rest[n_in - 1][...] if has_final else None
    o_ref = rest[n_in + n_cast]
    x = _rows(x_ref) + _rows(pos_ref) if has_pos else _rows(x_ref)
    y = _rows(y_ref)
    mod = mod_ref[...]
    d = x.shape[-1]
    half = x.shape[0] // 2
    mids = []
    for k in range(2):
        rows = slice(k * half, (k + 1) * half)
        ag = jnp.dot(_gelu(y[rows]).astype(BF16), wout_ref[...], preferred_element_type=F32)
        mids.append(x[rows] + mod[2:3] * (ag[:, :d] * jax.nn.sigmoid(ag[:, d:])))
    outs = _ffn_blocks(mids, [mod, mod], gf_ref[...], w1_ref, w2_ref, th, g_final)
    o_ref[...] = jnp.concatenate(outs, axis=0).reshape(o_ref.shape)
    _cast_refs(rest[n_in:n_in + n_cast], rest[n_in + n_cast + 1:])


def _ssm_out_ffn(x, y, st, mods, layer, w_out, gf, w1, w2, pos=None, g_final=None, casts=(), th=1024):
    d = x.shape[-1]
    x_specs, x_args = _x_operands(x, st, pos)
    last = g_final is not None
    c_in, c_out, c_shapes = st.cast_plan(casts)
    out, *cast = pl.pallas_call(
        functools.partial(_ssm_out_ffn_kernel, has_pos=len(x_args) == 2, has_final=last, th=th, n_cast=len(casts)),
        grid=st.row_grid(),
        in_specs=x_specs + [st.row_spec(d), st.mod_spec(layer, d), _resident(w_out.shape), _resident((1, d)),
                            _resident(w1.shape), _resident(w2.shape)] + ([_resident((1, d))] if last else []) + c_in,
        out_specs=[st.nat_spec(d) if last else st.row_spec(d)] + c_out,
        out_shape=[jax.ShapeDtypeStruct((st.nb, st.nc, SSM_CHUNK, d) if last else st.row_shape(d), F32)] + c_shapes,
        compiler_params=_cparams(st.n_axes()), name="ssm_out_ffn",
    )(*x_args, st.row_view(st.unview(y)), mods, w_out, gf, w1, w2, *([g_final] if last else []),
      *[stack for stack, _ in casts])
    return (out if last else st.unview(out)), cast


def _ssm_operators(prow, bt, ct):
    nk = SSM_CHUNK
    half = prow.shape[1] // 2
    lr, li = prow[0:1], prow[1:2]
    dt = jnp.exp(prow[2:3])
    mag = jnp.exp(lr * dt)
    ar, ai = mag * jnp.cos(li * dt), mag * jnp.sin(li * dt)
    den = lr * lr + li * li
    nr, ni = _cmul(ar - 1.0, ai, lr, -li)
    bbr, bbi = _cmul(nr / den, ni / den, bt[0:nk], bt[nk:2 * nk])
    ctr, cti = ct[0:nk], ct[nk:2 * nk]

    pr, pi = [jnp.ones_like(ar)], [jnp.zeros_like(ar)]
    for _ in range(nk):
        r, i = _cmul(pr[-1], pi[-1], ar, ai)
        pr.append(r)
        pi.append(i)
    is_fwd = lax.broadcasted_iota(jnp.int32, (1, 2 * half), 1) < half

    def pw(kf, kb):
        return jnp.where(is_fwd, pr[kf], pr[kb]), jnp.where(is_fwd, pi[kf], pi[kb])

    wp_r, wp_i, ca_r, ca_i, wc_r, wc_i = [], [], [], [], [], []
    for t in range(nk):
        r, i = _cmul(bbr, bbi, *pw(nk - 1 - t, t))
        wp_r.append(r)
        wp_i.append(i)
        r, i = _cmul(ctr, cti, *pw(t, nk - 1 - t))
        ca_r.append(r)
        ca_i.append(i)
        r, i = _cmul(ctr, cti, *pw(t + 1, nk - t))
        wc_r.append(r)
        wc_i.append(-i)
    cat = jnp.concatenate
    wp = cat([cat(wp_r, 0), cat(wp_i, 0)], axis=1).astype(BF16)
    ca = cat([cat(ca_r, 0), cat(ca_i, 0)], axis=1)
    wct = cat([cat(wc_r, 0), cat(wc_i, 0)], axis=1).astype(BF16)

    zero = jnp.zeros_like(bbr)
    lhs = cat([cat([jnp.where(is_fwd, bbr, zero), jnp.where(is_fwd, -bbi, zero)], 1),
               cat([jnp.where(is_fwd, zero, bbr), jnp.where(is_fwd, zero, -bbi)], 1)], 0)
    kk = lax.dot_general(lhs, ca, NT, precision=lax.Precision.HIGHEST, preferred_element_type=F32)
    width = nk * SSM_GROUP
    lane = lax.broadcasted_iota(jnp.int32, (nk, width), 1)
    row = lax.broadcasted_iota(jnp.int32, (nk, width), 0)
    d_lanes = cat([prow[3:4], jnp.zeros_like(prow[3:4])], axis=1)
    kf = kk[0:nk] + jnp.where(lane == row, d_lanes, 0.0)
    kb = kk[nk:2 * nk]
    blocks = []
    for s in range(nk):
        f = kf if s == 0 else pltpu.roll(kf, SSM_GROUP * s, 1)
        sh = (SSM_GROUP * (s + 1)) % width
        b = kb if sh == 0 else pltpu.roll(kb, sh, 1)
        blocks.append(jnp.where(lane >= SSM_GROUP * s, f, 0.0) + jnp.where(lane < SSM_GROUP * (s + 1), b, 0.0))
    tg = cat(blocks, 0).astype(BF16)
    return tg, wp, wct, pr[nk], pi[nk]


def _ssm_core_kernel(prow_ref, bt_ref, ct_ref, h0r_ref, h0i_ref, up_ref, us_ref,
                     yp_ref, ys_ref, fr_ref, fi_ref, atp_scr, ats_scr, p_scr, q_scr, sa_scr, sb_scr,
                     *, p_batch, s_batch):
    nk = SSM_CHUNK
    n_groups = LANES // SSM_GROUP
    cb_p, cb_s = atp_scr.shape[2], ats_scr.shape[2]
    w2 = prow_ref.shape[2]
    for t in range(nk):
        atp_scr[t] = up_ref[pl.ds(t, cb_p, stride=nk), :].T
        ats_scr[t] = us_ref[pl.ds(t, cb_s, stride=nk), :].T

    lane1 = lax.broadcasted_iota(jnp.int32, (1, w2), 1)
    is_fwd = lane1 < (w2 // 2)
    fwd_s = (lax.broadcasted_iota(jnp.int32, (1, 2 * w2), 1) % w2) < (w2 // 2)

    def scan(n_tiles, rows, src_scr, cur_r, cur_i, m_r, m_i, unroll):
        def step(j, carry):
            c_r, c_i = carry
            lo, hi = j * rows, (n_tiles - 1 - j) * rows
            if not isinstance(j, int):
                lo, hi = pl.multiple_of(lo, rows), pl.multiple_of(hi, rows)
            sa_scr[pl.ds(lo, rows), 0:w2] = c_r
            sa_scr[pl.ds(lo, rows), w2:2 * w2] = c_i
            sb_scr[pl.ds(hi, rows), 0:w2] = c_r
            sb_scr[pl.ds(hi, rows), w2:2 * w2] = c_i
            s_r = jnp.where(is_fwd, src_scr[pl.ds(lo, rows), 0:w2], src_scr[pl.ds(hi, rows), 0:w2])
            s_i = jnp.where(is_fwd, src_scr[pl.ds(lo, rows), w2:2 * w2], src_scr[pl.ds(hi, rows), w2:2 * w2])
            n_r, n_i = _cmul(m_r, m_i, c_r, c_i)
            return n_r + s_r, n_i + s_i

        if unroll is True:
            carry = (cur_r, cur_i)
            for j in range(n_tiles):
                carry = step(j, carry)
            return carry
        return lax.fori_loop(0, n_tiles, step, (cur_r, cur_i), unroll=unroll)

    def group(g, _):
        r0 = pl.multiple_of(g * SSM_GROUP, SSM_GROUP)
        tg, wp, wct, a_r, a_i = _ssm_operators(prow_ref[g], bt_ref[g], ct_ref[g])

        def mix(at_scr, cb, carry_fn):
            a = at_scr[:, pl.ds(r0, SSM_GROUP), :].reshape(nk * SSM_GROUP, cb)
            u = a.T.astype(BF16)
            p_scr[0:cb, :] = jnp.dot(u, wp, preferred_element_type=F32)
            fin = carry_fn(cb)
            s = jnp.where(fwd_s, sa_scr[0:cb, :], sb_scr[0:cb, :]).astype(BF16)
            y = jnp.dot(u, tg, preferred_element_type=F32) + lax.dot_general(s, wct, NT, preferred_element_type=F32)
            at_scr[:, pl.ds(r0, SSM_GROUP), :] = y.T.reshape(nk, SSM_GROUP, cb)
            return fin

        def prompt_carry(cb):
            z = jnp.zeros((p_batch, w2), F32)
            return scan(cb // p_batch, p_batch, p_scr, z, z, a_r, a_i, True)

        def sample_carry(cb):
            assert 2 * s_batch == SUBLANES
            p_r, p_i = p_scr[0:cb, 0:w2], p_scr[0:cb, w2:2 * w2]
            ap_r, ap_i = _cmul(a_r, a_i, p_r, p_i)
            q_scr[0:cb, 0:w2] = jnp.where(is_fwd, pltpu.roll(p_r, cb - s_batch, 0), pltpu.roll(p_r, s_batch, 0)) + ap_r
            q_scr[0:cb, w2:2 * w2] = jnp.where(is_fwd, pltpu.roll(p_i, cb - s_batch, 0), pltpu.roll(p_i, s_batch, 0)) + ap_i
            h_r, h_i = h0r_ref[g], h0i_ref[g]
            row = lax.broadcasted_iota(jnp.int32, (SUBLANES, w2), 0)
            keep = jnp.where(is_fwd, 1, 0) == jnp.where(row < s_batch, 1, 0)
            e_r = pltpu.roll(jnp.where(is_fwd, p_r[0:SUBLANES], p_r[cb - SUBLANES:cb]), s_batch, 0)
            e_i = pltpu.roll(jnp.where(is_fwd, p_i[0:SUBLANES], p_i[cb - SUBLANES:cb]), s_batch, 0)
            ah_r, ah_i = _cmul(a_r, a_i, h_r, h_i)
            c_r = jnp.where(keep, h_r, ah_r + e_r)
            c_i = jnp.where(keep, h_i, ah_i + e_i)
            a2_r, a2_i = _cmul(a_r, a_i, a_r, a_i)
            return scan(cb // SUBLANES, SUBLANES, q_scr, c_r, c_i, a2_r, a2_i, True)

        f_r, f_i = mix(atp_scr, cb_p, prompt_carry)
        fr_ref[g] = f_r
        fi_ref[g] = f_i
        mix(ats_scr, cb_s, sample_carry)
        return 0

    lax.fori_loop(0, n_groups, group, 0, unroll=GROUP_UNROLL)
    for t in range(nk):
        yp_ref[pl.ds(t, cb_p, stride=nk), :] = atp_scr[t].T
        ys_ref[pl.ds(t, cb_s, stride=nk), :] = ats_scr[t].T


def _ssm_core(prow, bt, ct, h0r, h0i, up, us, p_batch, s_batch):
    groups, _, w2 = prow.shape
    gpb = LANES // SSM_GROUP
    np_rows, ns_rows = up.shape[0], us.shape[0]
    cb_p, cb_s = np_rows // SSM_CHUNK, ns_rows // SSM_CHUNK
    cb_max = max(cb_p, cb_s)

    def gspec(r, c):
        return pl.BlockSpec((gpb, r, c), lambda o: (o, 0, 0))

    def lane_spec(n):
        return pl.BlockSpec((n, LANES), lambda o: (0, o))

    return pl.pallas_call(
        functools.partial(_ssm_core_kernel, p_batch=p_batch, s_batch=s_batch),
        grid=(groups // gpb,),
        in_specs=[gspec(SUBLANES, w2), gspec(2 * SSM_GROUP, w2), gspec(2 * SSM_GROUP, w2),
                  gspec(SUBLANES, w2), gspec(SUBLANES, w2), lane_spec(np_rows), lane_spec(ns_rows)],
        out_specs=[lane_spec(np_rows), lane_spec(ns_rows), gspec(p_batch, w2), gspec(p_batch, w2)],
        out_shape=[jax.ShapeDtypeStruct(up.shape, F32), jax.ShapeDtypeStruct(us.shape, F32),
                   jax.ShapeDtypeStruct((groups, p_batch, w2), F32),
                   jax.ShapeDtypeStruct((groups, p_batch, w2), F32)],
        scratch_shapes=[pltpu.VMEM((SSM_CHUNK, LANES, cb_p), F32), pltpu.VMEM((SSM_CHUNK, LANES, cb_s), F32)]
        + [pltpu.VMEM((cb_max, 2 * w2), F32)] * 4,
        compiler_params=_cparams(), name="ssm_core",
    )(prow, bt, ct, h0r, h0i, up, us)


def _ssm_layer(xp, xs, stp, sts, mods, layer, g_mix, w_in, lam_re, lam_im, log_dt, b_re, b_im, c_re, c_im, d_skip,
               w_out, h0_re, h0_im, ffn, pos=None, g_final=None, casts=()):
    width = w_in.shape[1]
    groups = width // SSM_GROUP
    n_state = lam_re.shape[-1]
    up = _ssm_in(xp, stp, mods, layer, g_mix, w_in)
    us = _ssm_in(xs, sts, mods, layer, g_mix, w_in, pos)

    def lanes_dir_state(v):
        return v.transpose(1, 0, 2).reshape(groups, 2 * n_state)

    d_rows = jnp.pad(d_skip.reshape(groups, SSM_GROUP), ((0, 0), (0, 2 * n_state - SSM_GROUP)))
    prow = jnp.stack([lanes_dir_state(lam_re), lanes_dir_state(lam_im),
                      lanes_dir_state(jnp.broadcast_to(log_dt[..., None], lam_re.shape)), d_rows], axis=1)
    prow = jnp.pad(prow, ((0, 0), (0, SUBLANES - 4), (0, 0)))
    bt = jnp.concatenate([b_re.transpose(1, 3, 0, 2).reshape(groups, SSM_GROUP, 2 * n_state),
                          b_im.transpose(1, 3, 0, 2).reshape(groups, SSM_GROUP, 2 * n_state)], axis=1)
    ct = jnp.concatenate([c_re.transpose(1, 2, 0, 3).reshape(groups, SSM_GROUP, 2 * n_state),
                          c_im.transpose(1, 2, 0, 3).reshape(groups, SSM_GROUP, 2 * n_state)], axis=1)

    def h0_rows(h):
        h = h.transpose(2, 0, 1, 3).reshape(groups, sts.nb, 2 * n_state)
        return jnp.concatenate([h] * (SUBLANES // sts.nb), axis=1)

    yp, ys, f_re, f_im = _ssm_core(prow, bt, ct, h0_rows(h0_re), h0_rows(h0_im), up, us, stp.nb, sts.nb)
    xp, cast = _ssm_out_ffn(xp, yp, stp, mods, layer, w_out, *ffn, g_final=g_final, casts=casts)
    xs, _ = _ssm_out_ffn(xs, ys, sts, mods, layer, w_out, *ffn, pos=pos, g_final=g_final)

    def final(f):
        return f.reshape(groups, stp.nb, 2, n_state).transpose(1, 2, 0, 3)

    return xp, xs, final(f_re), final(f_im), cast


def _seq_rows(ref, k):
    return ref[:, k].reshape(-1, ref.shape[-1])


def _gmlp_kernel(x_ref, mod_ref, g_ref, win_ref, ws_ref, bs_ref, wout_ref, o_ref, t_scr):
    n_groups, chunk, gd = bs_ref.shape
    for k in range(SEQ_PAIR):
        x = _seq_rows(x_ref, k)
        mod = mod_ref[k]
        hn = _norm_mod(x, g_ref[...], mod[0:1], mod[1:2]).astype(BF16)
        z = _gelu(jnp.dot(hn, win_ref[...], preferred_element_type=F32))
        wdt = z.shape[1] // 2
        u, v = z[:, :wdt], z[:, wdt:]
        vc = v - jnp.mean(v, axis=-1, keepdims=True)
        vn = (vc * lax.rsqrt(jnp.mean(vc * vc, axis=-1, keepdims=True) + EPS)).astype(BF16)
        for c in range(x.shape[0] // chunk):
            rows = slice(c * chunk, (c + 1) * chunk)
            for g in range(n_groups):
                cols = slice(g * gd, (g + 1) * gd)
                s = jnp.dot(ws_ref[g], vn[rows, cols], preferred_element_type=F32) + bs_ref[g]
                t_scr[k, rows, cols] = (u[rows, cols] * s).astype(BF16)
        out = jnp.dot(t_scr[k], wout_ref[...], preferred_element_type=F32)
        o_ref[:, k] = (x + mod[2:3] * out).reshape(o_ref.shape[0], o_ref.shape[2], o_ref.shape[3])


def _gmlp_layer(x, st, mods, layer, g_mix, w_in, w_s, bs, w_out):
    d = x.shape[-1]
    return pl.pallas_call(
        _gmlp_kernel,
        grid=st.seq_grid(),
        in_specs=[st.seq_spec(d), st.seq_mod_spec(layer, d), _resident((1, d)), _resident(w_in.shape),
                  _resident(w_s.shape), _resident(bs.shape), _resident(w_out.shape)],
        out_specs=st.seq_spec(d),
        out_shape=jax.ShapeDtypeStruct(x.shape, F32),
        scratch_shapes=[pltpu.VMEM((SEQ_PAIR, SEQ_CHUNKS * SSM_CHUNK, w_out.shape[0]), BF16)],
        compiler_params=_cparams(2), name="gmlp",
    )(x, mods, g_mix, w_in, w_s, bs, w_out)


def _conv_ffn_kernel(x_ref, xprev_ref, xnext_ref, mod_ref, g_ref, win_ref, cw_ref, wout_ref, gf_ref, w1_ref, w2_ref,
                     *rest, seq_blocks, th, n_cast):
    o_ref = rest[n_cast]
    j = pl.program_id(1)
    g = g_ref[...]
    cw = cw_ref[...]
    d = x_ref.shape[-1]
    xs = [_seq_rows(x_ref, k) for k in range(SEQ_PAIR)]
    mods = [mod_ref[k] for k in range(SEQ_PAIR)]
    hh = jnp.concatenate([_norm_mod(jnp.concatenate([xprev_ref[k], xnext_ref[k]], axis=0), g, m[0:1], m[1:2])
                          for k, m in enumerate(mods)], axis=0).astype(BF16)
    zh = jnp.dot(hh, win_ref[:, d:], preferred_element_type=F32)
    th_all = zh[:, :d] * zh[:, d:]
    mids = []
    for k, (x, mod) in enumerate(zip(xs, mods)):
        tm = x.shape[0]
        hn = _norm_mod(x, g, mod[0:1], mod[1:2]).astype(BF16)
        z = jnp.dot(hn, win_ref[...], preferred_element_type=F32)
        gb, t = z[:, :d], z[:, d:2 * d] * z[:, 2 * d:]
        r0 = 2 * SUBLANES * k
        t_prev = jnp.where(j > 0, th_all[r0 + SUBLANES - 1:r0 + SUBLANES], 0.0)
        t_next = jnp.where(j < seq_blocks - 1, th_all[r0 + SUBLANES:r0 + SUBLANES + 1], 0.0)
        row = lax.broadcasted_iota(jnp.int32, (tm, 1), 0)
        up = jnp.where(row == 0, t_prev, pltpu.roll(t, 1, 0))
        dn = jnp.where(row == tm - 1, t_next, pltpu.roll(t, tm - 1, 0))
        y = cw[0:1] * up + cw[1:2] * t + cw[2:3] * dn
        out = jnp.dot((gb * y).astype(BF16), wout_ref[...], preferred_element_type=F32)
        mids.append(x + mod[2:3] * out)
    outs = _ffn_blocks(mids, mods, gf_ref[...], w1_ref, w2_ref, th)
    for k, out in enumerate(outs):
        o_ref[:, k] = out.reshape(o_ref.shape[0], o_ref.shape[2], o_ref.shape[3])
    _cast_refs(rest[:n_cast], rest[n_cast + 1:])


def _conv_ffn_layer(x, st, mods, layer, g_mix, w_in, cw, w_out, gf, w1, w2, casts=(), th=1024):
    d = x.shape[-1]
    prev, nxt = st.halo_specs(d)
    xh = st.halo_view(x)
    c_in, c_out, c_shapes = st.cast_plan(casts, seq=True)
    out, *cast = pl.pallas_call(
        functools.partial(_conv_ffn_kernel, seq_blocks=st.seq_blocks, th=th, n_cast=len(casts)),
        grid=st.seq_grid(),
        in_specs=[st.seq_spec(d), prev, nxt, st.seq_mod_spec(layer, d), _resident((1, d)), _resident(w_in.shape),
                  _resident(cw.shape), _resident(w_out.shape), _resident((1, d)), _resident(w1.shape),
                  _resident(w2.shape)] + c_in,
        out_specs=[st.seq_spec(d)] + c_out,
        out_shape=[jax.ShapeDtypeStruct(x.shape, F32)] + c_shapes,
        compiler_params=_cparams(2), name="conv_ffn",
    )(x, xh, xh, mods, g_mix, w_in, cw, w_out, gf, w1, w2, *[stack for stack, _ in casts])
    return out, cast


def kernel(x_prompt, x_sample, state_ssm_re, state_ssm_im, c, c_ctx, w_mod, b_mod, g_mix, g_ffn, ffn_w1, ffn_w2, ssm_w_in, ssm_lam_re, ssm_lam_im, ssm_log_dt, ssm_b_re, ssm_b_im, ssm_c_re, ssm_c_im, ssm_d, ssm_w_out, gmlp_w_in, gmlp_w_s, gmlp_b_s, gmlp_w_out, conv_w_in, conv_w, conv_w_out, g_final):
    n_prompt, len_prompt, d = x_prompt.shape
    n_sample, len_sample, _ = x_sample.shape
    depth = w_mod.shape[0]
    n_mixers = 3
    assert 1 + n_sample <= SUBLANES
    stp = _Stream(len_prompt // SSM_CHUNK, n_prompt, per_batch_cond=False)
    sts = _Stream(len_sample // SSM_CHUNK, n_sample, per_batch_cond=True)

    xp, xs = x_prompt, x_sample
    pos = _grid_pos_embed(len_sample, d)

    cond = jnp.concatenate([c_ctx[None, :], c, jnp.zeros((SUBLANES - 1 - n_sample, d), F32)], axis=0)
    mods = _adaln(cond.T, w_mod, b_mod, 1 + n_sample)
    table = [0] * SEQ_PAIR + list(range(1, 1 + n_sample))
    mods = mods[:, jnp.array(table)].reshape(depth, len(table), N_MOD, d)

    assert (depth - 1) % n_mixers == 0, "the last layer's kernel writes the natural-order outputs"

    def layer_weights(i):
        kind, j = i % n_mixers, i // n_mixers
        mixer = {0: [(ssm_w_in, j), (ssm_w_out, j)],
                 1: [(gmlp_w_in, j), (gmlp_w_s.reshape(gmlp_w_s.shape[0], -1, gmlp_w_s.shape[-1]), j), (gmlp_w_out, j)],
                 2: [(conv_w_in, j), (conv_w_out, j)]}[kind]
        return mixer + [(ffn_w1, i), (ffn_w2, i)]

    wb = [stack[layer].astype(BF16) for stack, layer in layer_weights(0)]
    new_re, new_im = [], []
    for i in range(depth):
        kind, j = i % n_mixers, i // n_mixers
        gm = g_mix[i].reshape(1, d)
        nxt = layer_weights(i + 1) if i + 1 < depth else []
        *wm, w1, w2 = wb
        ffn = (g_ffn[i].reshape(1, d), w1, w2)
        if kind == 0:
            xp, xs, f_re, f_im, wb = _ssm_layer(
                xp, xs, stp, sts, mods, i, gm, wm[0], ssm_lam_re[j], ssm_lam_im[j], ssm_log_dt[j],
                ssm_b_re[j], ssm_b_im[j], ssm_c_re[j], ssm_c_im[j], ssm_d[j], wm[1],
                state_ssm_re[:, j], state_ssm_im[:, j], ffn, pos if i == 0 else None,
                g_final.reshape(1, d) if i == depth - 1 else None, nxt)
            new_re.append(f_re)
            new_im.append(f_im)
        elif kind == 1:
            n_groups, chunk, _ = gmlp_w_s[j].shape
            gd = gmlp_w_out[j].shape[0] // n_groups
            bs = jnp.broadcast_to(gmlp_b_s[j][:, :, None], (n_groups, chunk, gd))
            ws = (wm[0], wm[1].reshape(gmlp_w_s[j].shape), bs, wm[2])
            xp, wb = _ffn(_gmlp_layer(xp, stp, mods, i, gm, *ws), stp, mods, i, *ffn, casts=nxt)
            xs, _ = _ffn(_gmlp_layer(xs, sts, mods, i, gm, *ws), sts, mods, i, *ffn)
        else:
            ws = (wm[0], jnp.pad(conv_w[j], ((0, SUBLANES - CONV_WIDTH), (0, 0))), wm[1])
            xp, wb = _conv_ffn_layer(xp, stp, mods, i, gm, *ws, *ffn, casts=nxt)
            xs, _ = _conv_ffn_layer(xs, sts, mods, i, gm, *ws, *ffn)

    return (xp.reshape(x_prompt.shape), xs.reshape(x_sample.shape),
            jnp.stack(new_re, axis=1), jnp.stack(new_im, axis=1))
```

```python
import functools
import math

import jax
import jax.numpy as jnp
from jax import lax
from jax.experimental import pallas as pl
from jax.experimental.pallas import tpu as pltpu

EPS = 1e-6
N_MOD = 6
GRID_W = 64
SSM_GROUP = 16
SSM_CHUNK = 16
CONV_WIDTH = 3
SUBLANES = 8
LANES = 128
ROW_BLOCK = 512
SEQ_CHUNKS = 16
SEQ_PAIR = 2
VMEM_LIMIT = 56 * 1024 * 1024

F32 = jnp.float32
BF16 = jnp.bfloat16
NT = (((1,), (1,)), ((), ()))


def _cparams(n_axes=1, vmem=VMEM_LIMIT):
    return pltpu.CompilerParams(dimension_semantics=("arbitrary",) * n_axes, vmem_limit_bytes=vmem)


def _gelu(x):
    return 0.5 * x * (1.0 + jnp.tanh(math.sqrt(2.0 / math.pi) * (x + 0.044715 * (x * x * x))))


def _norm_mod(x, g, shift, scale):
    y = x * lax.rsqrt(jnp.mean(x * x, axis=-1, keepdims=True) + EPS)
    return (y * g) * (1.0 + scale) + shift


def _cmul(ar, ai, br, bi):
    return ar * br - ai * bi, ar * bi + ai * br


def _resident(shape):
    nd = len(shape)
    return pl.BlockSpec(shape, lambda *_: (0,) * nd, pipeline_mode=pl.Buffered(1))


def _cast_plan(srcs, n_steps, step_of):
    ins, outs, shapes = [], [], []
    for stack, layer in srcs:
        _, r, c = stack.shape
        blk = r // n_steps
        assert r % n_steps == 0 and blk % (2 * SUBLANES) == 0
        ins.append(pl.BlockSpec((None, blk, c), lambda *ids, layer=layer: (layer, step_of(*ids), 0)))
        outs.append(pl.BlockSpec((blk, c), lambda *ids: (step_of(*ids), 0)))
        shapes.append(jax.ShapeDtypeStruct((r, c), BF16))
    return ins, outs, shapes


def _cast_refs(src_refs, dst_refs):
    for src, dst in zip(src_refs, dst_refs):
        dst[...] = src[...].astype(BF16)


class _Stream:
    def __init__(self, n_chunks, n_batch, per_batch_cond):
        self.nc, self.nb = n_chunks, n_batch
        self.n_rows = n_chunks * n_batch * SSM_CHUNK
        self.per_batch_cond = per_batch_cond
        self.row_chunks = ROW_BLOCK // SSM_CHUNK
        self.seq_blocks = n_chunks // SEQ_CHUNKS
        assert n_chunks % SEQ_CHUNKS == 0 and self.n_rows % ROW_BLOCK == 0 and n_batch % SEQ_PAIR == 0
        assert not per_batch_cond or n_chunks % self.row_chunks == 0

    def row_grid(self):
        return (self.nb, self.nc // self.row_chunks) if self.per_batch_cond else (self.n_rows // ROW_BLOCK,)

    def row_view(self, x):
        return x if self.per_batch_cond else x.reshape(self.n_rows, x.shape[-1])

    def row_spec(self, width):
        if self.per_batch_cond:
            return pl.BlockSpec((self.row_chunks, None, SSM_CHUNK, width), lambda b, j: (j, b, 0, 0))
        return pl.BlockSpec((ROW_BLOCK, width), lambda i: (i, 0))

    def row_shape(self, width):
        return (self.nc, self.nb, SSM_CHUNK, width) if self.per_batch_cond else (self.n_rows, width)

    def unview(self, x):
        return x.reshape(self.nc, self.nb, SSM_CHUNK, x.shape[-1])

    def nat_view(self, x):
        return x.reshape(self.nb, self.nc, SSM_CHUNK, x.shape[-1])

    def nat_spec(self, width):
        if self.per_batch_cond:
            return pl.BlockSpec((None, self.row_chunks, SSM_CHUNK, width), lambda b, j: (b, j, 0, 0))
        assert self.nb * SSM_CHUNK == ROW_BLOCK
        return pl.BlockSpec((self.nb, None, SSM_CHUNK, width), lambda i: (0, i, 0, 0))

    def pos_spec(self, width):
        return pl.BlockSpec((self.row_chunks, SSM_CHUNK, width), lambda b, j: (j, 0, 0))

    def seq_grid(self):
        return (self.nb // SEQ_PAIR, self.seq_blocks)

    def seq_spec(self, width):
        return pl.BlockSpec((SEQ_CHUNKS, SEQ_PAIR, SSM_CHUNK, width), lambda b, j: (j, b, 0, 0))

    def halo_specs(self, width):
        halves = SSM_CHUNK // SUBLANES
        shape = (None, SEQ_PAIR, None, SUBLANES, width)
        prev = pl.BlockSpec(shape, lambda b, j: (jnp.maximum(j * SEQ_CHUNKS - 1, 0), b, halves - 1, 0, 0))
        nxt = pl.BlockSpec(shape, lambda b, j: (jnp.minimum((j + 1) * SEQ_CHUNKS, self.nc - 1), b, 0, 0, 0))
        return prev, nxt

    def halo_view(self, x):
        return x.reshape(self.nc, self.nb, SSM_CHUNK // SUBLANES, SUBLANES, x.shape[-1])

    def mod_spec(self, layer, d):
        if self.per_batch_cond:
            return pl.BlockSpec((None, None, N_MOD, d), lambda b, j: (layer, SEQ_PAIR + b, 0, 0))
        return pl.BlockSpec((None, None, N_MOD, d), lambda *_: (layer, 0, 0, 0))

    def seq_mod_spec(self, layer, d):
        if self.per_batch_cond:
            return pl.BlockSpec((None, SEQ_PAIR, N_MOD, d), lambda b, j: (layer, 1 + b, 0, 0))
        return pl.BlockSpec((None, SEQ_PAIR, N_MOD, d), lambda *_: (layer, 0, 0, 0))

    def n_axes(self, seq=False):
        return 2 if (seq or self.per_batch_cond) else 1

    def cast_plan(self, srcs, seq=False):
        if seq:
            return _cast_plan(srcs, (self.nb // SEQ_PAIR) * self.seq_blocks, lambda b, j: b * self.seq_blocks + j)
        assert not srcs or not self.per_batch_cond
        return _cast_plan(srcs, self.n_rows // ROW_BLOCK, lambda i: i)


def _grid_pos_embed(n_tokens, d):
    rows = n_tokens // GRID_W
    t = jnp.arange(rows * GRID_W)
    r = (t // GRID_W).astype(F32)
    col = (t % GRID_W).astype(F32)
    quarter = d // 4
    freq = 1.0 / (10000.0 ** (jnp.arange(quarter, dtype=F32) / quarter))
    ar = r[:, None] * freq
    ac = col[:, None] * freq
    return jnp.concatenate([jnp.sin(ar), jnp.cos(ar), jnp.sin(ac), jnp.cos(ac)], axis=-1)


def _adaln_kernel(ct_ref, w_ref, b_ref, o_ref, *, n_cond):
    ct = ct_ref[...]
    s = ct * jax.nn.sigmoid(ct)
    w = w_ref[...]
    rows = [jnp.sum(s[:, r:r + 1] * w, axis=0, keepdims=True) for r in range(n_cond)]
    rows += [jnp.zeros_like(rows[0])] * (SUBLANES - n_cond)
    o_ref[...] = jnp.concatenate(rows, axis=0) + b_ref[...]


def _adaln(cond_t, w_mod, b_mod, n_cond, tn=1024):
    depth, d, n = w_mod.shape
    return pl.pallas_call(
        functools.partial(_adaln_kernel, n_cond=n_cond),
        grid=(depth, n // tn),
        in_specs=[
            pl.BlockSpec((d, SUBLANES), lambda l, j: (0, 0)),
            pl.BlockSpec((None, d, tn), lambda l, j: (l, 0, j)),
            pl.BlockSpec((None, 1, tn), lambda l, j: (l, 0, j)),
        ],
        out_specs=pl.BlockSpec((None, SUBLANES, tn), lambda l, j: (l, 0, j)),
        out_shape=jax.ShapeDtypeStruct((depth, SUBLANES, n), F32),
        compiler_params=_cparams(2), name="adaln",
    )(cond_t, w_mod, b_mod.reshape(depth, 1, n))


def _rows(ref):
    return ref[...].reshape(-1, ref.shape[-1])


def _rms(x, g):
    return (x * lax.rsqrt(jnp.mean(x * x, axis=-1, keepdims=True) + EPS)) * g


def _ffn_blocks(xs, mods, g, w1_ref, w2_ref, th, g_final=None):
    hn = jnp.concatenate([_norm_mod(x, g, m[3:4], m[4:5]).astype(BF16) for x, m in zip(xs, mods)], axis=0)
    acc = jnp.zeros((hn.shape[0], w2_ref.shape[1]), F32)
    for c in range(w1_ref.shape[1] // th):
        h1 = jnp.dot(hn, w1_ref[:, c * th:(c + 1) * th], preferred_element_type=F32)
        h1 = jnp.square(jnp.maximum(h1, 0.0)).astype(BF16)
        acc = acc + jnp.dot(h1, w2_ref[c * th:(c + 1) * th, :], preferred_element_type=F32)
    outs, r0 = [], 0
    for x, m in zip(xs, mods):
        out = x + m[5:6] * acc[r0:r0 + x.shape[0]]
        outs.append(out if g_final is None else _rms(out, g_final))
        r0 += x.shape[0]
    return outs


def _ffn_kernel(x_ref, mod_ref, g_ref, w1_ref, w2_ref, *rest, th, n_cast):
    o_ref = rest[n_cast]
    out, = _ffn_blocks([_rows(x_ref)], [mod_ref[...]], g_ref[...], w1_ref, w2_ref, th)
    o_ref[...] = out.reshape(o_ref.shape)
    _cast_refs(rest[:n_cast], rest[n_cast + 1:])


def _ffn(x, st, mods, layer, g, w1, w2, casts=(), th=1024):
    d = x.shape[-1]
    c_in, c_out, c_shapes = st.cast_plan(casts)
    out, *cast = pl.pallas_call(
        functools.partial(_ffn_kernel, th=th, n_cast=len(casts)),
        grid=st.row_grid(),
        in_specs=[st.row_spec(d), st.mod_spec(layer, d), _resident((1, d)), _resident(w1.shape), _resident(w2.shape)]
        + c_in,
        out_specs=[st.row_spec(d)] + c_out,
        out_shape=[jax.ShapeDtypeStruct(st.row_shape(d), F32)] + c_shapes,
        compiler_params=_cparams(st.n_axes()), name="ffn",
    )(st.row_view(x), mods, g, w1, w2, *[stack for stack, _ in casts])
    return st.unview(out), cast


def _x_operands(x, st, pos):
    d = x.shape[-1]
    if x.ndim == 4:
        return [st.row_spec(d)], [st.row_view(x)]
    specs, args = [st.nat_spec(d)], [st.nat_view(x)]
    if pos is not None:
        specs.append(st.pos_spec(d))
        args.append(pos.reshape(st.nc, SSM_CHUNK, d))
    return specs, args


def _ssm_in_kernel(x_ref, *rest):
    *pos_ref, mod_ref, g_ref, w_ref, o_ref = rest
    x = _rows(x_ref) + _rows(pos_ref[0]) if pos_ref else _rows(x_ref)
    mod = mod_ref[...]
    hn = _norm_mod(x, g_ref[...], mod[0:1], mod[1:2]).astype(BF16)
    o_ref[...] = jnp.dot(hn, w_ref[...], preferred_element_type=F32).reshape(o_ref.shape)


def _ssm_in(x, st, mods, layer, g, w, pos=None):
    d, width = x.shape[-1], w.shape[1]
    x_specs, x_args = _x_operands(x, st, pos)
    out = pl.pallas_call(
        _ssm_in_kernel,
        grid=st.row_grid(),
        in_specs=x_specs + [st.mod_spec(layer, d), _resident((1, d)), _resident(w.shape)],
        out_specs=st.row_spec(width),
        out_shape=jax.ShapeDtypeStruct(st.row_shape(width), F32),
        compiler_params=_cparams(st.n_axes()), name="ssm_in",
    )(*x_args, mods, g, w)
    return out.reshape(st.n_rows, width)


def _ssm_out_ffn_kernel(x_ref, *rest, has_pos, has_final, th, n_cast):
    pos_ref = rest[0] if has_pos else None
    y_ref, mod_ref, wout_ref, gf_ref, w1_ref, w2_ref = rest[has_pos:has_pos + 6]
    n_in = has_pos + 6 + has_final
    g_final = rest[n_in - 1][...] if has_final else None
    o_ref = rest[n_in + n_cast]
    x = _rows(x_ref) + _rows(pos_ref) if has_pos else _rows(x_ref)
    y = _rows(y_ref)
    mod = mod_ref[...]
    d = x.shape[-1]
    half = x.shape[0] // 2
    mids = []
    for k in range(2):
        rows = slice(k * half, (k + 1) * half)
        ag = jnp.dot(_gelu(y[rows]).astype(BF16), wout_ref[...], preferred_element_type=F32)
        mids.append(x[rows] + mod[2:3] * (ag[:, :d] * jax.nn.sigmoid(ag[:, d:])))
    outs = _ffn_blocks(mids, [mod, mod], gf_ref[...], w1_ref, w2_ref, th, g_final)
    o_ref[...] = jnp.concatenate(outs, axis=0).reshape(o_ref.shape)
    _cast_refs(rest[n_in:n_in + n_cast], rest[n_in + n_cast + 1:])


def _ssm_out_ffn(x, y, st, mods, layer, w_out, gf, w1, w2, pos=None, g_final=None, casts=(), th=1024):
    d = x.shape[-1]
    x_specs, x_args = _x_operands(x, st, pos)
    last = g_final is not None
    c_in, c_out, c_shapes = st.cast_plan(casts)
    out, *cast = pl.pallas_call(
        functools.partial(_ssm_out_ffn_kernel, has_pos=len(x_args) == 2, has_final=last, th=th, n_cast=len(casts)),
        grid=st.row_grid(),
        in_specs=x_specs + [st.row_spec(d), st.mod_spec(layer, d), _resident(w_out.shape), _resident((1, d)),
                            _resident(w1.shape), _resident(w2.shape)] + ([_resident((1, d))] if last else []) + c_in,
        out_specs=[st.nat_spec(d) if last else st.row_spec(d)] + c_out,
        out_shape=[jax.ShapeDtypeStruct((st.nb, st.nc, SSM_CHUNK, d) if last else st.row_shape(d), F32)] + c_shapes,
        compiler_params=_cparams(st.n_axes()), name="ssm_out_ffn",
    )(*x_args, st.row_view(st.unview(y)), mods, w_out, gf, w1, w2, *([g_final] if last else []),
      *[stack for stack, _ in casts])
    return (out if last else st.unview(out)), cast


def _ssm_operators(prow, bt, ct):
    nk = SSM_CHUNK
    half = prow.shape[1] // 2
    lr, li = prow[0:1], prow[1:2]
    dt = jnp.exp(prow[2:3])
    mag = jnp.exp(lr * dt)
    ar, ai = mag * jnp.cos(li * dt), mag * jnp.sin(li * dt)
    den = lr * lr + li * li
    nr, ni = _cmul(ar - 1.0, ai, lr, -li)
    bbr, bbi = _cmul(nr / den, ni / den, bt[0:nk], bt[nk:2 * nk])
    ctr, cti = ct[0:nk], ct[nk:2 * nk]

    pr, pi = [jnp.ones_like(ar)], [jnp.zeros_like(ar)]
    for _ in range(nk):
        r, i = _cmul(pr[-1], pi[-1], ar, ai)
        pr.append(r)
        pi.append(i)
    is_fwd = lax.broadcasted_iota(jnp.int32, (1, 2 * half), 1) < half

    def pw(kf, kb):
        return jnp.where(is_fwd, pr[kf], pr[kb]), jnp.where(is_fwd, pi[kf], pi[kb])

    wp_r, wp_i, ca_r, ca_i, wc_r, wc_i = [], [], [], [], [], []
    for t in range(nk):
        r, i = _cmul(bbr, bbi, *pw(nk - 1 - t, t))
        wp_r.append(r)
        wp_i.append(i)
        r, i = _cmul(ctr, cti, *pw(t, nk - 1 - t))
        ca_r.append(r)
        ca_i.append(i)
        r, i = _cmul(ctr, cti, *pw(t + 1, nk - t))
        wc_r.append(r)
        wc_i.append(-i)
    cat = jnp.concatenate
    wp = cat([cat(wp_r, 0), cat(wp_i, 0)], axis=1).astype(BF16)
    ca = cat([cat(ca_r, 0), cat(ca_i, 0)], axis=1)
    wct = cat([cat(wc_r, 0), cat(wc_i, 0)], axis=1).astype(BF16)

    zero = jnp.zeros_like(bbr)
    lhs = cat([cat([jnp.where(is_fwd, bbr, zero), jnp.where(is_fwd, -bbi, zero)], 1),
               cat([jnp.where(is_fwd, zero, bbr), jnp.where(is_fwd, zero, -bbi)], 1)], 0)
    kk = lax.dot_general(lhs, ca, NT, precision=lax.Precision.HIGHEST, preferred_element_type=F32)
    width = nk * SSM_GROUP
    lane = lax.broadcasted_iota(jnp.int32, (nk, width), 1)
    row = lax.broadcasted_iota(jnp.int32, (nk, width), 0)
    d_lanes = cat([prow[3:4], jnp.zeros_like(prow[3:4])], axis=1)
    kf = kk[0:nk] + jnp.where(lane == row, d_lanes, 0.0)
    return kf, kk[nk:2 * nk], wp, wct, pr[nk], pi[nk]


def _toeplitz_shifts():
    width = SSM_CHUNK * SSM_GROUP
    src = jnp.arange(width)[:, None]
    dst = jnp.arange(width)[None, :]
    fwd = [dst == src + SSM_GROUP * s for s in range(SSM_CHUNK)]
    bwd = [src == dst + SSM_GROUP * (SSM_CHUNK - 1 - s) for s in range(SSM_CHUNK)]
    return jnp.concatenate(fwd, axis=1).astype(BF16), jnp.concatenate(bwd, axis=1).astype(BF16)


def _ssm_core_kernel(prow_ref, bt_ref, ct_ref, h0r_ref, h0i_ref, shf_ref, shb_ref, up_ref, us_ref,
                     yp_ref, ys_ref, fr_ref, fi_ref,
                     atp_scr, ats_scr, tg_scr, wp_scr, wct_scr, u_scr, p_scr, s_scr, q_scr, *, p_batch, s_batch):
    nk = SSM_CHUNK
    n_groups = LANES // SSM_GROUP
    cb_p, cb_s = atp_scr.shape[2], ats_scr.shape[2]
    w2 = prow_ref.shape[2]
    width = nk * SSM_GROUP
    decay, kfs, kbs = [], [], []
    for g in range(n_groups):
        kf, kb, wp, wct, a_r, a_i = _ssm_operators(prow_ref[g], bt_ref[g], ct_ref[g])
        kfs.append(kf)
        kbs.append(kb)
        wp_scr[g] = wp
        wct_scr[g] = wct
        decay.append((a_r, a_i))
    tz = (jnp.dot(jnp.concatenate(kfs, axis=0).astype(BF16), shf_ref[...], preferred_element_type=F32)
          + jnp.dot(jnp.concatenate(kbs, axis=0).astype(BF16), shb_ref[...], preferred_element_type=F32))
    for g in range(n_groups):
        for s in range(nk):
            tg_scr[g, s * SSM_GROUP:(s + 1) * SSM_GROUP, :] = (
                tz[g * SSM_GROUP:(g + 1) * SSM_GROUP, s * width:(s + 1) * width].astype(BF16))

    is_fwd = lax.broadcasted_iota(jnp.int32, (1, w2), 1) < (w2 // 2)

    def scan(n_tiles, rows, src_scr, carries, mults):
        for j in range(n_tiles):
            lo, hi = j * rows, (n_tiles - 1 - j) * rows
            for g in range(n_groups):
                c_r, c_i = carries[g]
                for c, l0 in ((c_r, 0), (c_i, w2)):
                    if lo < hi:
                        s_scr[g, lo:lo + rows, l0:l0 + w2] = c
                        s_scr[g, hi:hi + rows, l0:l0 + w2] = c
                    else:
                        s_scr[g, lo:lo + rows, l0:l0 + w2] = jnp.where(is_fwd, c, s_scr[g, lo:lo + rows, l0:l0 + w2])
                        s_scr[g, hi:hi + rows, l0:l0 + w2] = jnp.where(is_fwd, s_scr[g, hi:hi + rows, l0:l0 + w2], c)
                s_r = jnp.where(is_fwd, src_scr[g, lo:lo + rows, 0:w2], src_scr[g, hi:hi + rows, 0:w2])
                s_i = jnp.where(is_fwd, src_scr[g, lo:lo + rows, w2:2 * w2], src_scr[g, hi:hi + rows, w2:2 * w2])
                n_r, n_i = _cmul(*mults[g], c_r, c_i)
                carries[g] = (n_r + s_r, n_i + s_i)
        return carries

    def prompt_carry(cb):
        z = jnp.zeros((p_batch, w2), F32)
        return scan(cb // p_batch, p_batch, p_scr, [(z, z)] * n_groups, decay)

    def sample_carry(cb):
        assert 2 * s_batch == SUBLANES
        row = lax.broadcasted_iota(jnp.int32, (SUBLANES, w2), 0)
        keep = jnp.where(is_fwd, 1, 0) == jnp.where(row < s_batch, 1, 0)
        carries, mults = [], []
        for g in range(n_groups):
            a_r, a_i = decay[g]
            p_r, p_i = p_scr[g, 0:cb, 0:w2], p_scr[g, 0:cb, w2:2 * w2]
            ap_r, ap_i = _cmul(a_r, a_i, p_r, p_i)
            q_scr[g, 0:cb, 0:w2] = jnp.where(is_fwd, pltpu.roll(p_r, cb - s_batch, 0), pltpu.roll(p_r, s_batch, 0)) + ap_r
            q_scr[g, 0:cb, w2:2 * w2] = jnp.where(is_fwd, pltpu.roll(p_i, cb - s_batch, 0),
                                                  pltpu.roll(p_i, s_batch, 0)) + ap_i
            h_r, h_i = h0r_ref[g], h0i_ref[g]
            e_r = pltpu.roll(jnp.where(is_fwd, p_r[0:SUBLANES], p_r[cb - SUBLANES:cb]), s_batch, 0)
            e_i = pltpu.roll(jnp.where(is_fwd, p_i[0:SUBLANES], p_i[cb - SUBLANES:cb]), s_batch, 0)
            ah_r, ah_i = _cmul(a_r, a_i, h_r, h_i)
            carries.append((jnp.where(keep, h_r, ah_r + e_r), jnp.where(keep, h_i, ah_i + e_i)))
            mults.append(_cmul(a_r, a_i, a_r, a_i))
        return scan(cb // SUBLANES, SUBLANES, q_scr, carries, mults)

    def mix(u_ref, y_ref, at_scr, cb, carry_fn):
        for t in range(nk):
            at_scr[t] = u_ref[pl.ds(t, cb, stride=nk), :].astype(BF16).T
        for g in range(n_groups):
            a = at_scr[:, g * SSM_GROUP:(g + 1) * SSM_GROUP, :].reshape(nk * SSM_GROUP, cb)
            u = a.T
            u_scr[g, 0:cb, :] = u
            p_scr[g, 0:cb, :] = jnp.dot(u, wp_scr[g], preferred_element_type=F32)
        finals = carry_fn(cb)
        for g in range(n_groups):
            s = s_scr[g, 0:cb, :].astype(BF16)
            y = (jnp.dot(u_scr[g, 0:cb, :], tg_scr[g], preferred_element_type=F32)
                 + lax.dot_general(s, wct_scr[g], NT, preferred_element_type=F32))
            at_scr[:, g * SSM_GROUP:(g + 1) * SSM_GROUP, :] = y.astype(BF16).T.reshape(nk, SSM_GROUP, cb)
        for t in range(nk):
            y_ref[pl.ds(t, cb, stride=nk), :] = at_scr[t].T.astype(F32)
        return finals

    finals = mix(up_ref, yp_ref, atp_scr, cb_p, prompt_carry)
    for g in range(n_groups):
        fr_ref[g] = finals[g][0]
        fi_ref[g] = finals[g][1]
    mix(us_ref, ys_ref, ats_scr, cb_s, sample_carry)


def _ssm_core(prow, bt, ct, h0r, h0i, up, us, p_batch, s_batch):
    groups, _, w2 = prow.shape
    gpb = LANES // SSM_GROUP
    np_rows, ns_rows = up.shape[0], us.shape[0]
    cb_p, cb_s = np_rows // SSM_CHUNK, ns_rows // SSM_CHUNK
    cb_max = max(cb_p, cb_s)
    width = SSM_CHUNK * SSM_GROUP
    shf, shb = _toeplitz_shifts()

    def gspec(r, c):
        return pl.BlockSpec((gpb, r, c), lambda o: (o, 0, 0))

    def lane_spec(n):
        return pl.BlockSpec((n, LANES), lambda o: (0, o))

    return pl.pallas_call(
        functools.partial(_ssm_core_kernel, p_batch=p_batch, s_batch=s_batch),
        grid=(groups // gpb,),
        in_specs=[gspec(SUBLANES, w2), gspec(2 * SSM_GROUP, w2), gspec(2 * SSM_GROUP, w2),
                  gspec(SUBLANES, w2), gspec(SUBLANES, w2), _resident(shf.shape), _resident(shb.shape),
                  lane_spec(np_rows), lane_spec(ns_rows)],
        out_specs=[lane_spec(np_rows), lane_spec(ns_rows), gspec(p_batch, w2), gspec(p_batch, w2)],
        out_shape=[jax.ShapeDtypeStruct(up.shape, F32), jax.ShapeDtypeStruct(us.shape, F32),
                   jax.ShapeDtypeStruct((groups, p_batch, w2), F32),
                   jax.ShapeDtypeStruct((groups, p_batch, w2), F32)],
        scratch_shapes=[pltpu.VMEM((SSM_CHUNK, LANES, cb_p), BF16), pltpu.VMEM((SSM_CHUNK, LANES, cb_s), BF16)]
        + [pltpu.VMEM((gpb, width, width), BF16)] * 3 + [pltpu.VMEM((gpb, cb_max, width), BF16)]
        + [pltpu.VMEM((gpb, cb_max, 2 * w2), F32)] * 2 + [pltpu.VMEM((gpb, cb_s, 2 * w2), F32)],
        compiler_params=_cparams(), name="ssm_core",
    )(prow, bt, ct, h0r, h0i, shf, shb, up, us)


def _ssm_layer(xp, xs, stp, sts, mods, layer, g_mix, w_in, lam_re, lam_im, log_dt, b_re, b_im, c_re, c_im, d_skip,
               w_out, h0_re, h0_im, ffn, pos=None, g_final=None, casts=()):
    width = w_in.shape[1]
    groups = width // SSM_GROUP
    n_state = lam_re.shape[-1]
    up = _ssm_in(xp, stp, mods, layer, g_mix, w_in)
    us = _ssm_in(xs, sts, mods, layer, g_mix, w_in, pos)

    def lanes_dir_state(v):
        return v.transpose(1, 0, 2).reshape(groups, 2 * n_state)

    d_rows = jnp.pad(d_skip.reshape(groups, SSM_GROUP), ((0, 0), (0, 2 * n_state - SSM_GROUP)))
    prow = jnp.stack([lanes_dir_state(lam_re), lanes_dir_state(lam_im),
                      lanes_dir_state(jnp.broadcast_to(log_dt[..., None], lam_re.shape)), d_rows], axis=1)
    prow = jnp.pad(prow, ((0, 0), (0, SUBLANES - 4), (0, 0)))
    bt = jnp.concatenate([b_re.transpose(1, 3, 0, 2).reshape(groups, SSM_GROUP, 2 * n_state),
                          b_im.transpose(1, 3, 0, 2).reshape(groups, SSM_GROUP, 2 * n_state)], axis=1)
    ct = jnp.concatenate([c_re.transpose(1, 2, 0, 3).reshape(groups, SSM_GROUP, 2 * n_state),
                          c_im.transpose(1, 2, 0, 3).reshape(groups, SSM_GROUP, 2 * n_state)], axis=1)

    def h0_rows(h):
        h = h.transpose(2, 0, 1, 3).reshape(groups, sts.nb, 2 * n_state)
        return jnp.concatenate([h] * (SUBLANES // sts.nb), axis=1)

    yp, ys, f_re, f_im = _ssm_core(prow, bt, ct, h0_rows(h0_re), h0_rows(h0_im), up, us, stp.nb, sts.nb)
    xp, cast = _ssm_out_ffn(xp, yp, stp, mods, layer, w_out, *ffn, g_final=g_final, casts=casts)
    xs, _ = _ssm_out_ffn(xs, ys, sts, mods, layer, w_out, *ffn, pos=pos, g_final=g_final)

    def final(f):
        return f.reshape(groups, stp.nb, 2, n_state).transpose(1, 2, 0, 3)

    return xp, xs, final(f_re), final(f_im), cast


def _seq_rows(ref, k):
    return ref[:, k].reshape(-1, ref.shape[-1])


def _gmlp_kernel(x_ref, mod_ref, g_ref, win_ref, ws_ref, bs_ref, wout_ref, o_ref, t_scr):
    n_groups, chunk, gd = bs_ref.shape
    for k in range(SEQ_PAIR):
        x = _seq_rows(x_ref, k)
        mod = mod_ref[k]
        hn = _norm_mod(x, g_ref[...], mod[0:1], mod[1:2]).astype(BF16)
        z = _gelu(jnp.dot(hn, win_ref[...], preferred_element_type=F32))
        wdt = z.shape[1] // 2
        u, v = z[:, :wdt], z[:, wdt:]
        vc = v - jnp.mean(v, axis=-1, keepdims=True)
        vn = (vc * lax.rsqrt(jnp.mean(vc * vc, axis=-1, keepdims=True) + EPS)).astype(BF16)
        for c in range(x.shape[0] // chunk):
            rows = slice(c * chunk, (c + 1) * chunk)
            for g in range(n_groups):
                cols = slice(g * gd, (g + 1) * gd)
                s = jnp.dot(ws_ref[g], vn[rows, cols], preferred_element_type=F32) + bs_ref[g]
                t_scr[k, rows, cols] = (u[rows, cols] * s).astype(BF16)
        out = jnp.dot(t_scr[k], wout_ref[...], preferred_element_type=F32)
        o_ref[:, k] = (x + mod[2:3] * out).reshape(o_ref.shape[0], o_ref.shape[2], o_ref.shape[3])


def _gmlp_layer(x, st, mods, layer, g_mix, w_in, w_s, bs, w_out):
    d = x.shape[-1]
    return pl.pallas_call(
        _gmlp_kernel,
        grid=st.seq_grid(),
        in_specs=[st.seq_spec(d), st.seq_mod_spec(layer, d), _resident((1, d)), _resident(w_in.shape),
                  _resident(w_s.shape), _resident(bs.shape), _resident(w_out.shape)],
        out_specs=st.seq_spec(d),
        out_shape=jax.ShapeDtypeStruct(x.shape, F32),
        scratch_shapes=[pltpu.VMEM((SEQ_PAIR, SEQ_CHUNKS * SSM_CHUNK, w_out.shape[0]), BF16)],
        compiler_params=_cparams(2), name="gmlp",
    )(x, mods, g_mix, w_in, w_s, bs, w_out)


def _conv_ffn_kernel(x_ref, xprev_ref, xnext_ref, mod_ref, g_ref, win_ref, cw_ref, wout_ref, gf_ref, w1_ref, w2_ref,
                     *rest, seq_blocks, th, n_cast):
    o_ref = rest[n_cast]
    j = pl.program_id(1)
    g = g_ref[...]
    cw = cw_ref[...]
    d = x_ref.shape[-1]
    xs = [_seq_rows(x_ref, k) for k in range(SEQ_PAIR)]
    mods = [mod_ref[k] for k in range(SEQ_PAIR)]
    hh = jnp.concatenate([_norm_mod(jnp.concatenate([xprev_ref[k], xnext_ref[k]], axis=0), g, m[0:1], m[1:2])
                          for k, m in enumerate(mods)], axis=0).astype(BF16)
    zh = jnp.dot(hh, win_ref[:, d:], preferred_element_type=F32)
    th_all = zh[:, :d] * zh[:, d:]
    mids = []
    for k, (x, mod) in enumerate(zip(xs, mods)):
        tm = x.shape[0]
        hn = _norm_mod(x, g, mod[0:1], mod[1:2]).astype(BF16)
        z = jnp.dot(hn, win_ref[...], preferred_element_type=F32)
        gb, t = z[:, :d], z[:, d:2 * d] * z[:, 2 * d:]
        r0 = 2 * SUBLANES * k
        t_prev = jnp.where(j > 0, th_all[r0 + SUBLANES - 1:r0 + SUBLANES], 0.0)
        t_next = jnp.where(j < seq_blocks - 1, th_all[r0 + SUBLANES:r0 + SUBLANES + 1], 0.0)
        row = lax.broadcasted_iota(jnp.int32, (tm, 1), 0)
        up = jnp.where(row == 0, t_prev, pltpu.roll(t, 1, 0))
        dn = jnp.where(row == tm - 1, t_next, pltpu.roll(t, tm - 1, 0))
        y = cw[0:1] * up + cw[1:2] * t + cw[2:3] * dn
        out = jnp.dot((gb * y).astype(BF16), wout_ref[...], preferred_element_type=F32)
        mids.append(x + mod[2:3] * out)
    outs = _ffn_blocks(mids, mods, gf_ref[...], w1_ref, w2_ref, th)
    for k, out in enumerate(outs):
        o_ref[:, k] = out.reshape(o_ref.shape[0], o_ref.shape[2], o_ref.shape[3])
    _cast_refs(rest[:n_cast], rest[n_cast + 1:])


def _conv_ffn_layer(x, st, mods, layer, g_mix, w_in, cw, w_out, gf, w1, w2, casts=(), th=1024):
    d = x.shape[-1]
    prev, nxt = st.halo_specs(d)
    xh = st.halo_view(x)
    c_in, c_out, c_shapes = st.cast_plan(casts, seq=True)
    out, *cast = pl.pallas_call(
        functools.partial(_conv_ffn_kernel, seq_blocks=st.seq_blocks, th=th, n_cast=len(casts)),
        grid=st.seq_grid(),
        in_specs=[st.seq_spec(d), prev, nxt, st.seq_mod_spec(layer, d), _resident((1, d)), _resident(w_in.shape),
                  _resident(cw.shape), _resident(w_out.shape), _resident((1, d)), _resident(w1.shape),
                  _resident(w2.shape)] + c_in,
        out_specs=[st.seq_spec(d)] + c_out,
        out_shape=[jax.ShapeDtypeStruct(x.shape, F32)] + c_shapes,
        compiler_params=_cparams(2), name="conv_ffn",
    )(x, xh, xh, mods, g_mix, w_in, cw, w_out, gf, w1, w2, *[stack for stack, _ in casts])
    return out, cast


def kernel(x_prompt, x_sample, state_ssm_re, state_ssm_im, c, c_ctx, w_mod, b_mod, g_mix, g_ffn, ffn_w1, ffn_w2, ssm_w_in, ssm_lam_re, ssm_lam_im, ssm_log_dt, ssm_b_re, ssm_b_im, ssm_c_re, ssm_c_im, ssm_d, ssm_w_out, gmlp_w_in, gmlp_w_s, gmlp_b_s, gmlp_w_out, conv_w_in, conv_w, conv_w_out, g_final):
    n_prompt, len_prompt, d = x_prompt.shape
    n_sample, len_sample, _ = x_sample.shape
    depth = w_mod.shape[0]
    n_mixers = 3
    assert 1 + n_sample <= SUBLANES
    stp = _Stream(len_prompt // SSM_CHUNK, n_prompt, per_batch_cond=False)
    sts = _Stream(len_sample // SSM_CHUNK, n_sample, per_batch_cond=True)

    xp, xs = x_prompt, x_sample
    pos = _grid_pos_embed(len_sample, d)

    cond = jnp.concatenate([c_ctx[None, :], c, jnp.zeros((SUBLANES - 1 - n_sample, d), F32)], axis=0)
    mods = _adaln(cond.T, w_mod, b_mod, 1 + n_sample)
    table = [0] * SEQ_PAIR + list(range(1, 1 + n_sample))
    mods = mods[:, jnp.array(table)].reshape(depth, len(table), N_MOD, d)

    assert (depth - 1) % n_mixers == 0, "the last layer's kernel writes the natural-order outputs"

    def layer_weights(i):
        kind, j = i % n_mixers, i // n_mixers
        mixer = {0: [(ssm_w_in, j), (ssm_w_out, j)],
                 1: [(gmlp_w_in, j), (gmlp_w_s.reshape(gmlp_w_s.shape[0], -1, gmlp_w_s.shape[-1]), j), (gmlp_w_out, j)],
                 2: [(conv_w_in, j), (conv_w_out, j)]}[kind]
        return mixer + [(ffn_w1, i), (ffn_w2, i)]

    wb = [stack[layer].astype(BF16) for stack, layer in layer_weights(0)]
    new_re, new_im = [], []
    for i in range(depth):
        kind, j = i % n_mixers, i // n_mixers
        gm = g_mix[i].reshape(1, d)
        nxt = layer_weights(i + 1) if i + 1 < depth else []
        *wm, w1, w2 = wb
        ffn = (g_ffn[i].reshape(1, d), w1, w2)
        if kind == 0:
            xp, xs, f_re, f_im, wb = _ssm_layer(
                xp, xs, stp, sts, mods, i, gm, wm[0], ssm_lam_re[j], ssm_lam_im[j], ssm_log_dt[j],
                ssm_b_re[j], ssm_b_im[j], ssm_c_re[j], ssm_c_im[j], ssm_d[j], wm[1],
                state_ssm_re[:, j], state_ssm_im[:, j], ffn, pos if i == 0 else None,
                g_final.reshape(1, d) if i == depth - 1 else None, nxt)
            new_re.append(f_re)
            new_im.append(f_im)
        elif kind == 1:
            n_groups, chunk, _ = gmlp_w_s[j].shape
            gd = gmlp_w_out[j].shape[0] // n_groups
            bs = jnp.broadcast_to(gmlp_b_s[j][:, :, None], (n_groups, chunk, gd))
            ws = (wm[0], wm[1].reshape(gmlp_w_s[j].shape), bs, wm[2])
            xp, wb = _ffn(_gmlp_layer(xp, stp, mods, i, gm, *ws), stp, mods, i, *ffn, casts=nxt)
            xs, _ = _ffn(_gmlp_layer(xs, sts, mods, i, gm, *ws), sts, mods, i, *ffn)
        else:
            ws = (wm[0], jnp.pad(conv_w[j], ((0, SUBLANES - CONV_WIDTH), (0, 0))), wm[1])
            xp, wb = _conv_ffn_layer(xp, stp, mods, i, gm, *ws, *ffn, casts=nxt)
            xs, _ = _conv_ffn_layer(xs, sts, mods, i, gm, *ws, *ffn)

    return (xp.reshape(x_prompt.shape), xs.reshape(x_sample.shape),
            jnp.stack(new_re, axis=1), jnp.stack(new_im, axis=1))
```

```python
import functools
import math

import jax
import jax.numpy as jnp
from jax import lax
from jax.experimental import pallas as pl
from jax.experimental.pallas import tpu as pltpu

EPS = 1e-6
N_MOD = 6
GRID_W = 64
SSM_GROUP = 16
SSM_CHUNK = 16
CONV_WIDTH = 3
SUBLANES = 8
LANES = 128
ROW_BLOCK = 512
FFN_ROW_BLOCK = 1024
SSM_IN_ROW_BLOCK = 1024
SEQ_CHUNKS = 16
SEQ_PAIR = 2
VMEM_LIMIT = 56 * 1024 * 1024

F32 = jnp.float32
BF16 = jnp.bfloat16
NT = (((1,), (1,)), ((), ()))


def _cparams(n_axes=1, vmem=VMEM_LIMIT):
    return pltpu.CompilerParams(dimension_semantics=("arbitrary",) * n_axes, vmem_limit_bytes=vmem)


def _gelu(x):
    return 0.5 * x * (1.0 + jnp.tanh(math.sqrt(2.0 / math.pi) * (x + 0.044715 * (x * x * x))))


def _norm_mod(x, g, shift, scale):
    y = x * lax.rsqrt(jnp.mean(x * x, axis=-1, keepdims=True) + EPS)
    return (y * g) * (1.0 + scale) + shift


def _cmul(ar, ai, br, bi):
    return ar * br - ai * bi, ar * bi + ai * br


def _resident(shape):
    nd = len(shape)
    return pl.BlockSpec(shape, lambda *_: (0,) * nd, pipeline_mode=pl.Buffered(1))


def _side_plan(jobs, n_steps, step_of):
    kinds, ins, args, outs, shapes = [], [], [], [], []
    for job in jobs:
        if job[0] == "cast":
            _, stack, layer = job
            _, r, c = stack.shape
            blk = r // n_steps
            assert r % n_steps == 0 and blk % (2 * SUBLANES) == 0
            kinds.append(("cast", 1, None))
            ins.append(pl.BlockSpec((None, blk, c), lambda *ids, layer=layer: (layer, step_of(*ids), 0)))
            args.append(stack)
            outs.append(pl.BlockSpec((blk, c), lambda *ids: (step_of(*ids), 0)))
            shapes.append(jax.ShapeDtypeStruct((r, c), BF16))
        else:
            _, cond_t, w_mod, b_mod, layer, n_cond = job
            _, d, n = w_mod.shape
            tn = n // n_steps
            assert n % n_steps == 0 and tn % LANES == 0
            kinds.append(("adaln", 3, n_cond))
            ins += [_resident(cond_t.shape),
                    pl.BlockSpec((None, d, tn), lambda *ids, layer=layer: (layer, 0, step_of(*ids))),
                    pl.BlockSpec((None, 1, tn), lambda *ids, layer=layer: (layer, 0, step_of(*ids)))]
            args += [cond_t, w_mod, b_mod.reshape(b_mod.shape[0], 1, n)]
            outs.append(pl.BlockSpec((SUBLANES, tn), lambda *ids: (0, step_of(*ids))))
            shapes.append(jax.ShapeDtypeStruct((SUBLANES, n), F32))
    return tuple(kinds), ins, args, outs, shapes


def _side_counts(kinds):
    return sum(k[1] for k in kinds), len(kinds)


def _side_run(kinds, in_refs, out_refs):
    i = 0
    for (name, n_in, param), dst in zip(kinds, out_refs):
        if name == "cast":
            dst[...] = in_refs[i][...].astype(BF16)
        else:
            dst[...] = _adaln_rows(in_refs[i][...], in_refs[i + 1][...], param) + in_refs[i + 2][...]
        i += n_in


def _adaln_rows(ct, w, n_cond):
    s = ct * jax.nn.sigmoid(ct)
    rows = [jnp.sum(s[:, r:r + 1] * w, axis=0, keepdims=True) for r in range(n_cond)]
    rows += [jnp.zeros_like(rows[0])] * (SUBLANES - n_cond)
    return jnp.concatenate(rows, axis=0)


class _Stream:
    def __init__(self, n_chunks, n_batch, per_batch_cond, row_block=ROW_BLOCK):
        self.nc, self.nb = n_chunks, n_batch
        self.n_rows = n_chunks * n_batch * SSM_CHUNK
        self.per_batch_cond = per_batch_cond
        self.row_block = row_block
        self.row_chunks = row_block // SSM_CHUNK
        self.seq_blocks = n_chunks // SEQ_CHUNKS
        assert n_chunks % SEQ_CHUNKS == 0 and self.n_rows % row_block == 0 and n_batch % SEQ_PAIR == 0
        assert not per_batch_cond or n_chunks % self.row_chunks == 0

    def resized(self, row_block):
        return _Stream(self.nc, self.nb, self.per_batch_cond, row_block)

    def row_grid(self):
        return (self.nb, self.nc // self.row_chunks) if self.per_batch_cond else (self.n_rows // self.row_block,)

    def row_view(self, x):
        return x if self.per_batch_cond else x.reshape(self.n_rows, x.shape[-1])

    def row_spec(self, width):
        if self.per_batch_cond:
            return pl.BlockSpec((self.row_chunks, None, SSM_CHUNK, width), lambda b, j: (j, b, 0, 0))
        return pl.BlockSpec((self.row_block, width), lambda i: (i, 0))

    def row_shape(self, width):
        return (self.nc, self.nb, SSM_CHUNK, width) if self.per_batch_cond else (self.n_rows, width)

    def unview(self, x):
        return x.reshape(self.nc, self.nb, SSM_CHUNK, x.shape[-1])

    def nat_view(self, x):
        return x.reshape(self.nb, self.nc, SSM_CHUNK, x.shape[-1])

    def nat_spec(self, width):
        if self.per_batch_cond:
            return pl.BlockSpec((None, self.row_chunks, SSM_CHUNK, width), lambda b, j: (b, j, 0, 0))
        assert self.nb * SSM_CHUNK == self.row_block
        return pl.BlockSpec((self.nb, None, SSM_CHUNK, width), lambda i: (0, i, 0, 0))

    def pos_spec(self, width):
        return pl.BlockSpec((self.row_chunks, SSM_CHUNK, width), lambda b, j: (j, 0, 0))

    def seq_grid(self):
        return (self.nb // SEQ_PAIR, self.seq_blocks)

    def seq_spec(self, width):
        return pl.BlockSpec((SEQ_CHUNKS, SEQ_PAIR, SSM_CHUNK, width), lambda b, j: (j, b, 0, 0))

    def halo_specs(self, width):
        halves = SSM_CHUNK // SUBLANES
        shape = (None, SEQ_PAIR, None, SUBLANES, width)
        prev = pl.BlockSpec(shape, lambda b, j: (jnp.maximum(j * SEQ_CHUNKS - 1, 0), b, halves - 1, 0, 0))
        nxt = pl.BlockSpec(shape, lambda b, j: (jnp.minimum((j + 1) * SEQ_CHUNKS, self.nc - 1), b, 0, 0, 0))
        return prev, nxt

    def halo_view(self, x):
        return x.reshape(self.nc, self.nb, SSM_CHUNK // SUBLANES, SUBLANES, x.shape[-1])

    def mod_spec(self, layer, d):
        if self.per_batch_cond:
            return pl.BlockSpec((None, None, N_MOD, d), lambda b, j: (layer, SEQ_PAIR + b, 0, 0))
        return pl.BlockSpec((None, None, N_MOD, d), lambda *_: (layer, 0, 0, 0))

    def seq_mod_spec(self, layer, d):
        if self.per_batch_cond:
            return pl.BlockSpec((None, SEQ_PAIR, N_MOD, d), lambda b, j: (layer, 1 + b, 0, 0))
        return pl.BlockSpec((None, SEQ_PAIR, N_MOD, d), lambda *_: (layer, 0, 0, 0))

    def n_axes(self, seq=False):
        return 2 if (seq or self.per_batch_cond) else 1

    def side_plan(self, jobs, seq=False):
        if seq:
            return _side_plan(jobs, (self.nb // SEQ_PAIR) * self.seq_blocks, lambda b, j: b * self.seq_blocks + j)
        assert not jobs or not self.per_batch_cond
        return _side_plan(jobs, self.n_rows // self.row_block, lambda i: i)


def _grid_pos_embed(n_tokens, d):
    rows = n_tokens // GRID_W
    quarter = d // 4
    freq = 1.0 / (10000.0 ** (jnp.arange(quarter, dtype=F32) / quarter))
    ar = jnp.arange(rows, dtype=F32)[:, None] * freq
    ac = jnp.arange(GRID_W, dtype=F32)[:, None] * freq
    row_part = jnp.concatenate([jnp.sin(ar), jnp.cos(ar)], axis=-1)
    col_part = jnp.concatenate([jnp.sin(ac), jnp.cos(ac)], axis=-1)
    table = jnp.concatenate([jnp.broadcast_to(row_part[:, None], (rows, GRID_W, d // 2)),
                             jnp.broadcast_to(col_part[None], (rows, GRID_W, d // 2))], axis=-1)
    return table.reshape(rows * GRID_W, d)


def _adaln_kernel(ct_ref, w_ref, b_ref, o_ref, *, n_cond):
    o_ref[...] = _adaln_rows(ct_ref[...], w_ref[...], n_cond) + b_ref[...]


def _adaln(cond_t, w_mod, b_mod, n_cond, layers, tn=1024):
    depth, d, n = w_mod.shape
    l0 = layers[0]
    assert list(layers) == list(range(l0, l0 + len(layers)))
    return pl.pallas_call(
        functools.partial(_adaln_kernel, n_cond=n_cond),
        grid=(len(layers), n // tn),
        in_specs=[
            pl.BlockSpec((d, SUBLANES), lambda l, j: (0, 0)),
            pl.BlockSpec((None, d, tn), lambda l, j: (l + l0, 0, j)),
            pl.BlockSpec((None, 1, tn), lambda l, j: (l + l0, 0, j)),
        ],
        out_specs=pl.BlockSpec((None, SUBLANES, tn), lambda l, j: (l, 0, j)),
        out_shape=jax.ShapeDtypeStruct((len(layers), SUBLANES, n), F32),
        compiler_params=_cparams(2), name="adaln",
    )(cond_t, w_mod, b_mod.reshape(depth, 1, n))


def _rows(ref):
    return ref[...].reshape(-1, ref.shape[-1])


def _rms(x, g):
    return (x * lax.rsqrt(jnp.mean(x * x, axis=-1, keepdims=True) + EPS)) * g


def _ffn_blocks(xs, mods, g, w1_ref, w2_ref, th, g_final=None):
    hn = jnp.concatenate([_norm_mod(x, g, m[3:4], m[4:5]).astype(BF16) for x, m in zip(xs, mods)], axis=0)
    acc = jnp.zeros((hn.shape[0], w2_ref.shape[1]), F32)
    for c in range(w1_ref.shape[1] // th):
        h1 = jnp.dot(hn, w1_ref[:, c * th:(c + 1) * th], preferred_element_type=F32)
        h1 = jnp.square(jnp.maximum(h1, 0.0)).astype(BF16)
        acc = acc + jnp.dot(h1, w2_ref[c * th:(c + 1) * th, :], preferred_element_type=F32)
    outs, r0 = [], 0
    for x, m in zip(xs, mods):
        out = x + m[5:6] * acc[r0:r0 + x.shape[0]]
        outs.append(out if g_final is None else _rms(out, g_final))
        r0 += x.shape[0]
    return outs


def _ffn_kernel(x_ref, mod_ref, g_ref, w1_ref, w2_ref, *rest, th, side):
    n_in, _ = _side_counts(side)
    o_ref = rest[n_in]
    out, = _ffn_blocks([_rows(x_ref)], [mod_ref[...]], g_ref[...], w1_ref, w2_ref, th)
    o_ref[...] = out.reshape(o_ref.shape)
    _side_run(side, rest[:n_in], rest[n_in + 1:])


def _ffn(x, st, mods, layer, g, w1, w2, jobs=(), th=1024):
    d = x.shape[-1]
    side, s_in, s_args, s_out, s_shapes = st.side_plan(jobs)
    out, *extra = pl.pallas_call(
        functools.partial(_ffn_kernel, th=th, side=side),
        grid=st.row_grid(),
        in_specs=[st.row_spec(d), st.mod_spec(layer, d), _resident((1, d)), _resident(w1.shape), _resident(w2.shape)]
        + s_in,
        out_specs=[st.row_spec(d)] + s_out,
        out_shape=[jax.ShapeDtypeStruct(st.row_shape(d), F32)] + s_shapes,
        compiler_params=_cparams(st.n_axes()), name="ffn",
    )(st.row_view(x), mods, g, w1, w2, *s_args)
    return st.unview(out), extra


def _x_operands(x, st, pos):
    d = x.shape[-1]
    if x.ndim == 4:
        return [st.row_spec(d)], [st.row_view(x)]
    specs, args = [st.nat_spec(d)], [st.nat_view(x)]
    if pos is not None:
        specs.append(st.pos_spec(d))
        args.append(pos.reshape(st.nc, SSM_CHUNK, d))
    return specs, args


def _ssm_in_kernel(x_ref, *rest):
    *pos_ref, mod_ref, g_ref, w_ref, o_ref = rest
    x = _rows(x_ref) + _rows(pos_ref[0]) if pos_ref else _rows(x_ref)
    mod = mod_ref[...]
    hn = _norm_mod(x, g_ref[...], mod[0:1], mod[1:2]).astype(BF16)
    o_ref[...] = jnp.dot(hn, w_ref[...], preferred_element_type=F32).reshape(o_ref.shape)


def _ssm_in(x, st, mods, layer, g, w, pos=None):
    d, width = x.shape[-1], w.shape[1]
    x_specs, x_args = _x_operands(x, st, pos)
    out = pl.pallas_call(
        _ssm_in_kernel,
        grid=st.row_grid(),
        in_specs=x_specs + [st.mod_spec(layer, d), _resident((1, d)), _resident(w.shape)],
        out_specs=st.row_spec(width),
        out_shape=jax.ShapeDtypeStruct(st.row_shape(width), F32),
        compiler_params=_cparams(st.n_axes()), name="ssm_in",
    )(*x_args, mods, g, w)
    return out.reshape(st.n_rows, width)


def _ssm_out_ffn_kernel(x_ref, *rest, has_pos, has_final, th, side):
    pos_ref = rest[0] if has_pos else None
    y_ref, mod_ref, wout_ref, gf_ref, w1_ref, w2_ref = rest[has_pos:has_pos + 6]
    n_in = has_pos + 6 + has_final
    n_side, _ = _side_counts(side)
    g_final = rest[n_in - 1][...] if has_final else None
    o_ref = rest[n_in + n_side]
    x = _rows(x_ref) + _rows(pos_ref) if has_pos else _rows(x_ref)
    y = _rows(y_ref)
    mod = mod_ref[...]
    d = x.shape[-1]
    half = x.shape[0] // 2
    mids = []
    for k in range(2):
        rows = slice(k * half, (k + 1) * half)
        ag = jnp.dot(_gelu(y[rows]).astype(BF16), wout_ref[...], preferred_element_type=F32)
        mids.append(x[rows] + mod[2:3] * (ag[:, :d] * jax.nn.sigmoid(ag[:, d:])))
    outs = _ffn_blocks(mids, [mod, mod], gf_ref[...], w1_ref, w2_ref, th, g_final)
    o_ref[...] = jnp.concatenate(outs, axis=0).reshape(o_ref.shape)
    _side_run(side, rest[n_in:n_in + n_side], rest[n_in + n_side + 1:])


def _ssm_out_ffn(x, y, st, mods, layer, w_out, gf, w1, w2, pos=None, g_final=None, jobs=(), th=1024):
    d = x.shape[-1]
    x_specs, x_args = _x_operands(x, st, pos)
    last = g_final is not None
    side, s_in, s_args, s_out, s_shapes = st.side_plan(jobs)
    out, *extra = pl.pallas_call(
        functools.partial(_ssm_out_ffn_kernel, has_pos=len(x_args) == 2, has_final=last, th=th, side=side),
        grid=st.row_grid(),
        in_specs=x_specs + [st.row_spec(d), st.mod_spec(layer, d), _resident(w_out.shape), _resident((1, d)),
                            _resident(w1.shape), _resident(w2.shape)] + ([_resident((1, d))] if last else []) + s_in,
        out_specs=[st.nat_spec(d) if last else st.row_spec(d)] + s_out,
        out_shape=[jax.ShapeDtypeStruct((st.nb, st.nc, SSM_CHUNK, d) if last else st.row_shape(d), F32)] + s_shapes,
        compiler_params=_cparams(st.n_axes()), name="ssm_out_ffn",
    )(*x_args, st.row_view(st.unview(y)), mods, w_out, gf, w1, w2, *([g_final] if last else []), *s_args)
    return (out if last else st.unview(out)), extra


def _ssm_operators(prow, bt, ct):
    nk = SSM_CHUNK
    half = prow.shape[1] // 2
    lr, li = prow[0:1], prow[1:2]
    dt = jnp.exp(prow[2:3])
    mag = jnp.exp(lr * dt)
    ar, ai = mag * jnp.cos(li * dt), mag * jnp.sin(li * dt)
    den = lr * lr + li * li
    nr, ni = _cmul(ar - 1.0, ai, lr, -li)
    bbr, bbi = _cmul(nr / den, ni / den, bt[0:nk], bt[nk:2 * nk])
    ctr, cti = ct[0:nk], ct[nk:2 * nk]

    pr, pi = [jnp.ones_like(ar)], [jnp.zeros_like(ar)]
    for _ in range(nk):
        r, i = _cmul(pr[-1], pi[-1], ar, ai)
        pr.append(r)
        pi.append(i)
    is_fwd = lax.broadcasted_iota(jnp.int32, (1, 2 * half), 1) < half

    def pw(kf, kb):
        return jnp.where(is_fwd, pr[kf], pr[kb]), jnp.where(is_fwd, pi[kf], pi[kb])

    wp_r, wp_i, ca_r, ca_i, wc_r, wc_i = [], [], [], [], [], []
    for t in range(nk):
        r, i = _cmul(bbr, bbi, *pw(nk - 1 - t, t))
        wp_r.append(r)
        wp_i.append(i)
        r, i = _cmul(ctr, cti, *pw(t, nk - 1 - t))
        ca_r.append(r)
        ca_i.append(i)
        r, i = _cmul(ctr, cti, *pw(t + 1, nk - t))
        wc_r.append(r)
        wc_i.append(-i)
    cat = jnp.concatenate
    wp = cat([cat(wp_r, 0), cat(wp_i, 0)], axis=1).astype(BF16)
    ca = cat([cat(ca_r, 0), cat(ca_i, 0)], axis=1)
    wct = cat([cat(wc_r, 0), cat(wc_i, 0)], axis=1).astype(BF16)

    zero = jnp.zeros_like(bbr)
    lhs = cat([cat([jnp.where(is_fwd, bbr, zero), jnp.where(is_fwd, -bbi, zero)], 1),
               cat([jnp.where(is_fwd, zero, bbr), jnp.where(is_fwd, zero, -bbi)], 1)], 0)
    kk = lax.dot_general(lhs, ca, NT, precision=lax.Precision.HIGHEST, preferred_element_type=F32)
    width = nk * SSM_GROUP
    lane = lax.broadcasted_iota(jnp.int32, (nk, width), 1)
    row = lax.broadcasted_iota(jnp.int32, (nk, width), 0)
    d_lanes = cat([prow[3:4], jnp.zeros_like(prow[3:4])], axis=1)
    kf = kk[0:nk] + jnp.where(lane == row, d_lanes, 0.0)
    return kf, kk[nk:2 * nk], wp, wct, pr[nk], pi[nk]


def _toeplitz_shifts():
    width = SSM_CHUNK * SSM_GROUP
    src = jnp.arange(width)[:, None]
    dst = jnp.arange(width)[None, :]
    fwd = [dst == src + SSM_GROUP * s for s in range(SSM_CHUNK)]
    bwd = [src == dst + SSM_GROUP * (SSM_CHUNK - 1 - s) for s in range(SSM_CHUNK)]
    return jnp.concatenate(fwd, axis=1).astype(BF16), jnp.concatenate(bwd, axis=1).astype(BF16)


def _ssm_core_kernel(prow_ref, bt_ref, ct_ref, h0r_ref, h0i_ref, shf_ref, shb_ref, up_ref, us_ref, *rest,
                     p_batch, s_batch, side):
    n_in, n_out = _side_counts(side)
    yp_ref, ys_ref, fr_ref, fi_ref = rest[n_in:n_in + 4]
    atp_scr, ats_scr, tg_scr, wp_scr, wct_scr, u_scr, p_scr, s_scr, q_scr = rest[n_in + 4 + n_out:]
    _side_run(side, rest[:n_in], rest[n_in + 4:n_in + 4 + n_out])
    nk = SSM_CHUNK
    n_groups = LANES // SSM_GROUP
    cb_p, cb_s = atp_scr.shape[2], ats_scr.shape[2]
    w2 = prow_ref.shape[2]
    width = nk * SSM_GROUP
    decay, kfs, kbs = [], [], []
    for g in range(n_groups):
        kf, kb, wp, wct, a_r, a_i = _ssm_operators(prow_ref[g], bt_ref[g], ct_ref[g])
        kfs.append(kf)
        kbs.append(kb)
        wp_scr[g] = wp
        wct_scr[g] = wct
        decay.append((a_r, a_i))
    tz = (jnp.dot(jnp.concatenate(kfs, axis=0).astype(BF16), shf_ref[...], preferred_element_type=F32)
          + jnp.dot(jnp.concatenate(kbs, axis=0).astype(BF16), shb_ref[...], preferred_element_type=F32))
    for g in range(n_groups):
        for s in range(nk):
            tg_scr[g, s * SSM_GROUP:(s + 1) * SSM_GROUP, :] = (
                tz[g * SSM_GROUP:(g + 1) * SSM_GROUP, s * width:(s + 1) * width].astype(BF16))

    is_fwd = lax.broadcasted_iota(jnp.int32, (1, w2), 1) < (w2 // 2)

    def scan(n_tiles, rows, src_scr, carries, mults):
        for j in range(n_tiles):
            lo, hi = j * rows, (n_tiles - 1 - j) * rows
            for g in range(n_groups):
                c_r, c_i = carries[g]
                for c, l0 in ((c_r, 0), (c_i, w2)):
                    if lo < hi:
                        s_scr[g, lo:lo + rows, l0:l0 + w2] = c
                        s_scr[g, hi:hi + rows, l0:l0 + w2] = c
                    else:
                        s_scr[g, lo:lo + rows, l0:l0 + w2] = jnp.where(is_fwd, c, s_scr[g, lo:lo + rows, l0:l0 + w2])
                        s_scr[g, hi:hi + rows, l0:l0 + w2] = jnp.where(is_fwd, s_scr[g, hi:hi + rows, l0:l0 + w2], c)
                s_r = jnp.where(is_fwd, src_scr[g, lo:lo + rows, 0:w2], src_scr[g, hi:hi + rows, 0:w2])
                s_i = jnp.where(is_fwd, src_scr[g, lo:lo + rows, w2:2 * w2], src_scr[g, hi:hi + rows, w2:2 * w2])
                n_r, n_i = _cmul(*mults[g], c_r, c_i)
                carries[g] = (n_r + s_r, n_i + s_i)
        return carries

    def prompt_carry(cb):
        z = jnp.zeros((p_batch, w2), F32)
        return scan(cb // p_batch, p_batch, p_scr, [(z, z)] * n_groups, decay)

    def sample_carry(cb):
        assert 2 * s_batch == SUBLANES
        row = lax.broadcasted_iota(jnp.int32, (SUBLANES, w2), 0)
        keep = jnp.where(is_fwd, 1, 0) == jnp.where(row < s_batch, 1, 0)
        carries, mults = [], []
        for g in range(n_groups):
            a_r, a_i = decay[g]
            p_r, p_i = p_scr[g, 0:cb, 0:w2], p_scr[g, 0:cb, w2:2 * w2]
            ap_r, ap_i = _cmul(a_r, a_i, p_r, p_i)
            q_scr[g, 0:cb, 0:w2] = jnp.where(is_fwd, pltpu.roll(p_r, cb - s_batch, 0), pltpu.roll(p_r, s_batch, 0)) + ap_r
            q_scr[g, 0:cb, w2:2 * w2] = jnp.where(is_fwd, pltpu.roll(p_i, cb - s_batch, 0),
                                                  pltpu.roll(p_i, s_batch, 0)) + ap_i
            h_r, h_i = h0r_ref[g], h0i_ref[g]
            e_r = pltpu.roll(jnp.where(is_fwd, p_r[0:SUBLANES], p_r[cb - SUBLANES:cb]), s_batch, 0)
            e_i = pltpu.roll(jnp.where(is_fwd, p_i[0:SUBLANES], p_i[cb - SUBLANES:cb]), s_batch, 0)
            ah_r, ah_i = _cmul(a_r, a_i, h_r, h_i)
            carries.append((jnp.where(keep, h_r, ah_r + e_r), jnp.where(keep, h_i, ah_i + e_i)))
            mults.append(_cmul(a_r, a_i, a_r, a_i))
        return scan(cb // SUBLANES, SUBLANES, q_scr, carries, mults)

    def mix(u_ref, y_ref, at_scr, cb, carry_fn):
        for t in range(nk):
            at_scr[t] = u_ref[pl.ds(t, cb, stride=nk), :].astype(BF16).T
        for g in range(n_groups):
            a = at_scr[:, g * SSM_GROUP:(g + 1) * SSM_GROUP, :].reshape(nk * SSM_GROUP, cb)
            u = a.T
            u_scr[g, 0:cb, :] = u
            p_scr[g, 0:cb, :] = jnp.dot(u, wp_scr[g], preferred_element_type=F32)
        finals = carry_fn(cb)
        for g in range(n_groups):
            s = s_scr[g, 0:cb, :].astype(BF16)
            y = (jnp.dot(u_scr[g, 0:cb, :], tg_scr[g], preferred_element_type=F32)
                 + lax.dot_general(s, wct_scr[g], NT, preferred_element_type=F32))
            at_scr[:, g * SSM_GROUP:(g + 1) * SSM_GROUP, :] = y.astype(BF16).T.reshape(nk, SSM_GROUP, cb)
        for t in range(nk):
            y_ref[pl.ds(t, cb, stride=nk), :] = at_scr[t].T.astype(F32)
        return finals

    finals = mix(up_ref, yp_ref, atp_scr, cb_p, prompt_carry)
    for g in range(n_groups):
        fr_ref[g] = finals[g][0]
        fi_ref[g] = finals[g][1]
    mix(us_ref, ys_ref, ats_scr, cb_s, sample_carry)


def _ssm_core(prow, bt, ct, h0r, h0i, up, us, p_batch, s_batch, jobs=()):
    groups, _, w2 = prow.shape
    gpb = LANES // SSM_GROUP
    np_rows, ns_rows = up.shape[0], us.shape[0]
    cb_p, cb_s = np_rows // SSM_CHUNK, ns_rows // SSM_CHUNK
    cb_max = max(cb_p, cb_s)
    width = SSM_CHUNK * SSM_GROUP
    shf, shb = _toeplitz_shifts()

    def gspec(r, c):
        return pl.BlockSpec((gpb, r, c), lambda o: (o, 0, 0))

    def lane_spec(n):
        return pl.BlockSpec((n, LANES), lambda o: (0, o))

    side, c_in, c_args, c_out, c_shapes = _side_plan(jobs, groups // gpb, lambda o: o)
    yp, ys, f_re, f_im, *cast = pl.pallas_call(
        functools.partial(_ssm_core_kernel, p_batch=p_batch, s_batch=s_batch, side=side),
        grid=(groups // gpb,),
        in_specs=[gspec(SUBLANES, w2), gspec(2 * SSM_GROUP, w2), gspec(2 * SSM_GROUP, w2),
                  gspec(SUBLANES, w2), gspec(SUBLANES, w2), _resident(shf.shape), _resident(shb.shape),
                  lane_spec(np_rows), lane_spec(ns_rows)] + c_in,
        out_specs=[lane_spec(np_rows), lane_spec(ns_rows), gspec(p_batch, w2), gspec(p_batch, w2)] + c_out,
        out_shape=[jax.ShapeDtypeStruct(up.shape, F32), jax.ShapeDtypeStruct(us.shape, F32),
                   jax.ShapeDtypeStruct((groups, p_batch, w2), F32),
                   jax.ShapeDtypeStruct((groups, p_batch, w2), F32)] + c_shapes,
        scratch_shapes=[pltpu.VMEM((SSM_CHUNK, LANES, cb_p), BF16), pltpu.VMEM((SSM_CHUNK, LANES, cb_s), BF16)]
        + [pltpu.VMEM((gpb, width, width), BF16)] * 3 + [pltpu.VMEM((gpb, cb_max, width), BF16)]
        + [pltpu.VMEM((gpb, cb_max, 2 * w2), F32)] * 2 + [pltpu.VMEM((gpb, cb_s, 2 * w2), F32)],
        compiler_params=_cparams(), name="ssm_core",
    )(prow, bt, ct, h0r, h0i, shf, shb, up, us, *c_args)
    return yp, ys, f_re, f_im, cast


def _ssm_layer(xp, xs, stp, sts, mods, layer, g_mix, w_in, lam_re, lam_im, log_dt, b_re, b_im, c_re, c_im, d_skip,
               w_out, h0_re, h0_im, ffn, pos=None, g_final=None, jobs=(), late=()):
    width = w_in.shape[1]
    groups = width // SSM_GROUP
    n_state = lam_re.shape[-1]
    stp_in = stp.resized(SSM_IN_ROW_BLOCK) if xp.ndim == 4 else stp
    up = _ssm_in(xp, stp_in, mods, layer, g_mix, w_in)
    us = _ssm_in(xs, sts.resized(SSM_IN_ROW_BLOCK), mods, layer, g_mix, w_in, pos)

    def lanes_dir_state(v):
        return v.transpose(1, 0, 2).reshape(groups, 2 * n_state)

    d_rows = jnp.pad(d_skip.reshape(groups, SSM_GROUP), ((0, 0), (0, 2 * n_state - SSM_GROUP)))
    prow = jnp.stack([lanes_dir_state(lam_re), lanes_dir_state(lam_im),
                      lanes_dir_state(jnp.broadcast_to(log_dt[..., None], lam_re.shape)), d_rows], axis=1)
    prow = jnp.pad(prow, ((0, 0), (0, SUBLANES - 4), (0, 0)))
    bt = jnp.concatenate([b_re.transpose(1, 3, 0, 2).reshape(groups, SSM_GROUP, 2 * n_state),
                          b_im.transpose(1, 3, 0, 2).reshape(groups, SSM_GROUP, 2 * n_state)], axis=1)
    ct = jnp.concatenate([c_re.transpose(1, 2, 0, 3).reshape(groups, SSM_GROUP, 2 * n_state),
                          c_im.transpose(1, 2, 0, 3).reshape(groups, SSM_GROUP, 2 * n_state)], axis=1)

    def h0_rows(h):
        h = h.transpose(2, 0, 1, 3).reshape(groups, sts.nb, 2 * n_state)
        return jnp.concatenate([h] * (SUBLANES // sts.nb), axis=1)

    yp, ys, f_re, f_im, conv = _ssm_core(prow, bt, ct, h0_rows(h0_re), h0_rows(h0_im), up, us, stp.nb, sts.nb, late)
    if late:
        w_out, ffn = conv[0], (ffn[0], conv[1], conv[2])
    xp, cast = _ssm_out_ffn(xp, yp, stp, mods, layer, w_out, *ffn, g_final=g_final, jobs=jobs)
    xs, _ = _ssm_out_ffn(xs, ys, sts, mods, layer, w_out, *ffn, pos=pos, g_final=g_final)

    def final(f):
        return f.reshape(groups, stp.nb, 2, n_state).transpose(1, 2, 0, 3)

    return xp, xs, final(f_re), final(f_im), cast


def _seq_rows(ref, k):
    return ref[:, k].reshape(-1, ref.shape[-1])


def _gmlp_kernel(x_ref, mod_ref, g_ref, win_ref, ws_ref, bs_ref, wout_ref, *rest, side):
    n_in, n_out = _side_counts(side)
    o_ref, t_scr = rest[n_in], rest[n_in + 1 + n_out]
    _side_run(side, rest[:n_in], rest[n_in + 1:n_in + 1 + n_out])
    n_groups, chunk, gd = bs_ref.shape
    xs, us, vns = [], [], []
    for k in range(SEQ_PAIR):
        x = _seq_rows(x_ref, k)
        mod = mod_ref[k]
        hn = _norm_mod(x, g_ref[...], mod[0:1], mod[1:2]).astype(BF16)
        z = _gelu(jnp.dot(hn, win_ref[...], preferred_element_type=F32))
        wdt = z.shape[1] // 2
        u, v = z[:, :wdt], z[:, wdt:]
        vc = v - jnp.mean(v, axis=-1, keepdims=True)
        xs.append(x)
        us.append(u)
        vns.append((vc * lax.rsqrt(jnp.mean(vc * vc, axis=-1, keepdims=True) + EPS)).astype(BF16))
    n_chunks = xs[0].shape[0] // chunk
    for g in range(n_groups):
        cols = slice(g * gd, (g + 1) * gd)
        rhs = jnp.concatenate([vn[c * chunk:(c + 1) * chunk, cols] for vn in vns for c in range(n_chunks)], axis=1)
        s = jnp.dot(ws_ref[g], rhs, preferred_element_type=F32)
        for k in range(SEQ_PAIR):
            for c in range(n_chunks):
                rows = slice(c * chunk, (c + 1) * chunk)
                i = k * n_chunks + c
                t_scr[k, rows, cols] = (us[k][rows, cols] * (s[:, i * gd:(i + 1) * gd] + bs_ref[g])).astype(BF16)
    for k in range(SEQ_PAIR):
        out = jnp.dot(t_scr[k], wout_ref[...], preferred_element_type=F32)
        o_ref[:, k] = (xs[k] + mod_ref[k][2:3] * out).reshape(o_ref.shape[0], o_ref.shape[2], o_ref.shape[3])


def _gmlp_layer(x, st, mods, layer, g_mix, w_in, w_s, bs, w_out, jobs=()):
    d = x.shape[-1]
    side, s_in, s_args, s_out, s_shapes = st.side_plan(jobs, seq=True)
    out, *extra = pl.pallas_call(
        functools.partial(_gmlp_kernel, side=side),
        grid=st.seq_grid(),
        in_specs=[st.seq_spec(d), st.seq_mod_spec(layer, d), _resident((1, d)), _resident(w_in.shape),
                  _resident(w_s.shape), _resident(bs.shape), _resident(w_out.shape)] + s_in,
        out_specs=[st.seq_spec(d)] + s_out,
        out_shape=[jax.ShapeDtypeStruct(x.shape, F32)] + s_shapes,
        scratch_shapes=[pltpu.VMEM((SEQ_PAIR, SEQ_CHUNKS * SSM_CHUNK, w_out.shape[0]), BF16)],
        compiler_params=_cparams(2), name="gmlp",
    )(x, mods, g_mix, w_in, w_s, bs, w_out, *s_args)
    return out, extra


def _conv_ffn_kernel(x_ref, xprev_ref, xnext_ref, mod_ref, g_ref, win_ref, cw_ref, wout_ref, gf_ref, w1_ref, w2_ref,
                     *rest, seq_blocks, th, side):
    n_in, _ = _side_counts(side)
    o_ref = rest[n_in]
    j = pl.program_id(1)
    g = g_ref[...]
    cw = cw_ref[...]
    d = x_ref.shape[-1]
    xs = [_seq_rows(x_ref, k) for k in range(SEQ_PAIR)]
    mods = [mod_ref[k] for k in range(SEQ_PAIR)]
    hh = jnp.concatenate([_norm_mod(jnp.concatenate([xprev_ref[k], xnext_ref[k]], axis=0), g, m[0:1], m[1:2])
                          for k, m in enumerate(mods)], axis=0).astype(BF16)
    zh = jnp.dot(hh, win_ref[:, d:], preferred_element_type=F32)
    th_all = zh[:, :d] * zh[:, d:]
    mids = []
    for k, (x, mod) in enumerate(zip(xs, mods)):
        tm = x.shape[0]
        hn = _norm_mod(x, g, mod[0:1], mod[1:2]).astype(BF16)
        z = jnp.dot(hn, win_ref[...], preferred_element_type=F32)
        gb, t = z[:, :d], z[:, d:2 * d] * z[:, 2 * d:]
        r0 = 2 * SUBLANES * k
        t_prev = jnp.where(j > 0, th_all[r0 + SUBLANES - 1:r0 + SUBLANES], 0.0)
        t_next = jnp.where(j < seq_blocks - 1, th_all[r0 + SUBLANES:r0 + SUBLANES + 1], 0.0)
        row = lax.broadcasted_iota(jnp.int32, (tm, 1), 0)
        up = jnp.where(row == 0, t_prev, pltpu.roll(t, 1, 0))
        dn = jnp.where(row == tm - 1, t_next, pltpu.roll(t, tm - 1, 0))
        y = cw[0:1] * up + cw[1:2] * t + cw[2:3] * dn
        out = jnp.dot((gb * y).astype(BF16), wout_ref[...], preferred_element_type=F32)
        mids.append(x + mod[2:3] * out)
    outs = _ffn_blocks(mids, mods, gf_ref[...], w1_ref, w2_ref, th)
    for k, out in enumerate(outs):
        o_ref[:, k] = out.reshape(o_ref.shape[0], o_ref.shape[2], o_ref.shape[3])
    _side_run(side, rest[:n_in], rest[n_in + 1:])


def _conv_ffn_layer(x, st, mods, layer, g_mix, w_in, cw, w_out, gf, w1, w2, jobs=(), th=1024):
    d = x.shape[-1]
    prev, nxt = st.halo_specs(d)
    xh = st.halo_view(x)
    side, c_in, c_args, c_out, c_shapes = st.side_plan(jobs, seq=True)
    out, *cast = pl.pallas_call(
        functools.partial(_conv_ffn_kernel, seq_blocks=st.seq_blocks, th=th, side=side),
        grid=st.seq_grid(),
        in_specs=[st.seq_spec(d), prev, nxt, st.seq_mod_spec(layer, d), _resident((1, d)), _resident(w_in.shape),
                  _resident(cw.shape), _resident(w_out.shape), _resident((1, d)), _resident(w1.shape),
                  _resident(w2.shape)] + c_in,
        out_specs=[st.seq_spec(d)] + c_out,
        out_shape=[jax.ShapeDtypeStruct(x.shape, F32)] + c_shapes,
        compiler_params=_cparams(2), name="conv_ffn",
    )(x, xh, xh, mods, g_mix, w_in, cw, w_out, gf, w1, w2, *c_args)
    return out, cast


def kernel(x_prompt, x_sample, state_ssm_re, state_ssm_im, c, c_ctx, w_mod, b_mod, g_mix, g_ffn, ffn_w1, ffn_w2, ssm_w_in, ssm_lam_re, ssm_lam_im, ssm_log_dt, ssm_b_re, ssm_b_im, ssm_c_re, ssm_c_im, ssm_d, ssm_w_out, gmlp_w_in, gmlp_w_s, gmlp_b_s, gmlp_w_out, conv_w_in, conv_w, conv_w_out, g_final):
    n_prompt, len_prompt, d = x_prompt.shape
    n_sample, len_sample, _ = x_sample.shape
    depth = w_mod.shape[0]
    n_mixers = 3
    assert 1 + n_sample <= SUBLANES
    stp = _Stream(len_prompt // SSM_CHUNK, n_prompt, per_batch_cond=False)
    sts = _Stream(len_sample // SSM_CHUNK, n_sample, per_batch_cond=True)
    sts_ffn = _Stream(len_sample // SSM_CHUNK, n_sample, per_batch_cond=True, row_block=FFN_ROW_BLOCK)

    xp, xs = x_prompt, x_sample
    pos = _grid_pos_embed(len_sample, d)

    cond_t = jnp.concatenate([c_ctx[None, :], c, jnp.zeros((SUBLANES - 1 - n_sample, d), F32)], axis=0).T
    n_cond = 1 + n_sample
    table = jnp.array([0] * SEQ_PAIR + list(range(1, n_cond)))

    def mod_table(m):
        return m[table].reshape(1, table.shape[0], N_MOD, d)

    assert (depth - 1) % n_mixers == 0, "the last layer's kernel writes the natural-order outputs"

    def layer_weights(i):
        kind, j = i % n_mixers, i // n_mixers
        mixer = {0: [(ssm_w_in, j), (ssm_w_out, j)],
                 1: [(gmlp_w_in, j), (gmlp_w_s.reshape(gmlp_w_s.shape[0], -1, gmlp_w_s.shape[-1]), j), (gmlp_w_out, j)],
                 2: [(conv_w_in, j), (conv_w_out, j)]}[kind]
        return [("cast", stack, layer) for stack, layer in mixer + [(ffn_w1, i), (ffn_w2, i)]]

    def adaln_jobs(layers):
        return [("adaln", cond_t, w_mod, b_mod, l, n_cond) for l in layers]

    first = layer_weights(0)
    wb = [first[0][1][first[0][2]].astype(BF16)] + [None] * (len(first) - 1)
    mods = {0: mod_table(_adaln(cond_t, w_mod, b_mod, n_cond, [0])[0])}
    new_re, new_im = [], []
    for i in range(depth):
        kind, j = i % n_mixers, i // n_mixers
        gm = g_mix[i].reshape(1, d)
        nxt = layer_weights(i + 1) if i + 1 < depth else []
        *wm, w1, w2 = wb
        ffn = (g_ffn[i].reshape(1, d), w1, w2)
        m = mods[i]
        if kind == 0:
            later = adaln_jobs([i + 1]) if i + 1 < depth else []
            xp, xs, f_re, f_im, extra = _ssm_layer(
                xp, xs, stp, sts, m, 0, gm, wm[0], ssm_lam_re[j], ssm_lam_im[j], ssm_log_dt[j],
                ssm_b_re[j], ssm_b_im[j], ssm_c_re[j], ssm_c_im[j], ssm_d[j], wm[1],
                state_ssm_re[:, j], state_ssm_im[:, j], ffn, pos if i == 0 else None,
                g_final.reshape(1, d) if i == depth - 1 else None, nxt + later, first[1:] if i == 0 else ())
            wb = extra[:len(nxt)]
            for job, out in zip(later, extra[len(nxt):]):
                mods[job[4]] = mod_table(out)
            new_re.append(f_re)
            new_im.append(f_im)
        elif kind == 1:
            n_groups, chunk, _ = gmlp_w_s[j].shape
            gd = gmlp_w_out[j].shape[0] // n_groups
            bs = jnp.broadcast_to(gmlp_b_s[j][:, :, None], (n_groups, chunk, gd))
            ws = (wm[0], wm[1].reshape(gmlp_w_s[j].shape), bs, wm[2])
            later = adaln_jobs([i + 1]) if i + 1 < depth else []
            xp, extra = _ffn(_gmlp_layer(xp, stp, m, 0, gm, *ws)[0], stp, m, 0, *ffn, jobs=nxt + later)
            wb = extra[:len(nxt)]
            for job, out in zip(later, extra[len(nxt):]):
                mods[job[4]] = mod_table(out)
            xs, _ = _ffn(_gmlp_layer(xs, sts, m, 0, gm, *ws)[0], sts_ffn, m, 0, *ffn)
        else:
            ws = (wm[0], jnp.pad(conv_w[j], ((0, SUBLANES - CONV_WIDTH), (0, 0))), wm[1])
            later = adaln_jobs([i + 1]) if i + 1 < depth else []
            xp, extra = _conv_ffn_layer(xp, stp, m, 0, gm, *ws, *ffn, jobs=nxt + later)
            wb = extra[:len(nxt)]
            for job, out in zip(later, extra[len(nxt):]):
                mods[job[4]] = mod_table(out)
            xs, _ = _conv_ffn_layer(xs, sts, m, 0, gm, *ws, *ffn)

    return (xp.reshape(x_prompt.shape), xs.reshape(x_sample.shape),
            jnp.stack(new_re, axis=1), jnp.stack(new_im, axis=1))
```

```python
import functools
import math

import jax
import jax.numpy as jnp
from jax import lax
from jax.experimental import pallas as pl
from jax.experimental.pallas import tpu as pltpu

EPS = 1e-6
N_MOD = 6
GRID_W = 64
SSM_GROUP = 16
SSM_CHUNK = 16
CONV_WIDTH = 3
SUBLANES = 8
LANES = 128
ROW_BLOCK = 512
FFN_ROW_BLOCK = 1024
SEQ_CHUNKS = 16
SEQ_PAIR = 2
CTX_ROWS = 4
VMEM_LIMIT = 56 * 1024 * 1024

F32 = jnp.float32
BF16 = jnp.bfloat16
NT = (((1,), (1,)), ((), ()))


def _cparams(n_axes=1, vmem=VMEM_LIMIT):
    return pltpu.CompilerParams(dimension_semantics=("arbitrary",) * n_axes, vmem_limit_bytes=vmem)


def _gelu(x):
    return 0.5 * x * (1.0 + jnp.tanh(math.sqrt(2.0 / math.pi) * (x + 0.044715 * (x * x * x))))


def _norm_mod(x, g, shift, scale):
    y = x * lax.rsqrt(jnp.mean(x * x, axis=-1, keepdims=True) + EPS)
    return (y * g) * (1.0 + scale) + shift


def _cmul(ar, ai, br, bi):
    return ar * br - ai * bi, ar * bi + ai * br


def _resident(shape):
    nd = len(shape)
    return pl.BlockSpec(shape, lambda *_: (0,) * nd, pipeline_mode=pl.Buffered(1))


def _side_plan(jobs, n_steps, step_of):
    kinds, ins, args, outs, shapes = [], [], [], [], []
    for job in jobs:
        if job[0] == "cast":
            _, stack, layer = job
            _, r, c = stack.shape
            blk = r // n_steps
            assert r % n_steps == 0 and blk % (2 * SUBLANES) == 0
            kinds.append(("cast", 1, None))
            ins.append(pl.BlockSpec((None, blk, c), lambda *ids, layer=layer: (layer, step_of(*ids), 0)))
            args.append(stack)
            outs.append(pl.BlockSpec((blk, c), lambda *ids: (step_of(*ids), 0)))
            shapes.append(jax.ShapeDtypeStruct((r, c), BF16))
        else:
            _, cond_t, w_mod, b_mod, layer, n_cond = job
            _, d, n = w_mod.shape
            tn = n // n_steps
            assert n % n_steps == 0 and tn % LANES == 0
            kinds.append(("adaln", 3, n_cond))
            ins += [_resident(cond_t.shape),
                    pl.BlockSpec((None, d, tn), lambda *ids, layer=layer: (layer, 0, step_of(*ids))),
                    pl.BlockSpec((None, 1, tn), lambda *ids, layer=layer: (layer, 0, step_of(*ids)))]
            args += [cond_t, w_mod, b_mod.reshape(b_mod.shape[0], 1, n)]
            outs.append(pl.BlockSpec((SUBLANES, tn), lambda *ids: (0, step_of(*ids))))
            shapes.append(jax.ShapeDtypeStruct((SUBLANES, n), F32))
    return tuple(kinds), ins, args, outs, shapes


def _side_counts(kinds):
    return sum(k[1] for k in kinds), len(kinds)


def _side_run(kinds, in_refs, out_refs):
    i = 0
    for (name, n_in, param), dst in zip(kinds, out_refs):
        if name == "cast":
            dst[...] = in_refs[i][...].astype(BF16)
        else:
            dst[...] = _adaln_rows(in_refs[i][...], in_refs[i + 1][...], param) + in_refs[i + 2][...]
        i += n_in


def _adaln_rows(ct, w, n_cond):
    s = ct * jax.nn.sigmoid(ct)
    rows = [jnp.sum(s[:, r:r + 1] * w, axis=0, keepdims=True) for r in range(n_cond)]
    rows += [jnp.zeros_like(rows[0])] * (SUBLANES - n_cond)
    return jnp.concatenate(rows, axis=0)


class _Stream:
    def __init__(self, n_chunks, n_batch, per_batch_cond, row_block=ROW_BLOCK):
        self.nc, self.nb = n_chunks, n_batch
        self.n_rows = n_chunks * n_batch * SSM_CHUNK
        self.per_batch_cond = per_batch_cond
        self.row_block = row_block
        self.row_chunks = row_block // SSM_CHUNK
        self.seq_blocks = n_chunks // SEQ_CHUNKS
        assert n_chunks % SEQ_CHUNKS == 0 and self.n_rows % row_block == 0 and n_batch % SEQ_PAIR == 0
        assert not per_batch_cond or n_chunks % self.row_chunks == 0

    def row_grid(self):
        return (self.nb, self.nc // self.row_chunks) if self.per_batch_cond else (self.n_rows // self.row_block,)

    def row_view(self, x):
        return x if self.per_batch_cond else x.reshape(self.n_rows, x.shape[-1])

    def row_spec(self, width):
        if self.per_batch_cond:
            return pl.BlockSpec((self.row_chunks, None, SSM_CHUNK, width), lambda b, j: (j, b, 0, 0))
        return pl.BlockSpec((self.row_block, width), lambda i: (i, 0))

    def row_shape(self, width):
        return (self.nc, self.nb, SSM_CHUNK, width) if self.per_batch_cond else (self.n_rows, width)

    def unview(self, x):
        return x.reshape(self.nc, self.nb, SSM_CHUNK, x.shape[-1])

    def nat_view(self, x):
        return x.reshape(self.nb, self.nc, SSM_CHUNK, x.shape[-1])

    def nat_spec(self, width):
        if self.per_batch_cond:
            return pl.BlockSpec((None, self.row_chunks, SSM_CHUNK, width), lambda b, j: (b, j, 0, 0))
        assert self.nb * SSM_CHUNK == self.row_block
        return pl.BlockSpec((self.nb, None, SSM_CHUNK, width), lambda i: (0, i, 0, 0))

    def pos_spec(self, width):
        return pl.BlockSpec((self.row_chunks, SSM_CHUNK, width), lambda b, j: (j, 0, 0))

    def seq_grid(self):
        return (self.nb // SEQ_PAIR, self.seq_blocks)

    def seq_spec(self, width):
        return pl.BlockSpec((SEQ_CHUNKS, SEQ_PAIR, SSM_CHUNK, width), lambda b, j: (j, b, 0, 0))

    def halo_specs(self, width):
        halves = SSM_CHUNK // SUBLANES
        shape = (None, SEQ_PAIR, None, SUBLANES, width)
        prev = pl.BlockSpec(shape, lambda b, j: (jnp.maximum(j * SEQ_CHUNKS - 1, 0), b, halves - 1, 0, 0))
        nxt = pl.BlockSpec(shape, lambda b, j: (jnp.minimum((j + 1) * SEQ_CHUNKS, self.nc - 1), b, 0, 0, 0))
        return prev, nxt

    def halo_view(self, x):
        return x.reshape(self.nc, self.nb, SSM_CHUNK // SUBLANES, SUBLANES, x.shape[-1])

    def mod_spec(self, layer, d):
        if self.per_batch_cond:
            return pl.BlockSpec((None, None, N_MOD, d), lambda b, j: (layer, CTX_ROWS + b, 0, 0))
        return pl.BlockSpec((None, None, N_MOD, d), lambda *_: (layer, 0, 0, 0))

    def seq_mod_spec(self, layer, d):
        if self.per_batch_cond:
            return pl.BlockSpec((None, SEQ_PAIR, N_MOD, d), lambda b, j: (layer, CTX_ROWS // SEQ_PAIR + b, 0, 0))
        return pl.BlockSpec((None, SEQ_PAIR, N_MOD, d), lambda *_: (layer, 0, 0, 0))

    def n_axes(self, seq=False):
        return 2 if (seq or self.per_batch_cond) else 1

    def side_plan(self, jobs, seq=False):
        if seq:
            return _side_plan(jobs, (self.nb // SEQ_PAIR) * self.seq_blocks, lambda b, j: b * self.seq_blocks + j)
        assert not jobs or not self.per_batch_cond
        return _side_plan(jobs, self.n_rows // self.row_block, lambda i: i)


def _grid_pos_embed(n_tokens, d):
    rows = n_tokens // GRID_W
    quarter = d // 4
    freq = 1.0 / (10000.0 ** (jnp.arange(quarter, dtype=F32) / quarter))
    ar = jnp.arange(rows, dtype=F32)[:, None] * freq
    ac = jnp.arange(GRID_W, dtype=F32)[:, None] * freq
    row_part = jnp.concatenate([jnp.sin(ar), jnp.cos(ar)], axis=-1)
    col_part = jnp.concatenate([jnp.sin(ac), jnp.cos(ac)], axis=-1)
    table = jnp.concatenate([jnp.broadcast_to(row_part[:, None], (rows, GRID_W, d // 2)),
                             jnp.broadcast_to(col_part[None], (rows, GRID_W, d // 2))], axis=-1)
    return table.reshape(rows * GRID_W, d)


def _adaln_kernel(ct_ref, w_ref, b_ref, o_ref, *, n_cond):
    o_ref[...] = _adaln_rows(ct_ref[...], w_ref[...], n_cond) + b_ref[...]


def _adaln(cond_t, w_mod, b_mod, n_cond, layers, tn=1024):
    depth, d, n = w_mod.shape
    l0 = layers[0]
    assert list(layers) == list(range(l0, l0 + len(layers)))
    return pl.pallas_call(
        functools.partial(_adaln_kernel, n_cond=n_cond),
        grid=(len(layers), n // tn),
        in_specs=[
            pl.BlockSpec((d, SUBLANES), lambda l, j: (0, 0)),
            pl.BlockSpec((None, d, tn), lambda l, j: (l + l0, 0, j)),
            pl.BlockSpec((None, 1, tn), lambda l, j: (l + l0, 0, j)),
        ],
        out_specs=pl.BlockSpec((None, SUBLANES, tn), lambda l, j: (l, 0, j)),
        out_shape=jax.ShapeDtypeStruct((len(layers), SUBLANES, n), F32),
        compiler_params=_cparams(2), name="adaln",
    )(cond_t, w_mod, b_mod.reshape(depth, 1, n))


def _rows(ref):
    return ref[...].reshape(-1, ref.shape[-1])


def _rms(x, g):
    return (x * lax.rsqrt(jnp.mean(x * x, axis=-1, keepdims=True) + EPS)) * g


def _ffn_blocks(xs, mods, g, w1_ref, w2_ref, th, g_final=None):
    hn = jnp.concatenate([_norm_mod(x, g, m[3:4], m[4:5]).astype(BF16) for x, m in zip(xs, mods)], axis=0)
    acc = jnp.zeros((hn.shape[0], w2_ref.shape[1]), F32)
    for c in range(w1_ref.shape[1] // th):
        h1 = jnp.dot(hn, w1_ref[:, c * th:(c + 1) * th], preferred_element_type=F32)
        h1 = jnp.square(jnp.maximum(h1, 0.0)).astype(BF16)
        acc = acc + jnp.dot(h1, w2_ref[c * th:(c + 1) * th, :], preferred_element_type=F32)
    outs, r0 = [], 0
    for x, m in zip(xs, mods):
        out = x + m[5:6] * acc[r0:r0 + x.shape[0]]
        outs.append(out if g_final is None else _rms(out, g_final))
        r0 += x.shape[0]
    return outs


def _ffn_kernel(x_ref, mod_ref, g_ref, w1_ref, w2_ref, *rest, th, side):
    n_in, _ = _side_counts(side)
    o_ref = rest[n_in]
    out, = _ffn_blocks([_rows(x_ref)], [mod_ref[...]], g_ref[...], w1_ref, w2_ref, th)
    o_ref[...] = out.reshape(o_ref.shape)
    _side_run(side, rest[:n_in], rest[n_in + 1:])


def _ffn(x, st, mods, layer, g, w1, w2, jobs=(), th=1024):
    d = x.shape[-1]
    side, s_in, s_args, s_out, s_shapes = st.side_plan(jobs)
    out, *extra = pl.pallas_call(
        functools.partial(_ffn_kernel, th=th, side=side),
        grid=st.row_grid(),
        in_specs=[st.row_spec(d), st.mod_spec(layer, d), _resident((1, d)), _resident(w1.shape), _resident(w2.shape)]
        + s_in,
        out_specs=[st.row_spec(d)] + s_out,
        out_shape=[jax.ShapeDtypeStruct(st.row_shape(d), F32)] + s_shapes,
        compiler_params=_cparams(st.n_axes()), name="ffn",
    )(st.row_view(x), mods, g, w1, w2, *s_args)
    return st.unview(out), extra


def _x_operands(x, st, pos):
    d = x.shape[-1]
    if x.ndim == 4:
        return [st.row_spec(d)], [st.row_view(x)]
    specs, args = [st.nat_spec(d)], [st.nat_view(x)]
    if pos is not None:
        specs.append(st.pos_spec(d))
        args.append(pos.reshape(st.nc, SSM_CHUNK, d))
    return specs, args


def _t_major_perm(n_batch):
    n_chunks = ROW_BLOCK // (n_batch * SSM_CHUNK)
    src = jnp.arange(ROW_BLOCK)
    b, c, t = src // (n_chunks * SSM_CHUNK), (src // SSM_CHUNK) % n_chunks, src % SSM_CHUNK
    dst = (t * n_chunks + c) * n_batch + b
    return (jnp.arange(ROW_BLOCK)[:, None] == dst[None, :]).astype(BF16)


def _ssm_in_kernel(x_ref, *rest, n_batch, natural, n_sub):
    *pos_ref, mod_ref, g_ref, perm_ref, w_ref, o_ref = rest
    g = g_ref[...]
    d = x_ref.shape[-1]
    group = o_ref.shape[1] // n_sub
    cpg = group // n_batch if n_batch > 1 else 1
    for h in range(n_sub):
        if n_batch == 1:
            mod = mod_ref[...]
            xh = x_ref[:, h] if natural else x_ref[h]
            hn = _norm_mod(xh.reshape(-1, d), g, mod[0:1], mod[1:2]).astype(BF16)
        else:
            parts = []
            cs = slice(h * cpg, (h + 1) * cpg)
            for b in range(n_batch):
                xb = (x_ref[b, cs] if natural else x_ref[cs, b]).reshape(-1, d)
                if pos_ref:
                    xb = xb + pos_ref[0][cs].reshape(-1, d)
                mod = mod_ref[b]
                parts.append(_norm_mod(xb, g, mod[0:1], mod[1:2]).astype(BF16))
            hn = jnp.concatenate(parts, axis=0)
        hn = jnp.dot(perm_ref[...], hn, preferred_element_type=F32).astype(BF16)
        u = jnp.dot(hn, w_ref[...], preferred_element_type=F32)
        o_ref[:, h * group:(h + 1) * group, :] = u.astype(BF16).reshape(o_ref.shape[0], group, o_ref.shape[2])


def _ssm_in(x, st, mods, layer, g, w, pos=None, n_sub=2):
    d, width = x.shape[-1], w.shape[1]
    natural = x.ndim != 4
    group = ROW_BLOCK // SSM_CHUNK
    if not st.per_batch_cond:
        assert st.nb == group and st.nc % n_sub == 0
        cps, n_batch = n_sub, 1
        mod_spec = st.mod_spec(layer, d)
    else:
        assert group % st.nb == 0 and CTX_ROWS % st.nb == 0
        cps, n_batch = n_sub * (group // st.nb), st.nb
        assert st.nc % cps == 0
        mod_spec = pl.BlockSpec((None, st.nb, N_MOD, d), lambda i: (layer, CTX_ROWS // st.nb, 0, 0))
    if natural:
        x_specs, x_args = [pl.BlockSpec((st.nb, cps, SSM_CHUNK, d), lambda i: (0, i, 0, 0))], [st.nat_view(x)]
        if pos is not None:
            x_specs.append(pl.BlockSpec((cps, SSM_CHUNK, d), lambda i: (i, 0, 0)))
            x_args.append(pos.reshape(st.nc, SSM_CHUNK, d))
    else:
        x_specs, x_args = [pl.BlockSpec((cps, st.nb, SSM_CHUNK, d), lambda i: (i, 0, 0, 0))], [x]
    perm = _t_major_perm(st.nb)
    return pl.pallas_call(
        functools.partial(_ssm_in_kernel, n_batch=n_batch, natural=natural, n_sub=n_sub),
        grid=(st.nc // cps,),
        in_specs=x_specs + [mod_spec, _resident((1, d)), _resident(perm.shape), _resident(w.shape)],
        out_specs=pl.BlockSpec((SSM_CHUNK, n_sub * group, width), lambda i: (0, i, 0)),
        out_shape=jax.ShapeDtypeStruct((SSM_CHUNK, st.nc * st.nb, width), BF16),
        compiler_params=_cparams(1), name="ssm_in",
    )(*x_args, mods, g, perm, w)


def _ssm_out_ffn_kernel(x_ref, *rest, has_pos, has_final, th, side):
    pos_ref = rest[0] if has_pos else None
    y_ref, mod_ref, wout_ref, gf_ref, w1_ref, w2_ref = rest[has_pos:has_pos + 6]
    n_in = has_pos + 6 + has_final
    n_side, _ = _side_counts(side)
    g_final = rest[n_in - 1][...] if has_final else None
    o_ref = rest[n_in + n_side]
    x = _rows(x_ref) + _rows(pos_ref) if has_pos else _rows(x_ref)
    y = _rows(y_ref)
    mod = mod_ref[...]
    d = x.shape[-1]
    half = x.shape[0] // 2
    mids = []
    for k in range(2):
        rows = slice(k * half, (k + 1) * half)
        ag = jnp.dot(_gelu(y[rows]).astype(BF16), wout_ref[...], preferred_element_type=F32)
        mids.append(x[rows] + mod[2:3] * (ag[:, :d] * jax.nn.sigmoid(ag[:, d:])))
    outs = _ffn_blocks(mids, [mod, mod], gf_ref[...], w1_ref, w2_ref, th, g_final)
    o_ref[...] = jnp.concatenate(outs, axis=0).reshape(o_ref.shape)
    _side_run(side, rest[n_in:n_in + n_side], rest[n_in + n_side + 1:])


def _ssm_out_ffn(x, y, st, mods, layer, w_out, gf, w1, w2, pos=None, g_final=None, jobs=(), th=1024):
    d = x.shape[-1]
    x_specs, x_args = _x_operands(x, st, pos)
    last = g_final is not None
    side, s_in, s_args, s_out, s_shapes = st.side_plan(jobs)
    out, *extra = pl.pallas_call(
        functools.partial(_ssm_out_ffn_kernel, has_pos=len(x_args) == 2, has_final=last, th=th, side=side),
        grid=st.row_grid(),
        in_specs=x_specs + [st.row_spec(d), st.mod_spec(layer, d), _resident(w_out.shape), _resident((1, d)),
                            _resident(w1.shape), _resident(w2.shape)] + ([_resident((1, d))] if last else []) + s_in,
        out_specs=[st.nat_spec(d) if last else st.row_spec(d)] + s_out,
        out_shape=[jax.ShapeDtypeStruct((st.nb, st.nc, SSM_CHUNK, d) if last else st.row_shape(d), F32)] + s_shapes,
        compiler_params=_cparams(st.n_axes()), name="ssm_out_ffn",
    )(*x_args, st.row_view(st.unview(y)), mods, w_out, gf, w1, w2, *([g_final] if last else []), *s_args)
    return (out if last else st.unview(out)), extra


def _ssm_operators(prow, bt, ct):
    nk = SSM_CHUNK
    half = prow.shape[1] // 2
    lr, li = prow[0:1], prow[1:2]
    dt = jnp.exp(prow[2:3])
    mag = jnp.exp(lr * dt)
    ar, ai = mag * jnp.cos(li * dt), mag * jnp.sin(li * dt)
    den = lr * lr + li * li
    nr, ni = _cmul(ar - 1.0, ai, lr, -li)
    bbr, bbi = _cmul(nr / den, ni / den, bt[0:nk], bt[nk:2 * nk])
    ctr, cti = ct[0:nk], ct[nk:2 * nk]

    pr, pi = [jnp.ones_like(ar)], [jnp.zeros_like(ar)]
    for _ in range(nk):
        r, i = _cmul(pr[-1], pi[-1], ar, ai)
        pr.append(r)
        pi.append(i)
    is_fwd = lax.broadcasted_iota(jnp.int32, (1, 2 * half), 1) < half

    def pw(kf, kb):
        return jnp.where(is_fwd, pr[kf], pr[kb]), jnp.where(is_fwd, pi[kf], pi[kb])

    wp_r, wp_i, ca_r, ca_i, wc_r, wc_i = [], [], [], [], [], []
    for t in range(nk):
        r, i = _cmul(bbr, bbi, *pw(nk - 1 - t, t))
        wp_r.append(r)
        wp_i.append(i)
        r, i = _cmul(ctr, cti, *pw(t, nk - 1 - t))
        ca_r.append(r)
        ca_i.append(i)
        r, i = _cmul(ctr, cti, *pw(t + 1, nk - t))
        wc_r.append(r)
        wc_i.append(-i)
    cat = jnp.concatenate
    wp = cat([cat(wp_r, 0), cat(wp_i, 0)], axis=1).astype(BF16)
    ca = cat([cat(ca_r, 0), cat(ca_i, 0)], axis=1)
    wct = cat([cat(wc_r, 0), cat(wc_i, 0)], axis=1).astype(BF16)

    zero = jnp.zeros_like(bbr)
    lhs = cat([cat([jnp.where(is_fwd, bbr, zero), jnp.where(is_fwd, -bbi, zero)], 1),
               cat([jnp.where(is_fwd, zero, bbr), jnp.where(is_fwd, zero, -bbi)], 1)], 0)
    kk = lax.dot_general(lhs, ca, NT, precision=lax.Precision.HIGHEST, preferred_element_type=F32)
    width = nk * SSM_GROUP
    lane = lax.broadcasted_iota(jnp.int32, (nk, width), 1)
    row = lax.broadcasted_iota(jnp.int32, (nk, width), 0)
    d_lanes = cat([prow[3:4], jnp.zeros_like(prow[3:4])], axis=1)
    kf = kk[0:nk] + jnp.where(lane == row, d_lanes, 0.0)
    return kf, kk[nk:2 * nk], wp, wct, pr[nk], pi[nk]


def _toeplitz_shifts():
    width = SSM_CHUNK * SSM_GROUP
    src = jnp.arange(width)[:, None]
    dst = jnp.arange(width)[None, :]
    fwd = [dst == src + SSM_GROUP * s for s in range(SSM_CHUNK)]
    bwd = [src == dst + SSM_GROUP * (SSM_CHUNK - 1 - s) for s in range(SSM_CHUNK)]
    return jnp.concatenate(fwd, axis=1).astype(BF16), jnp.concatenate(bwd, axis=1).astype(BF16)


def _ssm_core_kernel(prow_ref, bt_ref, ct_ref, h0r_ref, h0i_ref, shf_ref, shb_ref, up_ref, us_ref, *rest,
                     p_batch, s_batch, side):
    n_in, n_out = _side_counts(side)
    yp_ref, ys_ref, fr_ref, fi_ref = rest[n_in:n_in + 4]
    atp_scr, ats_scr, tg_scr, wp_scr, wct_scr, u_scr, p_scr, s_scr, q_scr = rest[n_in + 4 + n_out:]
    _side_run(side, rest[:n_in], rest[n_in + 4:n_in + 4 + n_out])
    nk = SSM_CHUNK
    n_groups = LANES // SSM_GROUP
    cb_p, cb_s = atp_scr.shape[2], ats_scr.shape[2]
    w2 = prow_ref.shape[2]
    width = nk * SSM_GROUP
    decay, kfs, kbs = [], [], []
    for g in range(n_groups):
        kf, kb, wp, wct, a_r, a_i = _ssm_operators(prow_ref[g], bt_ref[g], ct_ref[g])
        kfs.append(kf)
        kbs.append(kb)
        wp_scr[g] = wp
        wct_scr[g] = wct
        decay.append((a_r, a_i))
    tz = (jnp.dot(jnp.concatenate(kfs, axis=0).astype(BF16), shf_ref[...], preferred_element_type=F32)
          + jnp.dot(jnp.concatenate(kbs, axis=0).astype(BF16), shb_ref[...], preferred_element_type=F32))
    for g in range(n_groups):
        for s in range(nk):
            tg_scr[g, s * SSM_GROUP:(s + 1) * SSM_GROUP, :] = (
                tz[g * SSM_GROUP:(g + 1) * SSM_GROUP, s * width:(s + 1) * width].astype(BF16))

    is_fwd = lax.broadcasted_iota(jnp.int32, (1, w2), 1) < (w2 // 2)

    def scan(n_tiles, rows, src_scr, carries, mults):
        for j in range(n_tiles):
            lo, hi = j * rows, (n_tiles - 1 - j) * rows
            for g in range(n_groups):
                c_r, c_i = carries[g]
                for c, l0 in ((c_r, 0), (c_i, w2)):
                    if lo < hi:
                        s_scr[g, lo:lo + rows, l0:l0 + w2] = c
                        s_scr[g, hi:hi + rows, l0:l0 + w2] = c
                    else:
                        s_scr[g, lo:lo + rows, l0:l0 + w2] = jnp.where(is_fwd, c, s_scr[g, lo:lo + rows, l0:l0 + w2])
                        s_scr[g, hi:hi + rows, l0:l0 + w2] = jnp.where(is_fwd, s_scr[g, hi:hi + rows, l0:l0 + w2], c)
                s_r = jnp.where(is_fwd, src_scr[g, lo:lo + rows, 0:w2], src_scr[g, hi:hi + rows, 0:w2])
                s_i = jnp.where(is_fwd, src_scr[g, lo:lo + rows, w2:2 * w2], src_scr[g, hi:hi + rows, w2:2 * w2])
                n_r, n_i = _cmul(*mults[g], c_r, c_i)
                carries[g] = (n_r + s_r, n_i + s_i)
        return carries

    def prompt_carry(cb):
        z = jnp.zeros((p_batch, w2), F32)
        return scan(cb // p_batch, p_batch, p_scr, [(z, z)] * n_groups, decay)

    def sample_carry(cb):
        assert 2 * s_batch == SUBLANES
        row = lax.broadcasted_iota(jnp.int32, (SUBLANES, w2), 0)
        keep = jnp.where(is_fwd, 1, 0) == jnp.where(row < s_batch, 1, 0)
        carries, mults = [], []
        for g in range(n_groups):
            a_r, a_i = decay[g]
            p_r, p_i = p_scr[g, 0:cb, 0:w2], p_scr[g, 0:cb, w2:2 * w2]
            ap_r, ap_i = _cmul(a_r, a_i, p_r, p_i)
            q_scr[g, 0:cb, 0:w2] = jnp.where(is_fwd, pltpu.roll(p_r, cb - s_batch, 0), pltpu.roll(p_r, s_batch, 0)) + ap_r
            q_scr[g, 0:cb, w2:2 * w2] = jnp.where(is_fwd, pltpu.roll(p_i, cb - s_batch, 0),
                                                  pltpu.roll(p_i, s_batch, 0)) + ap_i
            h_r, h_i = h0r_ref[g], h0i_ref[g]
            e_r = pltpu.roll(jnp.where(is_fwd, p_r[0:SUBLANES], p_r[cb - SUBLANES:cb]), s_batch, 0)
            e_i = pltpu.roll(jnp.where(is_fwd, p_i[0:SUBLANES], p_i[cb - SUBLANES:cb]), s_batch, 0)
            ah_r, ah_i = _cmul(a_r, a_i, h_r, h_i)
            carries.append((jnp.where(keep, h_r, ah_r + e_r), jnp.where(keep, h_i, ah_i + e_i)))
            mults.append(_cmul(a_r, a_i, a_r, a_i))
        return scan(cb // SUBLANES, SUBLANES, q_scr, carries, mults)

    def mix(u_ref, y_ref, at_scr, cb, carry_fn):
        for t in range(nk):
            at_scr[t] = u_ref[t].T
        for g in range(n_groups):
            a = at_scr[:, g * SSM_GROUP:(g + 1) * SSM_GROUP, :].reshape(nk * SSM_GROUP, cb)
            u = a.T
            u_scr[g, 0:cb, :] = u
            p_scr[g, 0:cb, :] = jnp.dot(u, wp_scr[g], preferred_element_type=F32)
        finals = carry_fn(cb)
        for g in range(n_groups):
            s = s_scr[g, 0:cb, :].astype(BF16)
            y = (jnp.dot(u_scr[g, 0:cb, :], tg_scr[g], preferred_element_type=F32)
                 + lax.dot_general(s, wct_scr[g], NT, preferred_element_type=F32))
            at_scr[:, g * SSM_GROUP:(g + 1) * SSM_GROUP, :] = y.astype(BF16).T.reshape(nk, SSM_GROUP, cb)
        for t in range(nk):
            y_ref[pl.ds(t, cb, stride=nk), :] = at_scr[t].T.astype(F32)
        return finals

    finals = mix(up_ref, yp_ref, atp_scr, cb_p, prompt_carry)
    for g in range(n_groups):
        fr_ref[g] = finals[g][0]
        fi_ref[g] = finals[g][1]
    mix(us_ref, ys_ref, ats_scr, cb_s, sample_carry)


def _ssm_core(prow, bt, ct, h0r, h0i, up, us, p_batch, s_batch, jobs=()):
    groups, _, w2 = prow.shape
    gpb = LANES // SSM_GROUP
    cb_p, cb_s = up.shape[1], us.shape[1]
    np_rows, ns_rows = cb_p * SSM_CHUNK, cb_s * SSM_CHUNK
    cb_max = max(cb_p, cb_s)
    width = SSM_CHUNK * SSM_GROUP
    shf, shb = _toeplitz_shifts()

    def gspec(r, c):
        return pl.BlockSpec((gpb, r, c), lambda o: (o, 0, 0))

    def lane_spec(n):
        return pl.BlockSpec((n, LANES), lambda o: (0, o))

    def in_spec(cb):
        return pl.BlockSpec((SSM_CHUNK, cb, LANES), lambda o: (0, 0, o))

    side, c_in, c_args, c_out, c_shapes = _side_plan(jobs, groups // gpb, lambda o: o)
    yp, ys, f_re, f_im, *cast = pl.pallas_call(
        functools.partial(_ssm_core_kernel, p_batch=p_batch, s_batch=s_batch, side=side),
        grid=(groups // gpb,),
        in_specs=[gspec(SUBLANES, w2), gspec(2 * SSM_GROUP, w2), gspec(2 * SSM_GROUP, w2),
                  gspec(SUBLANES, w2), gspec(SUBLANES, w2), _resident(shf.shape), _resident(shb.shape),
                  in_spec(cb_p), in_spec(cb_s)] + c_in,
        out_specs=[lane_spec(np_rows), lane_spec(ns_rows), gspec(p_batch, w2), gspec(p_batch, w2)] + c_out,
        out_shape=[jax.ShapeDtypeStruct((np_rows, up.shape[2]), F32), jax.ShapeDtypeStruct((ns_rows, us.shape[2]), F32),
                   jax.ShapeDtypeStruct((groups, p_batch, w2), F32),
                   jax.ShapeDtypeStruct((groups, p_batch, w2), F32)] + c_shapes,
        scratch_shapes=[pltpu.VMEM((SSM_CHUNK, LANES, cb_p), BF16), pltpu.VMEM((SSM_CHUNK, LANES, cb_s), BF16)]
        + [pltpu.VMEM((gpb, width, width), BF16)] * 3 + [pltpu.VMEM((gpb, cb_max, width), BF16)]
        + [pltpu.VMEM((gpb, cb_max, 2 * w2), F32)] * 2 + [pltpu.VMEM((gpb, cb_s, 2 * w2), F32)],
        compiler_params=_cparams(), name="ssm_core",
    )(prow, bt, ct, h0r, h0i, shf, shb, up, us, *c_args)
    return yp, ys, f_re, f_im, cast


def _ssm_layer(xp, xs, stp, sts, mods, layer, g_mix, w_in, lam_re, lam_im, log_dt, b_re, b_im, c_re, c_im, d_skip,
               w_out, h0_re, h0_im, ffn, pos=None, g_final=None, jobs=(), late=()):
    width = w_in.shape[1]
    groups = width // SSM_GROUP
    n_state = lam_re.shape[-1]
    up = _ssm_in(xp, stp, mods, layer, g_mix, w_in)
    us = _ssm_in(xs, sts, mods, layer, g_mix, w_in, pos)

    def lanes_dir_state(v):
        return v.transpose(1, 0, 2).reshape(groups, 2 * n_state)

    d_rows = jnp.pad(d_skip.reshape(groups, SSM_GROUP), ((0, 0), (0, 2 * n_state - SSM_GROUP)))
    prow = jnp.stack([lanes_dir_state(lam_re), lanes_dir_state(lam_im),
                      lanes_dir_state(jnp.broadcast_to(log_dt[..., None], lam_re.shape)), d_rows], axis=1)
    prow = jnp.pad(prow, ((0, 0), (0, SUBLANES - 4), (0, 0)))
    bt = jnp.concatenate([b_re.transpose(1, 3, 0, 2).reshape(groups, SSM_GROUP, 2 * n_state),
                          b_im.transpose(1, 3, 0, 2).reshape(groups, SSM_GROUP, 2 * n_state)], axis=1)
    ct = jnp.concatenate([c_re.transpose(1, 2, 0, 3).reshape(groups, SSM_GROUP, 2 * n_state),
                          c_im.transpose(1, 2, 0, 3).reshape(groups, SSM_GROUP, 2 * n_state)], axis=1)

    def h0_rows(h):
        h = h.transpose(2, 0, 1, 3).reshape(groups, sts.nb, 2 * n_state)
        return jnp.concatenate([h] * (SUBLANES // sts.nb), axis=1)

    yp, ys, f_re, f_im, conv = _ssm_core(prow, bt, ct, h0_rows(h0_re), h0_rows(h0_im), up, us, stp.nb, sts.nb, late)
    if late:
        w_out, ffn = conv[0], (ffn[0], conv[1], conv[2])
    xp, cast = _ssm_out_ffn(xp, yp, stp, mods, layer, w_out, *ffn, g_final=g_final, jobs=jobs)
    xs, _ = _ssm_out_ffn(xs, ys, sts, mods, layer, w_out, *ffn, pos=pos, g_final=g_final)

    def final(f):
        return f.reshape(groups, stp.nb, 2, n_state).transpose(1, 2, 0, 3)

    return xp, xs, final(f_re), final(f_im), cast


def _seq_rows(ref, k):
    return ref[:, k].reshape(-1, ref.shape[-1])


def _gmlp_kernel(x_ref, mod_ref, g_ref, win_ref, ws_ref, bs_ref, wout_ref, *rest, side):
    n_in, n_out = _side_counts(side)
    o_ref, t_scr = rest[n_in], rest[n_in + 1 + n_out]
    _side_run(side, rest[:n_in], rest[n_in + 1:n_in + 1 + n_out])
    n_groups, chunk, gd = bs_ref.shape
    xs, us, vns = [], [], []
    for k in range(SEQ_PAIR):
        x = _seq_rows(x_ref, k)
        mod = mod_ref[k]
        hn = _norm_mod(x, g_ref[...], mod[0:1], mod[1:2]).astype(BF16)
        z = _gelu(jnp.dot(hn, win_ref[...], preferred_element_type=F32))
        wdt = z.shape[1] // 2
        u, v = z[:, :wdt], z[:, wdt:]
        vc = v - jnp.mean(v, axis=-1, keepdims=True)
        xs.append(x)
        us.append(u)
        vns.append((vc * lax.rsqrt(jnp.mean(vc * vc, axis=-1, keepdims=True) + EPS)).astype(BF16))
    n_chunks = xs[0].shape[0] // chunk
    for g in range(n_groups):
        cols = slice(g * gd, (g + 1) * gd)
        rhs = jnp.concatenate([vn[c * chunk:(c + 1) * chunk, cols] for vn in vns for c in range(n_chunks)], axis=1)
        s = jnp.dot(ws_ref[g], rhs, preferred_element_type=F32)
        for k in range(SEQ_PAIR):
            for c in range(n_chunks):
                rows = slice(c * chunk, (c + 1) * chunk)
                i = k * n_chunks + c
                t_scr[k, rows, cols] = (us[k][rows, cols] * (s[:, i * gd:(i + 1) * gd] + bs_ref[g])).astype(BF16)
    for k in range(SEQ_PAIR):
        out = jnp.dot(t_scr[k], wout_ref[...], preferred_element_type=F32)
        o_ref[:, k] = (xs[k] + mod_ref[k][2:3] * out).reshape(o_ref.shape[0], o_ref.shape[2], o_ref.shape[3])


def _gmlp_layer(x, st, mods, layer, g_mix, w_in, w_s, bs, w_out, jobs=()):
    d = x.shape[-1]
    side, s_in, s_args, s_out, s_shapes = st.side_plan(jobs, seq=True)
    out, *extra = pl.pallas_call(
        functools.partial(_gmlp_kernel, side=side),
        grid=st.seq_grid(),
        in_specs=[st.seq_spec(d), st.seq_mod_spec(layer, d), _resident((1, d)), _resident(w_in.shape),
                  _resident(w_s.shape), _resident(bs.shape), _resident(w_out.shape)] + s_in,
        out_specs=[st.seq_spec(d)] + s_out,
        out_shape=[jax.ShapeDtypeStruct(x.shape, F32)] + s_shapes,
        scratch_shapes=[pltpu.VMEM((SEQ_PAIR, SEQ_CHUNKS * SSM_CHUNK, w_out.shape[0]), BF16)],
        compiler_params=_cparams(2), name="gmlp",
    )(x, mods, g_mix, w_in, w_s, bs, w_out, *s_args)
    return out, extra


def _conv_ffn_kernel(x_ref, xprev_ref, xnext_ref, mod_ref, g_ref, win_ref, cw_ref, wout_ref, gf_ref, w1_ref, w2_ref,
                     *rest, seq_blocks, th, side):
    n_in, _ = _side_counts(side)
    o_ref = rest[n_in]
    j = pl.program_id(1)
    g = g_ref[...]
    cw = cw_ref[...]
    d = x_ref.shape[-1]
    xs = [_seq_rows(x_ref, k) for k in range(SEQ_PAIR)]
    mods = [mod_ref[k] for k in range(SEQ_PAIR)]
    if seq_blocks > 1:
        hh = jnp.concatenate([_norm_mod(jnp.concatenate([xprev_ref[k], xnext_ref[k]], axis=0), g, m[0:1], m[1:2])
                              for k, m in enumerate(mods)], axis=0).astype(BF16)
        zh = jnp.dot(hh, win_ref[:, d:], preferred_element_type=F32)
        th_all = zh[:, :d] * zh[:, d:]
    else:
        th_all = jnp.zeros((2 * SUBLANES * SEQ_PAIR, d), F32)
    mids = []
    for k, (x, mod) in enumerate(zip(xs, mods)):
        tm = x.shape[0]
        hn = _norm_mod(x, g, mod[0:1], mod[1:2]).astype(BF16)
        z = jnp.dot(hn, win_ref[...], preferred_element_type=F32)
        gb, t = z[:, :d], z[:, d:2 * d] * z[:, 2 * d:]
        r0 = 2 * SUBLANES * k
        t_prev = jnp.where(j > 0, th_all[r0 + SUBLANES - 1:r0 + SUBLANES], 0.0)
        t_next = jnp.where(j < seq_blocks - 1, th_all[r0 + SUBLANES:r0 + SUBLANES + 1], 0.0)
        row = lax.broadcasted_iota(jnp.int32, (tm, 1), 0)
        up = jnp.where(row == 0, t_prev, pltpu.roll(t, 1, 0))
        dn = jnp.where(row == tm - 1, t_next, pltpu.roll(t, tm - 1, 0))
        y = cw[0:1] * up + cw[1:2] * t + cw[2:3] * dn
        out = jnp.dot((gb * y).astype(BF16), wout_ref[...], preferred_element_type=F32)
        mids.append(x + mod[2:3] * out)
    outs = _ffn_blocks(mids, mods, gf_ref[...], w1_ref, w2_ref, th)
    for k, out in enumerate(outs):
        o_ref[:, k] = out.reshape(o_ref.shape[0], o_ref.shape[2], o_ref.shape[3])
    _side_run(side, rest[:n_in], rest[n_in + 1:])


def _conv_ffn_layer(x, st, mods, layer, g_mix, w_in, cw, w_out, gf, w1, w2, jobs=(), th=1024):
    d = x.shape[-1]
    prev, nxt = st.halo_specs(d)
    xh = st.halo_view(x)
    side, c_in, c_args, c_out, c_shapes = st.side_plan(jobs, seq=True)
    out, *cast = pl.pallas_call(
        functools.partial(_conv_ffn_kernel, seq_blocks=st.seq_blocks, th=th, side=side),
        grid=st.seq_grid(),
        in_specs=[st.seq_spec(d), prev, nxt, st.seq_mod_spec(layer, d), _resident((1, d)), _resident(w_in.shape),
                  _resident(cw.shape), _resident(w_out.shape), _resident((1, d)), _resident(w1.shape),
                  _resident(w2.shape)] + c_in,
        out_specs=[st.seq_spec(d)] + c_out,
        out_shape=[jax.ShapeDtypeStruct(x.shape, F32)] + c_shapes,
        compiler_params=_cparams(2), name="conv_ffn",
    )(x, xh, xh, mods, g_mix, w_in, cw, w_out, gf, w1, w2, *c_args)
    return out, cast


def kernel(x_prompt, x_sample, state_ssm_re, state_ssm_im, c, c_ctx, w_mod, b_mod, g_mix, g_ffn, ffn_w1, ffn_w2, ssm_w_in, ssm_lam_re, ssm_lam_im, ssm_log_dt, ssm_b_re, ssm_b_im, ssm_c_re, ssm_c_im, ssm_d, ssm_w_out, gmlp_w_in, gmlp_w_s, gmlp_b_s, gmlp_w_out, conv_w_in, conv_w, conv_w_out, g_final):
    n_prompt, len_prompt, d = x_prompt.shape
    n_sample, len_sample, _ = x_sample.shape
    depth = w_mod.shape[0]
    n_mixers = 3
    assert 1 + n_sample <= SUBLANES
    stp = _Stream(len_prompt // SSM_CHUNK, n_prompt, per_batch_cond=False)
    sts = _Stream(len_sample // SSM_CHUNK, n_sample, per_batch_cond=True)
    sts_ffn = _Stream(len_sample // SSM_CHUNK, n_sample, per_batch_cond=True, row_block=FFN_ROW_BLOCK)

    xp, xs = x_prompt, x_sample
    pos = _grid_pos_embed(len_sample, d)

    cond_t = jnp.concatenate([c_ctx[None, :], c, jnp.zeros((SUBLANES - 1 - n_sample, d), F32)], axis=0).T
    n_cond = 1 + n_sample
    table = jnp.array([0] * CTX_ROWS + list(range(1, n_cond)))

    def mod_table(m):
        return m[table].reshape(1, table.shape[0], N_MOD, d)

    assert (depth - 1) % n_mixers == 0, "the last layer's kernel writes the natural-order outputs"

    def layer_weights(i):
        kind, j = i % n_mixers, i // n_mixers
        mixer = {0: [(ssm_w_in, j), (ssm_w_out, j)],
                 1: [(gmlp_w_in, j), (gmlp_w_s.reshape(gmlp_w_s.shape[0], -1, gmlp_w_s.shape[-1]), j), (gmlp_w_out, j)],
                 2: [(conv_w_in, j), (conv_w_out, j)]}[kind]
        return [("cast", stack, layer) for stack, layer in mixer + [(ffn_w1, i), (ffn_w2, i)]]

    def adaln_jobs(layers):
        return [("adaln", cond_t, w_mod, b_mod, l, n_cond) for l in layers]

    first = layer_weights(0)
    wb = [first[0][1][first[0][2]].astype(BF16)] + [None] * (len(first) - 1)
    mods = {0: mod_table(_adaln(cond_t, w_mod, b_mod, n_cond, [0])[0])}
    new_re, new_im = [], []
    for i in range(depth):
        kind, j = i % n_mixers, i // n_mixers
        gm = g_mix[i].reshape(1, d)
        nxt = layer_weights(i + 1) if i + 1 < depth else []
        *wm, w1, w2 = wb
        ffn = (g_ffn[i].reshape(1, d), w1, w2)
        m = mods[i]
        if kind == 0:
            later = adaln_jobs([i + 1]) if i + 1 < depth else []
            xp, xs, f_re, f_im, extra = _ssm_layer(
                xp, xs, stp, sts, m, 0, gm, wm[0], ssm_lam_re[j], ssm_lam_im[j], ssm_log_dt[j],
                ssm_b_re[j], ssm_b_im[j], ssm_c_re[j], ssm_c_im[j], ssm_d[j], wm[1],
                state_ssm_re[:, j], state_ssm_im[:, j], ffn, pos if i == 0 else None,
                g_final.reshape(1, d) if i == depth - 1 else None, nxt + later, first[1:] if i == 0 else ())
            wb = extra[:len(nxt)]
            for job, out in zip(later, extra[len(nxt):]):
                mods[job[4]] = mod_table(out)
            new_re.append(f_re)
            new_im.append(f_im)
        elif kind == 1:
            n_groups, chunk, _ = gmlp_w_s[j].shape
            gd = gmlp_w_out[j].shape[0] // n_groups
            bs = jnp.broadcast_to(gmlp_b_s[j][:, :, None], (n_groups, chunk, gd))
            ws = (wm[0], wm[1].reshape(gmlp_w_s[j].shape), bs, wm[2])
            later = adaln_jobs([i + 1]) if i + 1 < depth else []
            xp, extra = _ffn(_gmlp_layer(xp, stp, m, 0, gm, *ws)[0], stp, m, 0, *ffn, jobs=nxt + later)
            wb = extra[:len(nxt)]
            for job, out in zip(later, extra[len(nxt):]):
                mods[job[4]] = mod_table(out)
            xs, _ = _ffn(_gmlp_layer(xs, sts, m, 0, gm, *ws)[0], sts_ffn, m, 0, *ffn)
        else:
            ws = (wm[0], jnp.pad(conv_w[j], ((0, SUBLANES - CONV_WIDTH), (0, 0))), wm[1])
            later = adaln_jobs([i + 1]) if i + 1 < depth else []
            xp, extra = _conv_ffn_layer(xp, stp, m, 0, gm, *ws, *ffn, jobs=nxt + later)
            wb = extra[:len(nxt)]
            for job, out in zip(later, extra[len(nxt):]):
                mods[job[4]] = mod_table(out)
            xs, _ = _conv_ffn_layer(xs, sts, m, 0, gm, *ws, *ffn)

    return (xp.reshape(x_prompt.shape), xs.reshape(x_sample.shape),
            jnp.stack(new_re, axis=1), jnp.stack(new_im, axis=1))
```

```python
import functools
import math

import jax
import jax.numpy as jnp
from jax import lax
from jax.experimental import pallas as pl
from jax.experimental.pallas import tpu as pltpu

EPS = 1e-6
N_MOD = 6
GRID_W = 64
SSM_GROUP = 16
SSM_CHUNK = 16
CONV_WIDTH = 3
SUBLANES = 8
LANES = 128
ROW_BLOCK = 512
FFN_ROW_BLOCK = 1024
SEQ_CHUNKS = 16
SEQ_PAIR = 2
CTX_ROWS = 4
VMEM_LIMIT = 56 * 1024 * 1024

F32 = jnp.float32
BF16 = jnp.bfloat16
NT = (((1,), (1,)), ((), ()))


def _cparams(n_axes=1, vmem=VMEM_LIMIT):
    return pltpu.CompilerParams(dimension_semantics=("arbitrary",) * n_axes, vmem_limit_bytes=vmem)


def _gelu(x):
    return 0.5 * x * (1.0 + jnp.tanh(math.sqrt(2.0 / math.pi) * (x + 0.044715 * (x * x * x))))


def _norm_mod(x, g, shift, scale):
    y = x * lax.rsqrt(jnp.mean(x * x, axis=-1, keepdims=True) + EPS)
    return (y * g) * (1.0 + scale) + shift


def _cmul(ar, ai, br, bi):
    return ar * br - ai * bi, ar * bi + ai * br


def _resident(shape):
    nd = len(shape)
    return pl.BlockSpec(shape, lambda *_: (0,) * nd, pipeline_mode=pl.Buffered(1))


def _side_plan(jobs, n_steps, step_of):
    kinds, ins, args, outs, shapes = [], [], [], [], []
    for job in jobs:
        if job[0] == "cast":
            _, stack, layer = job
            _, r, c = stack.shape
            blk = r // n_steps
            assert r % n_steps == 0 and blk % (2 * SUBLANES) == 0
            kinds.append(("cast", 1, None))
            ins.append(pl.BlockSpec((None, blk, c), lambda *ids, layer=layer: (layer, step_of(*ids), 0)))
            args.append(stack)
            outs.append(pl.BlockSpec((blk, c), lambda *ids: (step_of(*ids), 0)))
            shapes.append(jax.ShapeDtypeStruct((r, c), BF16))
        else:
            _, cond_t, w_mod, b_mod, layer, n_cond = job
            _, d, n = w_mod.shape
            tn = n // n_steps
            assert n % n_steps == 0 and tn % LANES == 0
            kinds.append(("adaln", 3, n_cond))
            ins += [_resident(cond_t.shape),
                    pl.BlockSpec((None, d, tn), lambda *ids, layer=layer: (layer, 0, step_of(*ids))),
                    pl.BlockSpec((None, 1, tn), lambda *ids, layer=layer: (layer, 0, step_of(*ids)))]
            args += [cond_t, w_mod, b_mod.reshape(b_mod.shape[0], 1, n)]
            outs.append(pl.BlockSpec((SUBLANES, tn), lambda *ids: (0, step_of(*ids))))
            shapes.append(jax.ShapeDtypeStruct((SUBLANES, n), F32))
    return tuple(kinds), ins, args, outs, shapes


def _side_counts(kinds):
    return sum(k[1] for k in kinds), len(kinds)


def _side_run(kinds, in_refs, out_refs):
    i = 0
    for (name, n_in, param), dst in zip(kinds, out_refs):
        if name == "cast":
            dst[...] = in_refs[i][...].astype(BF16)
        else:
            dst[...] = _adaln_rows(in_refs[i][...], in_refs[i + 1][...], param) + in_refs[i + 2][...]
        i += n_in


def _adaln_rows(ct, w, n_cond):
    s = ct * jax.nn.sigmoid(ct)
    rows = [jnp.sum(s[:, r:r + 1] * w, axis=0, keepdims=True) for r in range(n_cond)]
    rows += [jnp.zeros_like(rows[0])] * (SUBLANES - n_cond)
    return jnp.concatenate(rows, axis=0)


class _Stream:
    def __init__(self, n_chunks, n_batch, per_batch_cond, row_block=ROW_BLOCK):
        self.nc, self.nb = n_chunks, n_batch
        self.n_rows = n_chunks * n_batch * SSM_CHUNK
        self.per_batch_cond = per_batch_cond
        self.row_block = row_block
        self.row_chunks = row_block // SSM_CHUNK
        self.seq_blocks = n_chunks // SEQ_CHUNKS
        assert n_chunks % SEQ_CHUNKS == 0 and self.n_rows % row_block == 0 and n_batch % SEQ_PAIR == 0
        assert not per_batch_cond or n_chunks % self.row_chunks == 0

    def row_grid(self):
        return (self.nb, self.nc // self.row_chunks) if self.per_batch_cond else (self.n_rows // self.row_block,)

    def row_view(self, x):
        return x if self.per_batch_cond else x.reshape(self.n_rows, x.shape[-1])

    def row_spec(self, width):
        if self.per_batch_cond:
            return pl.BlockSpec((self.row_chunks, None, SSM_CHUNK, width), lambda b, j: (j, b, 0, 0))
        return pl.BlockSpec((self.row_block, width), lambda i: (i, 0))

    def row_shape(self, width):
        return (self.nc, self.nb, SSM_CHUNK, width) if self.per_batch_cond else (self.n_rows, width)

    def unview(self, x):
        return x.reshape(self.nc, self.nb, SSM_CHUNK, x.shape[-1])

    def nat_view(self, x):
        return x.reshape(self.nb, self.nc, SSM_CHUNK, x.shape[-1])

    def nat_spec(self, width):
        if self.per_batch_cond:
            return pl.BlockSpec((None, self.row_chunks, SSM_CHUNK, width), lambda b, j: (b, j, 0, 0))
        assert self.nb * SSM_CHUNK == self.row_block
        return pl.BlockSpec((self.nb, None, SSM_CHUNK, width), lambda i: (0, i, 0, 0))

    def pos_spec(self, width):
        return pl.BlockSpec((self.row_chunks, SSM_CHUNK, width), lambda b, j: (j, 0, 0))

    def seq_grid(self):
        return (self.nb // SEQ_PAIR, self.seq_blocks)

    def seq_spec(self, width):
        return pl.BlockSpec((SEQ_CHUNKS, SEQ_PAIR, SSM_CHUNK, width), lambda b, j: (j, b, 0, 0))

    def halo_specs(self, width):
        halves = SSM_CHUNK // SUBLANES
        shape = (None, SEQ_PAIR, None, SUBLANES, width)
        prev = pl.BlockSpec(shape, lambda b, j: (jnp.maximum(j * SEQ_CHUNKS - 1, 0), b, halves - 1, 0, 0))
        nxt = pl.BlockSpec(shape, lambda b, j: (jnp.minimum((j + 1) * SEQ_CHUNKS, self.nc - 1), b, 0, 0, 0))
        return prev, nxt

    def halo_view(self, x):
        return x.reshape(self.nc, self.nb, SSM_CHUNK // SUBLANES, SUBLANES, x.shape[-1])

    def mod_spec(self, layer, d):
        if self.per_batch_cond:
            return pl.BlockSpec((None, None, N_MOD, d), lambda b, j: (layer, CTX_ROWS + b, 0, 0))
        return pl.BlockSpec((None, None, N_MOD, d), lambda *_: (layer, 0, 0, 0))

    def seq_mod_spec(self, layer, d):
        if self.per_batch_cond:
            return pl.BlockSpec((None, SEQ_PAIR, N_MOD, d), lambda b, j: (layer, CTX_ROWS // SEQ_PAIR + b, 0, 0))
        return pl.BlockSpec((None, SEQ_PAIR, N_MOD, d), lambda *_: (layer, 0, 0, 0))

    def n_axes(self, seq=False):
        return 2 if (seq or self.per_batch_cond) else 1

    def side_plan(self, jobs, seq=False):
        if seq:
            return _side_plan(jobs, (self.nb // SEQ_PAIR) * self.seq_blocks, lambda b, j: b * self.seq_blocks + j)
        assert not jobs or not self.per_batch_cond
        return _side_plan(jobs, self.n_rows // self.row_block, lambda i: i)


def _grid_pos_embed(n_tokens, d):
    rows = n_tokens // GRID_W
    quarter = d // 4
    freq = 1.0 / (10000.0 ** (jnp.arange(quarter, dtype=F32) / quarter))
    ar = jnp.arange(rows, dtype=F32)[:, None] * freq
    ac = jnp.arange(GRID_W, dtype=F32)[:, None] * freq
    row_part = jnp.concatenate([jnp.sin(ar), jnp.cos(ar)], axis=-1)
    col_part = jnp.concatenate([jnp.sin(ac), jnp.cos(ac)], axis=-1)
    table = jnp.concatenate([jnp.broadcast_to(row_part[:, None], (rows, GRID_W, d // 2)),
                             jnp.broadcast_to(col_part[None], (rows, GRID_W, d // 2))], axis=-1)
    return table.reshape(rows * GRID_W, d)


def _adaln_kernel(ct_ref, w_ref, b_ref, o_ref, *, n_cond):
    o_ref[...] = _adaln_rows(ct_ref[...], w_ref[...], n_cond) + b_ref[...]


def _adaln(cond_t, w_mod, b_mod, n_cond, layers, tn=1024):
    depth, d, n = w_mod.shape
    l0 = layers[0]
    assert list(layers) == list(range(l0, l0 + len(layers)))
    return pl.pallas_call(
        functools.partial(_adaln_kernel, n_cond=n_cond),
        grid=(len(layers), n // tn),
        in_specs=[
            pl.BlockSpec((d, SUBLANES), lambda l, j: (0, 0)),
            pl.BlockSpec((None, d, tn), lambda l, j: (l + l0, 0, j)),
            pl.BlockSpec((None, 1, tn), lambda l, j: (l + l0, 0, j)),
        ],
        out_specs=pl.BlockSpec((None, SUBLANES, tn), lambda l, j: (l, 0, j)),
        out_shape=jax.ShapeDtypeStruct((len(layers), SUBLANES, n), F32),
        compiler_params=_cparams(2), name="adaln",
    )(cond_t, w_mod, b_mod.reshape(depth, 1, n))


def _rows(ref):
    return ref[...].reshape(-1, ref.shape[-1])


def _rms(x, g):
    return (x * lax.rsqrt(jnp.mean(x * x, axis=-1, keepdims=True) + EPS)) * g


def _ffn_blocks(xs, mods, g, w1_ref, w2_ref, th, g_final=None):
    hn = jnp.concatenate([_norm_mod(x, g, m[3:4], m[4:5]).astype(BF16) for x, m in zip(xs, mods)], axis=0)
    acc = jnp.zeros((hn.shape[0], w2_ref.shape[1]), F32)
    for c in range(w1_ref.shape[1] // th):
        h1 = jnp.dot(hn, w1_ref[:, c * th:(c + 1) * th], preferred_element_type=F32)
        h1 = jnp.square(jnp.maximum(h1, 0.0)).astype(BF16)
        acc = acc + jnp.dot(h1, w2_ref[c * th:(c + 1) * th, :], preferred_element_type=F32)
    outs, r0 = [], 0
    for x, m in zip(xs, mods):
        out = x + m[5:6] * acc[r0:r0 + x.shape[0]]
        outs.append(out if g_final is None else _rms(out, g_final))
        r0 += x.shape[0]
    return outs


def _ffn_kernel(x_ref, mod_ref, g_ref, w1_ref, w2_ref, *rest, th, side):
    n_in, _ = _side_counts(side)
    o_ref = rest[n_in]
    out, = _ffn_blocks([_rows(x_ref)], [mod_ref[...]], g_ref[...], w1_ref, w2_ref, th)
    o_ref[...] = out.reshape(o_ref.shape)
    _side_run(side, rest[:n_in], rest[n_in + 1:])


def _ffn(x, st, mods, layer, g, w1, w2, jobs=(), th=1024):
    d = x.shape[-1]
    side, s_in, s_args, s_out, s_shapes = st.side_plan(jobs)
    out, *extra = pl.pallas_call(
        functools.partial(_ffn_kernel, th=th, side=side),
        grid=st.row_grid(),
        in_specs=[st.row_spec(d), st.mod_spec(layer, d), _resident((1, d)), _resident(w1.shape), _resident(w2.shape)]
        + s_in,
        out_specs=[st.row_spec(d)] + s_out,
        out_shape=[jax.ShapeDtypeStruct(st.row_shape(d), F32)] + s_shapes,
        compiler_params=_cparams(st.n_axes()), name="ffn",
    )(st.row_view(x), mods, g, w1, w2, *s_args)
    return st.unview(out), extra


def _x_operands(x, st, pos):
    d = x.shape[-1]
    if x.ndim == 4:
        return [st.row_spec(d)], [st.row_view(x)]
    specs, args = [st.nat_spec(d)], [st.nat_view(x)]
    if pos is not None:
        specs.append(st.pos_spec(d))
        args.append(pos.reshape(st.nc, SSM_CHUNK, d))
    return specs, args


PERM_SPLIT = 2


def _t_major_perm(n_batch):
    rows = ROW_BLOCK // PERM_SPLIT
    n_chunks = rows // (n_batch * SSM_CHUNK)
    assert n_chunks * n_batch == 2 * SUBLANES
    src = jnp.arange(rows)
    b, c, t = src // (n_chunks * SSM_CHUNK), (src // SSM_CHUNK) % n_chunks, src % SSM_CHUNK
    dst = (t * n_chunks + c) * n_batch + b
    return (jnp.arange(rows)[:, None] == dst[None, :]).astype(BF16)


def _ssm_in_kernel(x_ref, *rest, n_batch, natural, n_sub):
    *pos_ref, mod_ref, g_ref, perm_ref, w_ref, o_ref = rest
    g = g_ref[...]
    d = x_ref.shape[-1]
    n_t = o_ref.shape[0]
    group = o_ref.shape[1] // n_sub
    piece = group // PERM_SPLIT
    for h in range(n_sub):
        parts = []
        for s in range(PERM_SPLIT):
            if n_batch == 1:
                mod = mod_ref[...]
                xs = x_ref[s * piece:(s + 1) * piece, h] if natural else x_ref[h, s * piece:(s + 1) * piece]
                hn = _norm_mod(xs.reshape(-1, d), g, mod[0:1], mod[1:2]).astype(BF16)
            else:
                cpp = piece // n_batch
                cs = slice((h * PERM_SPLIT + s) * cpp, (h * PERM_SPLIT + s + 1) * cpp)
                rows = []
                for b in range(n_batch):
                    xb = (x_ref[b, cs] if natural else x_ref[cs, b]).reshape(-1, d)
                    if pos_ref:
                        xb = xb + pos_ref[0][cs].reshape(-1, d)
                    mod = mod_ref[b]
                    rows.append(_norm_mod(xb, g, mod[0:1], mod[1:2]).astype(BF16))
                hn = jnp.concatenate(rows, axis=0)
            parts.append(jnp.dot(perm_ref[...], hn, preferred_element_type=F32).astype(BF16))
        hn = jnp.concatenate([p[t * piece:(t + 1) * piece] for t in range(n_t) for p in parts], axis=0)
        u = jnp.dot(hn, w_ref[...], preferred_element_type=F32)
        o_ref[:, h * group:(h + 1) * group, :] = u.astype(BF16).reshape(n_t, group, o_ref.shape[2])


def _ssm_in(x, st, mods, layer, g, w, pos=None, n_sub=2):
    d, width = x.shape[-1], w.shape[1]
    natural = x.ndim != 4
    group = ROW_BLOCK // SSM_CHUNK
    if not st.per_batch_cond:
        assert st.nb == group and st.nc % n_sub == 0
        cps, n_batch = n_sub, 1
        mod_spec = st.mod_spec(layer, d)
    else:
        assert group % st.nb == 0 and CTX_ROWS % st.nb == 0
        cps, n_batch = n_sub * (group // st.nb), st.nb
        assert st.nc % cps == 0
        mod_spec = pl.BlockSpec((None, st.nb, N_MOD, d), lambda i: (layer, CTX_ROWS // st.nb, 0, 0))
    if natural:
        x_specs, x_args = [pl.BlockSpec((st.nb, cps, SSM_CHUNK, d), lambda i: (0, i, 0, 0))], [st.nat_view(x)]
        if pos is not None:
            x_specs.append(pl.BlockSpec((cps, SSM_CHUNK, d), lambda i: (i, 0, 0)))
            x_args.append(pos.reshape(st.nc, SSM_CHUNK, d))
    else:
        x_specs, x_args = [pl.BlockSpec((cps, st.nb, SSM_CHUNK, d), lambda i: (i, 0, 0, 0))], [x]
    perm = _t_major_perm(min(st.nb, group // PERM_SPLIT))
    return pl.pallas_call(
        functools.partial(_ssm_in_kernel, n_batch=n_batch, natural=natural, n_sub=n_sub),
        grid=(st.nc // cps,),
        in_specs=x_specs + [mod_spec, _resident((1, d)), _resident(perm.shape), _resident(w.shape)],
        out_specs=pl.BlockSpec((SSM_CHUNK, n_sub * group, width), lambda i: (0, i, 0)),
        out_shape=jax.ShapeDtypeStruct((SSM_CHUNK, st.nc * st.nb, width), BF16),
        compiler_params=_cparams(1), name="ssm_in",
    )(*x_args, mods, g, perm, w)


def _ssm_out_ffn_kernel(x_ref, *rest, has_pos, has_final, th, side):
    pos_ref = rest[0] if has_pos else None
    y_ref, mod_ref, wout_ref, gf_ref, w1_ref, w2_ref = rest[has_pos:has_pos + 6]
    n_in = has_pos + 6 + has_final
    n_side, _ = _side_counts(side)
    g_final = rest[n_in - 1][...] if has_final else None
    o_ref = rest[n_in + n_side]
    x = _rows(x_ref) + _rows(pos_ref) if has_pos else _rows(x_ref)
    y = _rows(y_ref)
    mod = mod_ref[...]
    d = x.shape[-1]
    half = x.shape[0] // 2
    mids = []
    for k in range(2):
        rows = slice(k * half, (k + 1) * half)
        ag = jnp.dot(_gelu(y[rows]).astype(BF16), wout_ref[...], preferred_element_type=F32)
        mids.append(x[rows] + mod[2:3] * (ag[:, :d] * jax.nn.sigmoid(ag[:, d:])))
    outs = _ffn_blocks(mids, [mod, mod], gf_ref[...], w1_ref, w2_ref, th, g_final)
    o_ref[...] = jnp.concatenate(outs, axis=0).reshape(o_ref.shape)
    _side_run(side, rest[n_in:n_in + n_side], rest[n_in + n_side + 1:])


def _ssm_out_ffn(x, y, st, mods, layer, w_out, gf, w1, w2, pos=None, g_final=None, jobs=(), th=1024):
    d = x.shape[-1]
    x_specs, x_args = _x_operands(x, st, pos)
    last = g_final is not None
    side, s_in, s_args, s_out, s_shapes = st.side_plan(jobs)
    out, *extra = pl.pallas_call(
        functools.partial(_ssm_out_ffn_kernel, has_pos=len(x_args) == 2, has_final=last, th=th, side=side),
        grid=st.row_grid(),
        in_specs=x_specs + [st.row_spec(d), st.mod_spec(layer, d), _resident(w_out.shape), _resident((1, d)),
                            _resident(w1.shape), _resident(w2.shape)] + ([_resident((1, d))] if last else []) + s_in,
        out_specs=[st.nat_spec(d) if last else st.row_spec(d)] + s_out,
        out_shape=[jax.ShapeDtypeStruct((st.nb, st.nc, SSM_CHUNK, d) if last else st.row_shape(d), F32)] + s_shapes,
        compiler_params=_cparams(st.n_axes()), name="ssm_out_ffn",
    )(*x_args, st.row_view(st.unview(y)), mods, w_out, gf, w1, w2, *([g_final] if last else []), *s_args)
    return (out if last else st.unview(out)), extra


def _ssm_operators(prow, bt, ct):
    nk = SSM_CHUNK
    half = prow.shape[1] // 2
    lr, li = prow[0:1], prow[1:2]
    dt = jnp.exp(prow[2:3])
    mag = jnp.exp(lr * dt)
    ar, ai = mag * jnp.cos(li * dt), mag * jnp.sin(li * dt)
    den = lr * lr + li * li
    nr, ni = _cmul(ar - 1.0, ai, lr, -li)
    bbr, bbi = _cmul(nr / den, ni / den, bt[0:nk], bt[nk:2 * nk])
    ctr, cti = ct[0:nk], ct[nk:2 * nk]

    pr, pi = [jnp.ones_like(ar)], [jnp.zeros_like(ar)]
    for _ in range(nk):
        r, i = _cmul(pr[-1], pi[-1], ar, ai)
        pr.append(r)
        pi.append(i)
    is_fwd = lax.broadcasted_iota(jnp.int32, (1, 2 * half), 1) < half

    def pw(kf, kb):
        return jnp.where(is_fwd, pr[kf], pr[kb]), jnp.where(is_fwd, pi[kf], pi[kb])

    wp_r, wp_i, ca_r, ca_i, wc_r, wc_i = [], [], [], [], [], []
    for t in range(nk):
        r, i = _cmul(bbr, bbi, *pw(nk - 1 - t, t))
        wp_r.append(r)
        wp_i.append(i)
        r, i = _cmul(ctr, cti, *pw(t, nk - 1 - t))
        ca_r.append(r)
        ca_i.append(i)
        r, i = _cmul(ctr, cti, *pw(t + 1, nk - t))
        wc_r.append(r)
        wc_i.append(-i)
    cat = jnp.concatenate
    wp = cat([cat(wp_r, 0), cat(wp_i, 0)], axis=1).astype(BF16)
    ca = cat([cat(ca_r, 0), cat(ca_i, 0)], axis=1)
    wct = cat([cat(wc_r, 0), cat(wc_i, 0)], axis=1).astype(BF16)

    zero = jnp.zeros_like(bbr)
    lhs = cat([cat([jnp.where(is_fwd, bbr, zero), jnp.where(is_fwd, -bbi, zero)], 1),
               cat([jnp.where(is_fwd, zero, bbr), jnp.where(is_fwd, zero, -bbi)], 1)], 0)
    kk = lax.dot_general(lhs, ca, NT, precision=lax.Precision.HIGHEST, preferred_element_type=F32)
    width = nk * SSM_GROUP
    lane = lax.broadcasted_iota(jnp.int32, (nk, width), 1)
    row = lax.broadcasted_iota(jnp.int32, (nk, width), 0)
    d_lanes = cat([prow[3:4], jnp.zeros_like(prow[3:4])], axis=1)
    kf = kk[0:nk] + jnp.where(lane == row, d_lanes, 0.0)
    return kf, kk[nk:2 * nk], wp, wct, pr[nk], pi[nk]


def _toeplitz_shifts():
    width = SSM_CHUNK * SSM_GROUP
    src = jnp.arange(width)[:, None]
    dst = jnp.arange(width)[None, :]
    fwd = [dst == src + SSM_GROUP * s for s in range(SSM_CHUNK)]
    bwd = [src == dst + SSM_GROUP * (SSM_CHUNK - 1 - s) for s in range(SSM_CHUNK)]
    return jnp.concatenate(fwd, axis=1).astype(BF16), jnp.concatenate(bwd, axis=1).astype(BF16)


def _ssm_core_kernel(prow_ref, bt_ref, ct_ref, h0r_ref, h0i_ref, shf_ref, shb_ref, up_ref, us_ref, *rest,
                     p_batch, s_batch, side):
    n_in, n_out = _side_counts(side)
    yp_ref, ys_ref, fr_ref, fi_ref = rest[n_in:n_in + 4]
    atp_scr, ats_scr, tg_scr, wp_scr, wct_scr, u_scr, p_scr, s_scr, q_scr = rest[n_in + 4 + n_out:]
    _side_run(side, rest[:n_in], rest[n_in + 4:n_in + 4 + n_out])
    nk = SSM_CHUNK
    n_groups = LANES // SSM_GROUP
    cb_p, cb_s = atp_scr.shape[2], ats_scr.shape[2]
    w2 = prow_ref.shape[2]
    width = nk * SSM_GROUP
    decay, kfs, kbs = [], [], []
    for g in range(n_groups):
        kf, kb, wp, wct, a_r, a_i = _ssm_operators(prow_ref[g], bt_ref[g], ct_ref[g])
        kfs.append(kf)
        kbs.append(kb)
        wp_scr[g] = wp
        wct_scr[g] = wct
        decay.append((a_r, a_i))
    tz = (jnp.dot(jnp.concatenate(kfs, axis=0).astype(BF16), shf_ref[...], preferred_element_type=F32)
          + jnp.dot(jnp.concatenate(kbs, axis=0).astype(BF16), shb_ref[...], preferred_element_type=F32))
    for g in range(n_groups):
        for s in range(nk):
            tg_scr[g, s * SSM_GROUP:(s + 1) * SSM_GROUP, :] = (
                tz[g * SSM_GROUP:(g + 1) * SSM_GROUP, s * width:(s + 1) * width].astype(BF16))

    is_fwd = lax.broadcasted_iota(jnp.int32, (1, w2), 1) < (w2 // 2)

    def scan(n_tiles, rows, src_scr, carries, mults):
        for j in range(n_tiles):
            lo, hi = j * rows, (n_tiles - 1 - j) * rows
            for g in range(n_groups):
                c_r, c_i = carries[g]
                for c, l0 in ((c_r, 0), (c_i, w2)):
                    if lo < hi:
                        s_scr[g, lo:lo + rows, l0:l0 + w2] = c
                        s_scr[g, hi:hi + rows, l0:l0 + w2] = c
                    else:
                        s_scr[g, lo:lo + rows, l0:l0 + w2] = jnp.where(is_fwd, c, s_scr[g, lo:lo + rows, l0:l0 + w2])
                        s_scr[g, hi:hi + rows, l0:l0 + w2] = jnp.where(is_fwd, s_scr[g, hi:hi + rows, l0:l0 + w2], c)
                s_r = jnp.where(is_fwd, src_scr[g, lo:lo + rows, 0:w2], src_scr[g, hi:hi + rows, 0:w2])
                s_i = jnp.where(is_fwd, src_scr[g, lo:lo + rows, w2:2 * w2], src_scr[g, hi:hi + rows, w2:2 * w2])
                n_r, n_i = _cmul(*mults[g], c_r, c_i)
                carries[g] = (n_r + s_r, n_i + s_i)
        return carries

    def prompt_carry(cb):
        z = jnp.zeros((p_batch, w2), F32)
        return scan(cb // p_batch, p_batch, p_scr, [(z, z)] * n_groups, decay)

    def sample_carry(cb):
        assert 2 * s_batch == SUBLANES
        row = lax.broadcasted_iota(jnp.int32, (SUBLANES, w2), 0)
        keep = jnp.where(is_fwd, 1, 0) == jnp.where(row < s_batch, 1, 0)
        carries, mults = [], []
        for g in range(n_groups):
            a_r, a_i = decay[g]
            p_r, p_i = p_scr[g, 0:cb, 0:w2], p_scr[g, 0:cb, w2:2 * w2]
            ap_r, ap_i = _cmul(a_r, a_i, p_r, p_i)
            q_scr[g, 0:cb, 0:w2] = jnp.where(is_fwd, pltpu.roll(p_r, cb - s_batch, 0), pltpu.roll(p_r, s_batch, 0)) + ap_r
            q_scr[g, 0:cb, w2:2 * w2] = jnp.where(is_fwd, pltpu.roll(p_i, cb - s_batch, 0),
                                                  pltpu.roll(p_i, s_batch, 0)) + ap_i
            h_r, h_i = h0r_ref[g], h0i_ref[g]
            e_r = pltpu.roll(jnp.where(is_fwd, p_r[0:SUBLANES], p_r[cb - SUBLANES:cb]), s_batch, 0)
            e_i = pltpu.roll(jnp.where(is_fwd, p_i[0:SUBLANES], p_i[cb - SUBLANES:cb]), s_batch, 0)
            ah_r, ah_i = _cmul(a_r, a_i, h_r, h_i)
            carries.append((jnp.where(keep, h_r, ah_r + e_r), jnp.where(keep, h_i, ah_i + e_i)))
            mults.append(_cmul(a_r, a_i, a_r, a_i))
        return scan(cb // SUBLANES, SUBLANES, q_scr, carries, mults)

    def mix(u_ref, y_ref, at_scr, cb, carry_fn):
        for t in range(nk):
            at_scr[t] = u_ref[t].T
        for g in range(n_groups):
            a = at_scr[:, g * SSM_GROUP:(g + 1) * SSM_GROUP, :].reshape(nk * SSM_GROUP, cb)
            u = a.T
            u_scr[g, 0:cb, :] = u
            p_scr[g, 0:cb, :] = jnp.dot(u, wp_scr[g], preferred_element_type=F32)
        finals = carry_fn(cb)
        for g in range(n_groups):
            s = s_scr[g, 0:cb, :].astype(BF16)
            y = (jnp.dot(u_scr[g, 0:cb, :], tg_scr[g], preferred_element_type=F32)
                 + lax.dot_general(s, wct_scr[g], NT, preferred_element_type=F32))
            at_scr[:, g * SSM_GROUP:(g + 1) * SSM_GROUP, :] = y.astype(BF16).T.reshape(nk, SSM_GROUP, cb)
        for t in range(nk):
            y_ref[pl.ds(t, cb, stride=nk), :] = at_scr[t].T.astype(F32)
        return finals

    finals = mix(up_ref, yp_ref, atp_scr, cb_p, prompt_carry)
    for g in range(n_groups):
        fr_ref[g] = finals[g][0]
        fi_ref[g] = finals[g][1]
    mix(us_ref, ys_ref, ats_scr, cb_s, sample_carry)


def _ssm_core(prow, bt, ct, h0r, h0i, up, us, p_batch, s_batch, jobs=()):
    groups, _, w2 = prow.shape
    gpb = LANES // SSM_GROUP
    cb_p, cb_s = up.shape[1], us.shape[1]
    np_rows, ns_rows = cb_p * SSM_CHUNK, cb_s * SSM_CHUNK
    cb_max = max(cb_p, cb_s)
    width = SSM_CHUNK * SSM_GROUP
    shf, shb = _toeplitz_shifts()

    def gspec(r, c):
        return pl.BlockSpec((gpb, r, c), lambda o: (o, 0, 0))

    def lane_spec(n):
        return pl.BlockSpec((n, LANES), lambda o: (0, o))

    def in_spec(cb):
        return pl.BlockSpec((SSM_CHUNK, cb, LANES), lambda o: (0, 0, o))

    side, c_in, c_args, c_out, c_shapes = _side_plan(jobs, groups // gpb, lambda o: o)
    yp, ys, f_re, f_im, *cast = pl.pallas_call(
        functools.partial(_ssm_core_kernel, p_batch=p_batch, s_batch=s_batch, side=side),
        grid=(groups // gpb,),
        in_specs=[gspec(SUBLANES, w2), gspec(2 * SSM_GROUP, w2), gspec(2 * SSM_GROUP, w2),
                  gspec(SUBLANES, w2), gspec(SUBLANES, w2), _resident(shf.shape), _resident(shb.shape),
                  in_spec(cb_p), in_spec(cb_s)] + c_in,
        out_specs=[lane_spec(np_rows), lane_spec(ns_rows), gspec(p_batch, w2), gspec(p_batch, w2)] + c_out,
        out_shape=[jax.ShapeDtypeStruct((np_rows, up.shape[2]), F32), jax.ShapeDtypeStruct((ns_rows, us.shape[2]), F32),
                   jax.ShapeDtypeStruct((groups, p_batch, w2), F32),
                   jax.ShapeDtypeStruct((groups, p_batch, w2), F32)] + c_shapes,
        scratch_shapes=[pltpu.VMEM((SSM_CHUNK, LANES, cb_p), BF16), pltpu.VMEM((SSM_CHUNK, LANES, cb_s), BF16)]
        + [pltpu.VMEM((gpb, width, width), BF16)] * 3 + [pltpu.VMEM((gpb, cb_max, width), BF16)]
        + [pltpu.VMEM((gpb, cb_max, 2 * w2), F32)] * 2 + [pltpu.VMEM((gpb, cb_s, 2 * w2), F32)],
        compiler_params=_cparams(), name="ssm_core",
    )(prow, bt, ct, h0r, h0i, shf, shb, up, us, *c_args)
    return yp, ys, f_re, f_im, cast


def _ssm_layer(xp, xs, stp, sts, mods, layer, g_mix, w_in, lam_re, lam_im, log_dt, b_re, b_im, c_re, c_im, d_skip,
               w_out, h0_re, h0_im, ffn, pos=None, g_final=None, jobs=(), late=()):
    width = w_in.shape[1]
    groups = width // SSM_GROUP
    n_state = lam_re.shape[-1]
    up = _ssm_in(xp, stp, mods, layer, g_mix, w_in)
    us = _ssm_in(xs, sts, mods, layer, g_mix, w_in, pos)

    def lanes_dir_state(v):
        return v.transpose(1, 0, 2).reshape(groups, 2 * n_state)

    d_rows = jnp.pad(d_skip.reshape(groups, SSM_GROUP), ((0, 0), (0, 2 * n_state - SSM_GROUP)))
    prow = jnp.stack([lanes_dir_state(lam_re), lanes_dir_state(lam_im),
                      lanes_dir_state(jnp.broadcast_to(log_dt[..., None], lam_re.shape)), d_rows], axis=1)
    prow = jnp.pad(prow, ((0, 0), (0, SUBLANES - 4), (0, 0)))
    bt = jnp.concatenate([b_re.transpose(1, 3, 0, 2).reshape(groups, SSM_GROUP, 2 * n_state),
                          b_im.transpose(1, 3, 0, 2).reshape(groups, SSM_GROUP, 2 * n_state)], axis=1)
    ct = jnp.concatenate([c_re.transpose(1, 2, 0, 3).reshape(groups, SSM_GROUP, 2 * n_state),
                          c_im.transpose(1, 2, 0, 3).reshape(groups, SSM_GROUP, 2 * n_state)], axis=1)

    def h0_rows(h):
        h = h.transpose(2, 0, 1, 3).reshape(groups, sts.nb, 2 * n_state)
        return jnp.concatenate([h] * (SUBLANES // sts.nb), axis=1)

    yp, ys, f_re, f_im, conv = _ssm_core(prow, bt, ct, h0_rows(h0_re), h0_rows(h0_im), up, us, stp.nb, sts.nb, late)
    if late:
        w_out, ffn = conv[0], (ffn[0], conv[1], conv[2])
    xp, cast = _ssm_out_ffn(xp, yp, stp, mods, layer, w_out, *ffn, g_final=g_final, jobs=jobs)
    xs, _ = _ssm_out_ffn(xs, ys, sts, mods, layer, w_out, *ffn, pos=pos, g_final=g_final)

    def final(f):
        return f.reshape(groups, stp.nb, 2, n_state).transpose(1, 2, 0, 3)

    return xp, xs, final(f_re), final(f_im), cast


def _seq_rows(ref, k):
    return ref[:, k].reshape(-1, ref.shape[-1])


def _gmlp_kernel(x_ref, mod_ref, g_ref, win_ref, ws_ref, bs_ref, wout_ref, *rest, side):
    n_in, n_out = _side_counts(side)
    o_ref, t_scr = rest[n_in], rest[n_in + 1 + n_out]
    _side_run(side, rest[:n_in], rest[n_in + 1:n_in + 1 + n_out])
    n_groups, chunk, gd = bs_ref.shape
    xs, us, vns = [], [], []
    for k in range(SEQ_PAIR):
        x = _seq_rows(x_ref, k)
        mod = mod_ref[k]
        hn = _norm_mod(x, g_ref[...], mod[0:1], mod[1:2]).astype(BF16)
        z = _gelu(jnp.dot(hn, win_ref[...], preferred_element_type=F32))
        wdt = z.shape[1] // 2
        u, v = z[:, :wdt], z[:, wdt:]
        vc = v - jnp.mean(v, axis=-1, keepdims=True)
        xs.append(x)
        us.append(u)
        vns.append((vc * lax.rsqrt(jnp.mean(vc * vc, axis=-1, keepdims=True) + EPS)).astype(BF16))
    n_chunks = xs[0].shape[0] // chunk
    for g in range(n_groups):
        cols = slice(g * gd, (g + 1) * gd)
        rhs = jnp.concatenate([vn[c * chunk:(c + 1) * chunk, cols] for vn in vns for c in range(n_chunks)], axis=1)
        s = jnp.dot(ws_ref[g], rhs, preferred_element_type=F32)
        for k in range(SEQ_PAIR):
            for c in range(n_chunks):
                rows = slice(c * chunk, (c + 1) * chunk)
                i = k * n_chunks + c
                t_scr[k, rows, cols] = (us[k][rows, cols] * (s[:, i * gd:(i + 1) * gd] + bs_ref[g])).astype(BF16)
    for k in range(SEQ_PAIR):
        out = jnp.dot(t_scr[k], wout_ref[...], preferred_element_type=F32)
        o_ref[:, k] = (xs[k] + mod_ref[k][2:3] * out).reshape(o_ref.shape[0], o_ref.shape[2], o_ref.shape[3])


def _gmlp_layer(x, st, mods, layer, g_mix, w_in, w_s, bs, w_out, jobs=()):
    d = x.shape[-1]
    side, s_in, s_args, s_out, s_shapes = st.side_plan(jobs, seq=True)
    out, *extra = pl.pallas_call(
        functools.partial(_gmlp_kernel, side=side),
        grid=st.seq_grid(),
        in_specs=[st.seq_spec(d), st.seq_mod_spec(layer, d), _resident((1, d)), _resident(w_in.shape),
                  _resident(w_s.shape), _resident(bs.shape), _resident(w_out.shape)] + s_in,
        out_specs=[st.seq_spec(d)] + s_out,
        out_shape=[jax.ShapeDtypeStruct(x.shape, F32)] + s_shapes,
        scratch_shapes=[pltpu.VMEM((SEQ_PAIR, SEQ_CHUNKS * SSM_CHUNK, w_out.shape[0]), BF16)],
        compiler_params=_cparams(2), name="gmlp",
    )(x, mods, g_mix, w_in, w_s, bs, w_out, *s_args)
    return out, extra


def _conv_ffn_kernel(x_ref, xprev_ref, xnext_ref, mod_ref, g_ref, win_ref, cw_ref, wout_ref, gf_ref, w1_ref, w2_ref,
                     *rest, seq_blocks, th, side):
    n_in, _ = _side_counts(side)
    o_ref = rest[n_in]
    j = pl.program_id(1)
    g = g_ref[...]
    cw = cw_ref[...]
    d = x_ref.shape[-1]
    xs = [_seq_rows(x_ref, k) for k in range(SEQ_PAIR)]
    mods = [mod_ref[k] for k in range(SEQ_PAIR)]
    if seq_blocks > 1:
        hh = jnp.concatenate([_norm_mod(jnp.concatenate([xprev_ref[k], xnext_ref[k]], axis=0), g, m[0:1], m[1:2])
                              for k, m in enumerate(mods)], axis=0).astype(BF16)
        zh = jnp.dot(hh, win_ref[:, d:], preferred_element_type=F32)
        th_all = zh[:, :d] * zh[:, d:]
    else:
        th_all = jnp.zeros((2 * SUBLANES * SEQ_PAIR, d), F32)
    mids = []
    for k, (x, mod) in enumerate(zip(xs, mods)):
        tm = x.shape[0]
        hn = _norm_mod(x, g, mod[0:1], mod[1:2]).astype(BF16)
        z = jnp.dot(hn, win_ref[...], preferred_element_type=F32)
        gb, t = z[:, :d], z[:, d:2 * d] * z[:, 2 * d:]
        r0 = 2 * SUBLANES * k
        t_prev = jnp.where(j > 0, th_all[r0 + SUBLANES - 1:r0 + SUBLANES], 0.0)
        t_next = jnp.where(j < seq_blocks - 1, th_all[r0 + SUBLANES:r0 + SUBLANES + 1], 0.0)
        row = lax.broadcasted_iota(jnp.int32, (tm, 1), 0)
        up = jnp.where(row == 0, t_prev, pltpu.roll(t, 1, 0))
        dn = jnp.where(row == tm - 1, t_next, pltpu.roll(t, tm - 1, 0))
        y = cw[0:1] * up + cw[1:2] * t + cw[2:3] * dn
        out = jnp.dot((gb * y).astype(BF16), wout_ref[...], preferred_element_type=F32)
        mids.append(x + mod[2:3] * out)
    outs = _ffn_blocks(mids, mods, gf_ref[...], w1_ref, w2_ref, th)
    for k, out in enumerate(outs):
        o_ref[:, k] = out.reshape(o_ref.shape[0], o_ref.shape[2], o_ref.shape[3])
    _side_run(side, rest[:n_in], rest[n_in + 1:])


def _conv_ffn_layer(x, st, mods, layer, g_mix, w_in, cw, w_out, gf, w1, w2, jobs=(), th=1024):
    d = x.shape[-1]
    prev, nxt = st.halo_specs(d)
    xh = st.halo_view(x)
    side, c_in, c_args, c_out, c_shapes = st.side_plan(jobs, seq=True)
    out, *cast = pl.pallas_call(
        functools.partial(_conv_ffn_kernel, seq_blocks=st.seq_blocks, th=th, side=side),
        grid=st.seq_grid(),
        in_specs=[st.seq_spec(d), prev, nxt, st.seq_mod_spec(layer, d), _resident((1, d)), _resident(w_in.shape),
                  _resident(cw.shape), _resident(w_out.shape), _resident((1, d)), _resident(w1.shape),
                  _resident(w2.shape)] + c_in,
        out_specs=[st.seq_spec(d)] + c_out,
        out_shape=[jax.ShapeDtypeStruct(x.shape, F32)] + c_shapes,
        compiler_params=_cparams(2), name="conv_ffn",
    )(x, xh, xh, mods, g_mix, w_in, cw, w_out, gf, w1, w2, *c_args)
    return out, cast


def kernel(x_prompt, x_sample, state_ssm_re, state_ssm_im, c, c_ctx, w_mod, b_mod, g_mix, g_ffn, ffn_w1, ffn_w2, ssm_w_in, ssm_lam_re, ssm_lam_im, ssm_log_dt, ssm_b_re, ssm_b_im, ssm_c_re, ssm_c_im, ssm_d, ssm_w_out, gmlp_w_in, gmlp_w_s, gmlp_b_s, gmlp_w_out, conv_w_in, conv_w, conv_w_out, g_final):
    n_prompt, len_prompt, d = x_prompt.shape
    n_sample, len_sample, _ = x_sample.shape
    depth = w_mod.shape[0]
    n_mixers = 3
    assert 1 + n_sample <= SUBLANES
    stp = _Stream(len_prompt // SSM_CHUNK, n_prompt, per_batch_cond=False)
    sts = _Stream(len_sample // SSM_CHUNK, n_sample, per_batch_cond=True)
    sts_ffn = _Stream(len_sample // SSM_CHUNK, n_sample, per_batch_cond=True, row_block=FFN_ROW_BLOCK)

    xp, xs = x_prompt, x_sample
    pos = _grid_pos_embed(len_sample, d)

    cond_t = jnp.concatenate([c_ctx[None, :], c, jnp.zeros((SUBLANES - 1 - n_sample, d), F32)], axis=0).T
    n_cond = 1 + n_sample
    table = jnp.array([0] * CTX_ROWS + list(range(1, n_cond)))

    def mod_table(m):
        return m[table].reshape(1, table.shape[0], N_MOD, d)

    assert (depth - 1) % n_mixers == 0, "the last layer's kernel writes the natural-order outputs"

    def layer_weights(i):
        kind, j = i % n_mixers, i // n_mixers
        mixer = {0: [(ssm_w_in, j), (ssm_w_out, j)],
                 1: [(gmlp_w_in, j), (gmlp_w_s.reshape(gmlp_w_s.shape[0], -1, gmlp_w_s.shape[-1]), j), (gmlp_w_out, j)],
                 2: [(conv_w_in, j), (conv_w_out, j)]}[kind]
        return [("cast", stack, layer) for stack, layer in mixer + [(ffn_w1, i), (ffn_w2, i)]]

    def adaln_jobs(layers):
        return [("adaln", cond_t, w_mod, b_mod, l, n_cond) for l in layers]

    first = layer_weights(0)
    wb = [first[0][1][first[0][2]].astype(BF16)] + [None] * (len(first) - 1)
    mods = {0: mod_table(_adaln(cond_t, w_mod, b_mod, n_cond, [0])[0])}
    new_re, new_im = [], []
    for i in range(depth):
        kind, j = i % n_mixers, i // n_mixers
        gm = g_mix[i].reshape(1, d)
        nxt = layer_weights(i + 1) if i + 1 < depth else []
        *wm, w1, w2 = wb
        ffn = (g_ffn[i].reshape(1, d), w1, w2)
        m = mods[i]
        if kind == 0:
            later = adaln_jobs([i + 1]) if i + 1 < depth else []
            xp, xs, f_re, f_im, extra = _ssm_layer(
                xp, xs, stp, sts, m, 0, gm, wm[0], ssm_lam_re[j], ssm_lam_im[j], ssm_log_dt[j],
                ssm_b_re[j], ssm_b_im[j], ssm_c_re[j], ssm_c_im[j], ssm_d[j], wm[1],
                state_ssm_re[:, j], state_ssm_im[:, j], ffn, pos if i == 0 else None,
                g_final.reshape(1, d) if i == depth - 1 else None, nxt + later, first[1:] if i == 0 else ())
            wb = extra[:len(nxt)]
            for job, out in zip(later, extra[len(nxt):]):
                mods[job[4]] = mod_table(out)
            new_re.append(f_re)
            new_im.append(f_im)
        elif kind == 1:
            n_groups, chunk, _ = gmlp_w_s[j].shape
            gd = gmlp_w_out[j].shape[0] // n_groups
            bs = jnp.broadcast_to(gmlp_b_s[j][:, :, None], (n_groups, chunk, gd))
            ws = (wm[0], wm[1].reshape(gmlp_w_s[j].shape), bs, wm[2])
            later = adaln_jobs([i + 1]) if i + 1 < depth else []
            xp, extra = _ffn(_gmlp_layer(xp, stp, m, 0, gm, *ws)[0], stp, m, 0, *ffn, jobs=nxt + later)
            wb = extra[:len(nxt)]
            for job, out in zip(later, extra[len(nxt):]):
                mods[job[4]] = mod_table(out)
            xs, _ = _ffn(_gmlp_layer(xs, sts, m, 0, gm, *ws)[0], sts_ffn, m, 0, *ffn)
        else:
            ws = (wm[0], jnp.pad(conv_w[j], ((0, SUBLANES - CONV_WIDTH), (0, 0))), wm[1])
            later = adaln_jobs([i + 1]) if i + 1 < depth else []
            xp, extra = _conv_ffn_layer(xp, stp, m, 0, gm, *ws, *ffn, jobs=nxt + later)
            wb = extra[:len(nxt)]
            for job, out in zip(later, extra[len(nxt):]):
                mods[job[4]] = mod_table(out)
            xs, _ = _conv_ffn_layer(xs, sts, m, 0, gm, *ws, *ffn)

    return (xp.reshape(x_prompt.shape), xs.reshape(x_sample.shape),
            jnp.stack(new_re, axis=1), jnp.stack(new_im, axis=1))
```

```python
import functools
import math

import jax
import jax.numpy as jnp
from jax import lax
from jax.experimental import pallas as pl
from jax.experimental.pallas import tpu as pltpu

EPS = 1e-6
N_MOD = 6
GRID_W = 64
SSM_GROUP = 16
SSM_CHUNK = 16
CONV_WIDTH = 3
SUBLANES = 8
LANES = 128
ROW_BLOCK = 512
FFN_ROW_BLOCK = 1024
SEQ_CHUNKS = 16
SEQ_PAIR = 2
CTX_ROWS = 4
VMEM_LIMIT = 56 * 1024 * 1024

F32 = jnp.float32
BF16 = jnp.bfloat16
NT = (((1,), (1,)), ((), ()))


def _cparams(n_axes=1, vmem=VMEM_LIMIT):
    return pltpu.CompilerParams(dimension_semantics=("arbitrary",) * n_axes, vmem_limit_bytes=vmem)


def _gelu(x):
    return 0.5 * x * (1.0 + jnp.tanh(math.sqrt(2.0 / math.pi) * (x + 0.044715 * (x * x * x))))


def _norm_mod(x, g, shift, scale):
    y = x * lax.rsqrt(jnp.mean(x * x, axis=-1, keepdims=True) + EPS)
    return (y * g) * (1.0 + scale) + shift


def _cmul(ar, ai, br, bi):
    return ar * br - ai * bi, ar * bi + ai * br


def _resident(shape):
    nd = len(shape)
    return pl.BlockSpec(shape, lambda *_: (0,) * nd, pipeline_mode=pl.Buffered(1))


def _side_plan(jobs, n_steps, step_of):
    kinds, ins, args, outs, shapes = [], [], [], [], []
    for job in jobs:
        if job[0] == "cast":
            _, stack, layer = job
            _, r, c = stack.shape
            blk = r // n_steps
            assert r % n_steps == 0 and blk % (2 * SUBLANES) == 0
            kinds.append(("cast", 1, None))
            ins.append(pl.BlockSpec((None, blk, c), lambda *ids, layer=layer: (layer, step_of(*ids), 0)))
            args.append(stack)
            outs.append(pl.BlockSpec((blk, c), lambda *ids: (step_of(*ids), 0)))
            shapes.append(jax.ShapeDtypeStruct((r, c), BF16))
        else:
            _, cond_t, w_mod, b_mod, layer, n_cond = job
            _, d, n = w_mod.shape
            tn = n // n_steps
            assert n % n_steps == 0 and tn % LANES == 0
            kinds.append(("adaln", 3, n_cond))
            ins += [_resident(cond_t.shape),
                    pl.BlockSpec((None, d, tn), lambda *ids, layer=layer: (layer, 0, step_of(*ids))),
                    pl.BlockSpec((None, 1, tn), lambda *ids, layer=layer: (layer, 0, step_of(*ids)))]
            args += [cond_t, w_mod, b_mod.reshape(b_mod.shape[0], 1, n)]
            outs.append(pl.BlockSpec((SUBLANES, tn), lambda *ids: (0, step_of(*ids))))
            shapes.append(jax.ShapeDtypeStruct((SUBLANES, n), F32))
    return tuple(kinds), ins, args, outs, shapes


def _side_counts(kinds):
    return sum(k[1] for k in kinds), len(kinds)


def _side_run(kinds, in_refs, out_refs):
    i = 0
    for (name, n_in, param), dst in zip(kinds, out_refs):
        if name == "cast":
            dst[...] = in_refs[i][...].astype(BF16)
        else:
            dst[...] = _adaln_rows(in_refs[i][...], in_refs[i + 1][...], param) + in_refs[i + 2][...]
        i += n_in


def _adaln_rows(ct, w, n_cond):
    s = ct * jax.nn.sigmoid(ct)
    rows = [jnp.sum(s[:, r:r + 1] * w, axis=0, keepdims=True) for r in range(n_cond)]
    rows += [jnp.zeros_like(rows[0])] * (SUBLANES - n_cond)
    return jnp.concatenate(rows, axis=0)


class _Stream:
    def __init__(self, n_chunks, n_batch, per_batch_cond, row_block=ROW_BLOCK):
        self.nc, self.nb = n_chunks, n_batch
        self.n_rows = n_chunks * n_batch * SSM_CHUNK
        self.per_batch_cond = per_batch_cond
        self.row_block = row_block
        self.row_chunks = row_block // SSM_CHUNK
        self.seq_blocks = n_chunks // SEQ_CHUNKS
        assert n_chunks % SEQ_CHUNKS == 0 and self.n_rows % row_block == 0 and n_batch % SEQ_PAIR == 0
        assert not per_batch_cond or n_chunks % self.row_chunks == 0

    def row_grid(self):
        return (self.nb, self.nc // self.row_chunks) if self.per_batch_cond else (self.n_rows // self.row_block,)

    def row_view(self, x):
        return x if self.per_batch_cond else x.reshape(self.n_rows, x.shape[-1])

    def row_spec(self, width):
        if self.per_batch_cond:
            return pl.BlockSpec((self.row_chunks, None, SSM_CHUNK, width), lambda b, j: (j, b, 0, 0))
        return pl.BlockSpec((self.row_block, width), lambda i: (i, 0))

    def row_shape(self, width):
        return (self.nc, self.nb, SSM_CHUNK, width) if self.per_batch_cond else (self.n_rows, width)

    def unview(self, x):
        return x.reshape(self.nc, self.nb, SSM_CHUNK, x.shape[-1])

    def nat_view(self, x):
        return x.reshape(self.nb, self.nc, SSM_CHUNK, x.shape[-1])

    def nat_spec(self, width):
        if self.per_batch_cond:
            return pl.BlockSpec((None, self.row_chunks, SSM_CHUNK, width), lambda b, j: (b, j, 0, 0))
        assert self.nb * SSM_CHUNK == self.row_block
        return pl.BlockSpec((self.nb, None, SSM_CHUNK, width), lambda i: (0, i, 0, 0))

    def pos_spec(self, width):
        return pl.BlockSpec((self.row_chunks, SSM_CHUNK, width), lambda b, j: (j, 0, 0))

    def seq_grid(self):
        return (self.nb // SEQ_PAIR, self.seq_blocks)

    def seq_spec(self, width):
        return pl.BlockSpec((SEQ_CHUNKS, SEQ_PAIR, SSM_CHUNK, width), lambda b, j: (j, b, 0, 0))

    def halo_specs(self, width):
        halves = SSM_CHUNK // SUBLANES
        shape = (None, SEQ_PAIR, None, SUBLANES, width)
        prev = pl.BlockSpec(shape, lambda b, j: (jnp.maximum(j * SEQ_CHUNKS - 1, 0), b, halves - 1, 0, 0))
        nxt = pl.BlockSpec(shape, lambda b, j: (jnp.minimum((j + 1) * SEQ_CHUNKS, self.nc - 1), b, 0, 0, 0))
        return prev, nxt

    def halo_view(self, x):
        return x.reshape(self.nc, self.nb, SSM_CHUNK // SUBLANES, SUBLANES, x.shape[-1])

    def mod_spec(self, layer, d):
        if self.per_batch_cond:
            return pl.BlockSpec((None, None, N_MOD, d), lambda b, j: (layer, CTX_ROWS + b, 0, 0))
        return pl.BlockSpec((None, None, N_MOD, d), lambda *_: (layer, 0, 0, 0))

    def seq_mod_spec(self, layer, d):
        if self.per_batch_cond:
            return pl.BlockSpec((None, SEQ_PAIR, N_MOD, d), lambda b, j: (layer, CTX_ROWS // SEQ_PAIR + b, 0, 0))
        return pl.BlockSpec((None, SEQ_PAIR, N_MOD, d), lambda *_: (layer, 0, 0, 0))

    def n_axes(self, seq=False):
        return 2 if (seq or self.per_batch_cond) else 1

    def side_plan(self, jobs, seq=False):
        if seq:
            return _side_plan(jobs, (self.nb // SEQ_PAIR) * self.seq_blocks, lambda b, j: b * self.seq_blocks + j)
        assert not jobs or not self.per_batch_cond
        return _side_plan(jobs, self.n_rows // self.row_block, lambda i: i)


def _grid_pos_embed(n_tokens, d):
    rows = n_tokens // GRID_W
    quarter = d // 4
    freq = 1.0 / (10000.0 ** (jnp.arange(quarter, dtype=F32) / quarter))
    ar = jnp.arange(rows, dtype=F32)[:, None] * freq
    ac = jnp.arange(GRID_W, dtype=F32)[:, None] * freq
    row_part = jnp.concatenate([jnp.sin(ar), jnp.cos(ar)], axis=-1)
    col_part = jnp.concatenate([jnp.sin(ac), jnp.cos(ac)], axis=-1)
    table = jnp.concatenate([jnp.broadcast_to(row_part[:, None], (rows, GRID_W, d // 2)),
                             jnp.broadcast_to(col_part[None], (rows, GRID_W, d // 2))], axis=-1)
    return table.reshape(rows * GRID_W, d)


def _adaln_kernel(ct_ref, w_ref, b_ref, o_ref, *, n_cond):
    o_ref[...] = _adaln_rows(ct_ref[...], w_ref[...], n_cond) + b_ref[...]


def _adaln(cond_t, w_mod, b_mod, n_cond, layers, tn=1024):
    depth, d, n = w_mod.shape
    l0 = layers[0]
    assert list(layers) == list(range(l0, l0 + len(layers)))
    return pl.pallas_call(
        functools.partial(_adaln_kernel, n_cond=n_cond),
        grid=(len(layers), n // tn),
        in_specs=[
            pl.BlockSpec((d, SUBLANES), lambda l, j: (0, 0)),
            pl.BlockSpec((None, d, tn), lambda l, j: (l + l0, 0, j)),
            pl.BlockSpec((None, 1, tn), lambda l, j: (l + l0, 0, j)),
        ],
        out_specs=pl.BlockSpec((None, SUBLANES, tn), lambda l, j: (l, 0, j)),
        out_shape=jax.ShapeDtypeStruct((len(layers), SUBLANES, n), F32),
        compiler_params=_cparams(2), name="adaln",
    )(cond_t, w_mod, b_mod.reshape(depth, 1, n))


def _rows(ref):
    return ref[...].reshape(-1, ref.shape[-1])


def _rms(x, g):
    return (x * lax.rsqrt(jnp.mean(x * x, axis=-1, keepdims=True) + EPS)) * g


def _ffn_blocks(xs, mods, g, w1_ref, w2_ref, th, g_final=None):
    hn = jnp.concatenate([_norm_mod(x, g, m[3:4], m[4:5]).astype(BF16) for x, m in zip(xs, mods)], axis=0)
    acc = jnp.zeros((hn.shape[0], w2_ref.shape[1]), F32)
    for c in range(w1_ref.shape[1] // th):
        h1 = jnp.dot(hn, w1_ref[:, c * th:(c + 1) * th], preferred_element_type=F32)
        h1 = jnp.square(jnp.maximum(h1, 0.0)).astype(BF16)
        acc = acc + jnp.dot(h1, w2_ref[c * th:(c + 1) * th, :], preferred_element_type=F32)
    outs, r0 = [], 0
    for x, m in zip(xs, mods):
        out = x + m[5:6] * acc[r0:r0 + x.shape[0]]
        outs.append(out if g_final is None else _rms(out, g_final))
        r0 += x.shape[0]
    return outs


def _ffn_kernel(x_ref, mod_ref, g_ref, w1_ref, w2_ref, *rest, th, side):
    n_in, _ = _side_counts(side)
    o_ref = rest[n_in]
    out, = _ffn_blocks([_rows(x_ref)], [mod_ref[...]], g_ref[...], w1_ref, w2_ref, th)
    o_ref[...] = out.reshape(o_ref.shape)
    _side_run(side, rest[:n_in], rest[n_in + 1:])


def _ffn(x, st, mods, layer, g, w1, w2, jobs=(), th=1024):
    d = x.shape[-1]
    side, s_in, s_args, s_out, s_shapes = st.side_plan(jobs)
    out, *extra = pl.pallas_call(
        functools.partial(_ffn_kernel, th=th, side=side),
        grid=st.row_grid(),
        in_specs=[st.row_spec(d), st.mod_spec(layer, d), _resident((1, d)), _resident(w1.shape), _resident(w2.shape)]
        + s_in,
        out_specs=[st.row_spec(d)] + s_out,
        out_shape=[jax.ShapeDtypeStruct(st.row_shape(d), F32)] + s_shapes,
        compiler_params=_cparams(st.n_axes()), name="ffn",
    )(st.row_view(x), mods, g, w1, w2, *s_args)
    return st.unview(out), extra


def _x_operands(x, st, pos):
    d = x.shape[-1]
    if x.ndim == 4:
        return [st.row_spec(d)], [st.row_view(x)]
    specs, args = [st.nat_spec(d)], [st.nat_view(x)]
    if pos is not None:
        specs.append(st.pos_spec(d))
        args.append(pos.reshape(st.nc, SSM_CHUNK, d))
    return specs, args


PERM_SPLIT = 2


def _t_major_perm(n_batch):
    rows = ROW_BLOCK // PERM_SPLIT
    n_chunks = rows // (n_batch * SSM_CHUNK)
    assert n_chunks * n_batch == 2 * SUBLANES
    src = jnp.arange(rows)
    b, c, t = src // (n_chunks * SSM_CHUNK), (src // SSM_CHUNK) % n_chunks, src % SSM_CHUNK
    dst = (t * n_chunks + c) * n_batch + b
    return (jnp.arange(rows)[:, None] == dst[None, :]).astype(BF16)


def _ssm_in_kernel(x_ref, *rest, n_batch, natural, n_sub):
    *pos_ref, mod_ref, g_ref, perm_ref, w_ref, o_ref = rest
    g = g_ref[...]
    d = x_ref.shape[-1]
    n_t = o_ref.shape[0]
    group = o_ref.shape[1] // n_sub
    piece = group // PERM_SPLIT
    for h in range(n_sub):
        parts = []
        for s in range(PERM_SPLIT):
            if n_batch == 1:
                mod = mod_ref[...]
                xs = x_ref[s * piece:(s + 1) * piece, h] if natural else x_ref[h, s * piece:(s + 1) * piece]
                hn = _norm_mod(xs.reshape(-1, d), g, mod[0:1], mod[1:2]).astype(BF16)
            else:
                cpp = piece // n_batch
                cs = slice((h * PERM_SPLIT + s) * cpp, (h * PERM_SPLIT + s + 1) * cpp)
                rows = []
                for b in range(n_batch):
                    xb = (x_ref[b, cs] if natural else x_ref[cs, b]).reshape(-1, d)
                    if pos_ref:
                        xb = xb + pos_ref[0][cs].reshape(-1, d)
                    mod = mod_ref[b]
                    rows.append(_norm_mod(xb, g, mod[0:1], mod[1:2]).astype(BF16))
                hn = jnp.concatenate(rows, axis=0)
            parts.append(jnp.dot(perm_ref[...], hn, preferred_element_type=F32).astype(BF16))
        hn = jnp.concatenate([p[t * piece:(t + 1) * piece] for t in range(n_t) for p in parts], axis=0)
        u = jnp.dot(hn, w_ref[...], preferred_element_type=F32)
        o_ref[:, h * group:(h + 1) * group, :] = u.astype(BF16).reshape(n_t, group, o_ref.shape[2])


def _ssm_in(x, st, mods, layer, g, w, pos=None, n_sub=2):
    d, width = x.shape[-1], w.shape[1]
    natural = x.ndim != 4
    group = ROW_BLOCK // SSM_CHUNK
    if not st.per_batch_cond:
        assert st.nb == group and st.nc % n_sub == 0
        cps, n_batch = n_sub, 1
        mod_spec = st.mod_spec(layer, d)
    else:
        assert group % st.nb == 0 and CTX_ROWS % st.nb == 0
        cps, n_batch = n_sub * (group // st.nb), st.nb
        assert st.nc % cps == 0
        mod_spec = pl.BlockSpec((None, st.nb, N_MOD, d), lambda i: (layer, CTX_ROWS // st.nb, 0, 0))
    if natural:
        x_specs, x_args = [pl.BlockSpec((st.nb, cps, SSM_CHUNK, d), lambda i: (0, i, 0, 0))], [st.nat_view(x)]
        if pos is not None:
            x_specs.append(pl.BlockSpec((cps, SSM_CHUNK, d), lambda i: (i, 0, 0)))
            x_args.append(pos.reshape(st.nc, SSM_CHUNK, d))
    else:
        x_specs, x_args = [pl.BlockSpec((cps, st.nb, SSM_CHUNK, d), lambda i: (i, 0, 0, 0))], [x]
    perm = _t_major_perm(min(st.nb, group // PERM_SPLIT))
    return pl.pallas_call(
        functools.partial(_ssm_in_kernel, n_batch=n_batch, natural=natural, n_sub=n_sub),
        grid=(st.nc // cps,),
        in_specs=x_specs + [mod_spec, _resident((1, d)), _resident(perm.shape), _resident(w.shape)],
        out_specs=pl.BlockSpec((SSM_CHUNK, n_sub * group, width), lambda i: (0, i, 0)),
        out_shape=jax.ShapeDtypeStruct((SSM_CHUNK, st.nc * st.nb, width), BF16),
        compiler_params=_cparams(1), name="ssm_in",
    )(*x_args, mods, g, perm, w)


def _ssm_out_ffn_kernel(x_ref, *rest, has_pos, has_final, th, side):
    pos_ref = rest[0] if has_pos else None
    y_ref, mod_ref, wout_ref, gf_ref, w1_ref, w2_ref = rest[has_pos:has_pos + 6]
    n_in = has_pos + 6 + has_final
    n_side, _ = _side_counts(side)
    g_final = rest[n_in - 1][...] if has_final else None
    o_ref = rest[n_in + n_side]
    x = _rows(x_ref) + _rows(pos_ref) if has_pos else _rows(x_ref)
    y = _rows(y_ref)
    mod = mod_ref[...]
    d = x.shape[-1]
    half = x.shape[0] // 2
    mids = []
    for k in range(2):
        rows = slice(k * half, (k + 1) * half)
        ag = jnp.dot(_gelu(y[rows]).astype(BF16), wout_ref[...], preferred_element_type=F32)
        mids.append(x[rows] + mod[2:3] * (ag[:, :d] * jax.nn.sigmoid(ag[:, d:])))
    outs = _ffn_blocks(mids, [mod, mod], gf_ref[...], w1_ref, w2_ref, th, g_final)
    o_ref[...] = jnp.concatenate(outs, axis=0).reshape(o_ref.shape)
    _side_run(side, rest[n_in:n_in + n_side], rest[n_in + n_side + 1:])


def _ssm_out_ffn(x, y, st, mods, layer, w_out, gf, w1, w2, pos=None, g_final=None, jobs=(), th=1024):
    d = x.shape[-1]
    x_specs, x_args = _x_operands(x, st, pos)
    last = g_final is not None
    side, s_in, s_args, s_out, s_shapes = st.side_plan(jobs)
    out, *extra = pl.pallas_call(
        functools.partial(_ssm_out_ffn_kernel, has_pos=len(x_args) == 2, has_final=last, th=th, side=side),
        grid=st.row_grid(),
        in_specs=x_specs + [st.row_spec(d), st.mod_spec(layer, d), _resident(w_out.shape), _resident((1, d)),
                            _resident(w1.shape), _resident(w2.shape)] + ([_resident((1, d))] if last else []) + s_in,
        out_specs=[st.nat_spec(d) if last else st.row_spec(d)] + s_out,
        out_shape=[jax.ShapeDtypeStruct((st.nb, st.nc, SSM_CHUNK, d) if last else st.row_shape(d), F32)] + s_shapes,
        compiler_params=_cparams(st.n_axes()), name="ssm_out_ffn",
    )(*x_args, st.row_view(st.unview(y)), mods, w_out, gf, w1, w2, *([g_final] if last else []), *s_args)
    return (out if last else st.unview(out)), extra


def _ssm_operators(prow, bt, ct):
    nk = SSM_CHUNK
    half = prow.shape[1] // 2
    lr, li = prow[0:1], prow[1:2]
    dt = jnp.exp(prow[2:3])
    mag = jnp.exp(lr * dt)
    ar, ai = mag * jnp.cos(li * dt), mag * jnp.sin(li * dt)
    den = lr * lr + li * li
    nr, ni = _cmul(ar - 1.0, ai, lr, -li)
    bbr, bbi = _cmul(nr / den, ni / den, bt[0:nk], bt[nk:2 * nk])
    ctr, cti = ct[0:nk], ct[nk:2 * nk]

    pr, pi = [jnp.ones_like(ar)], [jnp.zeros_like(ar)]
    for _ in range(nk):
        r, i = _cmul(pr[-1], pi[-1], ar, ai)
        pr.append(r)
        pi.append(i)
    is_fwd = lax.broadcasted_iota(jnp.int32, (1, 2 * half), 1) < half

    def pw(kf, kb):
        return jnp.where(is_fwd, pr[kf], pr[kb]), jnp.where(is_fwd, pi[kf], pi[kb])

    wp_r, wp_i, ca_r, ca_i, wc_r, wc_i = [], [], [], [], [], []
    for t in range(nk):
        r, i = _cmul(bbr, bbi, *pw(nk - 1 - t, t))
        wp_r.append(r)
        wp_i.append(i)
        r, i = _cmul(ctr, cti, *pw(t, nk - 1 - t))
        ca_r.append(r)
        ca_i.append(i)
        r, i = _cmul(ctr, cti, *pw(t + 1, nk - t))
        wc_r.append(r)
        wc_i.append(-i)
    cat = jnp.concatenate
    wp = cat([cat(wp_r, 0), cat(wp_i, 0)], axis=1).astype(BF16)
    ca = cat([cat(ca_r, 0), cat(ca_i, 0)], axis=1)
    wct = cat([cat(wc_r, 0), cat(wc_i, 0)], axis=1).astype(BF16)

    zero = jnp.zeros_like(bbr)
    lhs = cat([cat([jnp.where(is_fwd, bbr, zero), jnp.where(is_fwd, -bbi, zero)], 1),
               cat([jnp.where(is_fwd, zero, bbr), jnp.where(is_fwd, zero, -bbi)], 1)], 0)
    lhs_h, ca_h = lhs.astype(BF16), ca.astype(BF16)
    lhs_l, ca_l = (lhs - lhs_h.astype(F32)).astype(BF16), (ca - ca_h.astype(F32)).astype(BF16)
    kk = (lax.dot_general(lhs_h, ca_h, NT, preferred_element_type=F32)
          + lax.dot_general(lhs_h, ca_l, NT, preferred_element_type=F32)
          + lax.dot_general(lhs_l, ca_h, NT, preferred_element_type=F32))
    width = nk * SSM_GROUP
    lane = lax.broadcasted_iota(jnp.int32, (nk, width), 1)
    row = lax.broadcasted_iota(jnp.int32, (nk, width), 0)
    d_lanes = cat([prow[3:4], jnp.zeros_like(prow[3:4])], axis=1)
    kf = kk[0:nk] + jnp.where(lane == row, d_lanes, 0.0)
    return kf, kk[nk:2 * nk], wp, wct, pr[nk], pi[nk]


def _toeplitz_shifts():
    width = SSM_CHUNK * SSM_GROUP
    src = jnp.arange(width)[:, None]
    dst = jnp.arange(width)[None, :]
    fwd = [dst == src + SSM_GROUP * s for s in range(SSM_CHUNK)]
    bwd = [src == dst + SSM_GROUP * (SSM_CHUNK - 1 - s) for s in range(SSM_CHUNK)]
    return jnp.concatenate(fwd, axis=1).astype(BF16), jnp.concatenate(bwd, axis=1).astype(BF16)


def _ssm_core_kernel(prow_ref, bt_ref, ct_ref, h0r_ref, h0i_ref, shf_ref, shb_ref, up_ref, us_ref, *rest,
                     p_batch, s_batch, side):
    n_in, n_out = _side_counts(side)
    yp_ref, ys_ref, fr_ref, fi_ref = rest[n_in:n_in + 4]
    atp_scr, ats_scr, tg_scr, wp_scr, wct_scr, u_scr, p_scr, s_scr, q_scr = rest[n_in + 4 + n_out:]
    _side_run(side, rest[:n_in], rest[n_in + 4:n_in + 4 + n_out])
    nk = SSM_CHUNK
    n_groups = LANES // SSM_GROUP
    cb_p, cb_s = atp_scr.shape[2], ats_scr.shape[2]
    w2 = prow_ref.shape[2]
    width = nk * SSM_GROUP
    decay, kfs, kbs = [], [], []
    for g in range(n_groups):
        kf, kb, wp, wct, a_r, a_i = _ssm_operators(prow_ref[g], bt_ref[g], ct_ref[g])
        kfs.append(kf)
        kbs.append(kb)
        wp_scr[g] = wp
        wct_scr[g] = wct
        decay.append((a_r, a_i))
    tz = (jnp.dot(jnp.concatenate(kfs, axis=0).astype(BF16), shf_ref[...], preferred_element_type=F32)
          + jnp.dot(jnp.concatenate(kbs, axis=0).astype(BF16), shb_ref[...], preferred_element_type=F32))
    for g in range(n_groups):
        for s in range(nk):
            tg_scr[g, s * SSM_GROUP:(s + 1) * SSM_GROUP, :] = (
                tz[g * SSM_GROUP:(g + 1) * SSM_GROUP, s * width:(s + 1) * width].astype(BF16))

    is_fwd = lax.broadcasted_iota(jnp.int32, (1, w2), 1) < (w2 // 2)

    def scan(n_tiles, rows, src_scr, carries, mults):
        for j in range(n_tiles):
            lo, hi = j * rows, (n_tiles - 1 - j) * rows
            for g in range(n_groups):
                c_r, c_i = carries[g]
                for c, l0 in ((c_r, 0), (c_i, w2)):
                    if lo < hi:
                        s_scr[g, lo:lo + rows, l0:l0 + w2] = c
                        s_scr[g, hi:hi + rows, l0:l0 + w2] = c
                    else:
                        s_scr[g, lo:lo + rows, l0:l0 + w2] = jnp.where(is_fwd, c, s_scr[g, lo:lo + rows, l0:l0 + w2])
                        s_scr[g, hi:hi + rows, l0:l0 + w2] = jnp.where(is_fwd, s_scr[g, hi:hi + rows, l0:l0 + w2], c)
                s_r = jnp.where(is_fwd, src_scr[g, lo:lo + rows, 0:w2], src_scr[g, hi:hi + rows, 0:w2])
                s_i = jnp.where(is_fwd, src_scr[g, lo:lo + rows, w2:2 * w2], src_scr[g, hi:hi + rows, w2:2 * w2])
                n_r, n_i = _cmul(*mults[g], c_r, c_i)
                carries[g] = (n_r + s_r, n_i + s_i)
        return carries

    def prompt_carry(cb):
        z = jnp.zeros((p_batch, w2), F32)
        return scan(cb // p_batch, p_batch, p_scr, [(z, z)] * n_groups, decay)

    def sample_carry(cb):
        assert 2 * s_batch == SUBLANES
        row = lax.broadcasted_iota(jnp.int32, (SUBLANES, w2), 0)
        keep = jnp.where(is_fwd, 1, 0) == jnp.where(row < s_batch, 1, 0)
        carries, mults = [], []
        for g in range(n_groups):
            a_r, a_i = decay[g]
            p_r, p_i = p_scr[g, 0:cb, 0:w2], p_scr[g, 0:cb, w2:2 * w2]
            ap_r, ap_i = _cmul(a_r, a_i, p_r, p_i)
            q_scr[g, 0:cb, 0:w2] = jnp.where(is_fwd, pltpu.roll(p_r, cb - s_batch, 0), pltpu.roll(p_r, s_batch, 0)) + ap_r
            q_scr[g, 0:cb, w2:2 * w2] = jnp.where(is_fwd, pltpu.roll(p_i, cb - s_batch, 0),
                                                  pltpu.roll(p_i, s_batch, 0)) + ap_i
            h_r, h_i = h0r_ref[g], h0i_ref[g]
            e_r = pltpu.roll(jnp.where(is_fwd, p_r[0:SUBLANES], p_r[cb - SUBLANES:cb]), s_batch, 0)
            e_i = pltpu.roll(jnp.where(is_fwd, p_i[0:SUBLANES], p_i[cb - SUBLANES:cb]), s_batch, 0)
            ah_r, ah_i = _cmul(a_r, a_i, h_r, h_i)
            carries.append((jnp.where(keep, h_r, ah_r + e_r), jnp.where(keep, h_i, ah_i + e_i)))
            mults.append(_cmul(a_r, a_i, a_r, a_i))
        return scan(cb // SUBLANES, SUBLANES, q_scr, carries, mults)

    def mix(u_ref, y_ref, at_scr, cb, carry_fn):
        for t in range(nk):
            at_scr[t] = u_ref[t].T
        for g in range(n_groups):
            a = at_scr[:, g * SSM_GROUP:(g + 1) * SSM_GROUP, :].reshape(nk * SSM_GROUP, cb)
            u = a.T
            u_scr[g, 0:cb, :] = u
            p_scr[g, 0:cb, :] = jnp.dot(u, wp_scr[g], preferred_element_type=F32)
        finals = carry_fn(cb)
        for g in range(n_groups):
            s = s_scr[g, 0:cb, :].astype(BF16)
            y = (jnp.dot(u_scr[g, 0:cb, :], tg_scr[g], preferred_element_type=F32)
                 + lax.dot_general(s, wct_scr[g], NT, preferred_element_type=F32))
            at_scr[:, g * SSM_GROUP:(g + 1) * SSM_GROUP, :] = y.astype(BF16).T.reshape(nk, SSM_GROUP, cb)
        for t in range(nk):
            y_ref[pl.ds(t, cb, stride=nk), :] = at_scr[t].T.astype(F32)
        return finals

    finals = mix(up_ref, yp_ref, atp_scr, cb_p, prompt_carry)
    for g in range(n_groups):
        fr_ref[g] = finals[g][0]
        fi_ref[g] = finals[g][1]
    mix(us_ref, ys_ref, ats_scr, cb_s, sample_carry)


def _ssm_core(prow, bt, ct, h0r, h0i, up, us, p_batch, s_batch, jobs=()):
    groups, _, w2 = prow.shape
    gpb = LANES // SSM_GROUP
    cb_p, cb_s = up.shape[1], us.shape[1]
    np_rows, ns_rows = cb_p * SSM_CHUNK, cb_s * SSM_CHUNK
    cb_max = max(cb_p, cb_s)
    width = SSM_CHUNK * SSM_GROUP
    shf, shb = _toeplitz_shifts()

    def gspec(r, c):
        return pl.BlockSpec((gpb, r, c), lambda o: (o, 0, 0))

    def lane_spec(n):
        return pl.BlockSpec((n, LANES), lambda o: (0, o))

    def in_spec(cb):
        return pl.BlockSpec((SSM_CHUNK, cb, LANES), lambda o: (0, 0, o))

    side, c_in, c_args, c_out, c_shapes = _side_plan(jobs, groups // gpb, lambda o: o)
    yp, ys, f_re, f_im, *cast = pl.pallas_call(
        functools.partial(_ssm_core_kernel, p_batch=p_batch, s_batch=s_batch, side=side),
        grid=(groups // gpb,),
        in_specs=[gspec(SUBLANES, w2), gspec(2 * SSM_GROUP, w2), gspec(2 * SSM_GROUP, w2),
                  gspec(SUBLANES, w2), gspec(SUBLANES, w2), _resident(shf.shape), _resident(shb.shape),
                  in_spec(cb_p), in_spec(cb_s)] + c_in,
        out_specs=[lane_spec(np_rows), lane_spec(ns_rows), gspec(p_batch, w2), gspec(p_batch, w2)] + c_out,
        out_shape=[jax.ShapeDtypeStruct((np_rows, up.shape[2]), F32), jax.ShapeDtypeStruct((ns_rows, us.shape[2]), F32),
                   jax.ShapeDtypeStruct((groups, p_batch, w2), F32),
                   jax.ShapeDtypeStruct((groups, p_batch, w2), F32)] + c_shapes,
        scratch_shapes=[pltpu.VMEM((SSM_CHUNK, LANES, cb_p), BF16), pltpu.VMEM((SSM_CHUNK, LANES, cb_s), BF16)]
        + [pltpu.VMEM((gpb, width, width), BF16)] * 3 + [pltpu.VMEM((gpb, cb_max, width), BF16)]
        + [pltpu.VMEM((gpb, cb_max, 2 * w2), F32)] * 2 + [pltpu.VMEM((gpb, cb_s, 2 * w2), F32)],
        compiler_params=_cparams(), name="ssm_core",
    )(prow, bt, ct, h0r, h0i, shf, shb, up, us, *c_args)
    return yp, ys, f_re, f_im, cast


def _ssm_layer(xp, xs, stp, sts, mods, layer, g_mix, w_in, lam_re, lam_im, log_dt, b_re, b_im, c_re, c_im, d_skip,
               w_out, h0_re, h0_im, ffn, pos=None, g_final=None, jobs=(), late=()):
    width = w_in.shape[1]
    groups = width // SSM_GROUP
    n_state = lam_re.shape[-1]
    up = _ssm_in(xp, stp, mods, layer, g_mix, w_in)
    us = _ssm_in(xs, sts, mods, layer, g_mix, w_in, pos)

    def lanes_dir_state(v):
        return v.transpose(1, 0, 2).reshape(groups, 2 * n_state)

    d_rows = jnp.pad(d_skip.reshape(groups, SSM_GROUP), ((0, 0), (0, 2 * n_state - SSM_GROUP)))
    prow = jnp.stack([lanes_dir_state(lam_re), lanes_dir_state(lam_im),
                      lanes_dir_state(jnp.broadcast_to(log_dt[..., None], lam_re.shape)), d_rows], axis=1)
    prow = jnp.pad(prow, ((0, 0), (0, SUBLANES - 4), (0, 0)))
    bt = jnp.concatenate([b_re.transpose(1, 3, 0, 2).reshape(groups, SSM_GROUP, 2 * n_state),
                          b_im.transpose(1, 3, 0, 2).reshape(groups, SSM_GROUP, 2 * n_state)], axis=1)
    ct = jnp.concatenate([c_re.transpose(1, 2, 0, 3).reshape(groups, SSM_GROUP, 2 * n_state),
                          c_im.transpose(1, 2, 0, 3).reshape(groups, SSM_GROUP, 2 * n_state)], axis=1)

    def h0_rows(h):
        h = h.transpose(2, 0, 1, 3).reshape(groups, sts.nb, 2 * n_state)
        return jnp.concatenate([h] * (SUBLANES // sts.nb), axis=1)

    yp, ys, f_re, f_im, conv = _ssm_core(prow, bt, ct, h0_rows(h0_re), h0_rows(h0_im), up, us, stp.nb, sts.nb, late)
    if late:
        w_out, ffn = conv[0], (ffn[0], conv[1], conv[2])
    xp, cast = _ssm_out_ffn(xp, yp, stp, mods, layer, w_out, *ffn, g_final=g_final, jobs=jobs)
    xs, _ = _ssm_out_ffn(xs, ys, sts, mods, layer, w_out, *ffn, pos=pos, g_final=g_final)

    def final(f):
        return f.reshape(groups, stp.nb, 2, n_state).transpose(1, 2, 0, 3)

    return xp, xs, final(f_re), final(f_im), cast


def _seq_rows(ref, k):
    return ref[:, k].reshape(-1, ref.shape[-1])


def _gmlp_kernel(x_ref, mod_ref, g_ref, win_ref, ws_ref, bs_ref, wout_ref, *rest, side):
    n_in, n_out = _side_counts(side)
    o_ref, t_scr = rest[n_in], rest[n_in + 1 + n_out]
    _side_run(side, rest[:n_in], rest[n_in + 1:n_in + 1 + n_out])
    n_groups, chunk, gd = bs_ref.shape
    xs, us, vns = [], [], []
    for k in range(SEQ_PAIR):
        x = _seq_rows(x_ref, k)
        mod = mod_ref[k]
        hn = _norm_mod(x, g_ref[...], mod[0:1], mod[1:2]).astype(BF16)
        z = _gelu(jnp.dot(hn, win_ref[...], preferred_element_type=F32))
        wdt = z.shape[1] // 2
        u, v = z[:, :wdt], z[:, wdt:]
        vc = v - jnp.mean(v, axis=-1, keepdims=True)
        xs.append(x)
        us.append(u)
        vns.append((vc * lax.rsqrt(jnp.mean(vc * vc, axis=-1, keepdims=True) + EPS)).astype(BF16))
    n_chunks = xs[0].shape[0] // chunk
    for g in range(n_groups):
        cols = slice(g * gd, (g + 1) * gd)
        rhs = jnp.concatenate([vn[c * chunk:(c + 1) * chunk, cols] for vn in vns for c in range(n_chunks)], axis=1)
        s = jnp.dot(ws_ref[g], rhs, preferred_element_type=F32)
        for k in range(SEQ_PAIR):
            for c in range(n_chunks):
                rows = slice(c * chunk, (c + 1) * chunk)
                i = k * n_chunks + c
                t_scr[k, rows, cols] = (us[k][rows, cols] * (s[:, i * gd:(i + 1) * gd] + bs_ref[g])).astype(BF16)
    for k in range(SEQ_PAIR):
        out = jnp.dot(t_scr[k], wout_ref[...], preferred_element_type=F32)
        o_ref[:, k] = (xs[k] + mod_ref[k][2:3] * out).reshape(o_ref.shape[0], o_ref.shape[2], o_ref.shape[3])


def _gmlp_layer(x, st, mods, layer, g_mix, w_in, w_s, bs, w_out, jobs=()):
    d = x.shape[-1]
    side, s_in, s_args, s_out, s_shapes = st.side_plan(jobs, seq=True)
    out, *extra = pl.pallas_call(
        functools.partial(_gmlp_kernel, side=side),
        grid=st.seq_grid(),
        in_specs=[st.seq_spec(d), st.seq_mod_spec(layer, d), _resident((1, d)), _resident(w_in.shape),
                  _resident(w_s.shape), _resident(bs.shape), _resident(w_out.shape)] + s_in,
        out_specs=[st.seq_spec(d)] + s_out,
        out_shape=[jax.ShapeDtypeStruct(x.shape, F32)] + s_shapes,
        scratch_shapes=[pltpu.VMEM((SEQ_PAIR, SEQ_CHUNKS * SSM_CHUNK, w_out.shape[0]), BF16)],
        compiler_params=_cparams(2), name="gmlp",
    )(x, mods, g_mix, w_in, w_s, bs, w_out, *s_args)
    return out, extra


def _conv_ffn_kernel(x_ref, xprev_ref, xnext_ref, mod_ref, g_ref, win_ref, cw_ref, wout_ref, gf_ref, w1_ref, w2_ref,
                     *rest, seq_blocks, th, side):
    n_in, _ = _side_counts(side)
    o_ref = rest[n_in]
    j = pl.program_id(1)
    g = g_ref[...]
    cw = cw_ref[...]
    d = x_ref.shape[-1]
    xs = [_seq_rows(x_ref, k) for k in range(SEQ_PAIR)]
    mods = [mod_ref[k] for k in range(SEQ_PAIR)]
    if seq_blocks > 1:
        hh = jnp.concatenate([_norm_mod(jnp.concatenate([xprev_ref[k], xnext_ref[k]], axis=0), g, m[0:1], m[1:2])
                              for k, m in enumerate(mods)], axis=0).astype(BF16)
        zh = jnp.dot(hh, win_ref[:, d:], preferred_element_type=F32)
        th_all = zh[:, :d] * zh[:, d:]
    else:
        th_all = jnp.zeros((2 * SUBLANES * SEQ_PAIR, d), F32)
    mids = []
    for k, (x, mod) in enumerate(zip(xs, mods)):
        tm = x.shape[0]
        hn = _norm_mod(x, g, mod[0:1], mod[1:2]).astype(BF16)
        z = jnp.dot(hn, win_ref[...], preferred_element_type=F32)
        gb, t = z[:, :d], z[:, d:2 * d] * z[:, 2 * d:]
        r0 = 2 * SUBLANES * k
        t_prev = jnp.where(j > 0, th_all[r0 + SUBLANES - 1:r0 + SUBLANES], 0.0)
        t_next = jnp.where(j < seq_blocks - 1, th_all[r0 + SUBLANES:r0 + SUBLANES + 1], 0.0)
        row = lax.broadcasted_iota(jnp.int32, (tm, 1), 0)
        up = jnp.where(row == 0, t_prev, pltpu.roll(t, 1, 0))
        dn = jnp.where(row == tm - 1, t_next, pltpu.roll(t, tm - 1, 0))
        y = cw[0:1] * up + cw[1:2] * t + cw[2:3] * dn
        out = jnp.dot((gb * y).astype(BF16), wout_ref[...], preferred_element_type=F32)
        mids.append(x + mod[2:3] * out)
    outs = _ffn_blocks(mids, mods, gf_ref[...], w1_ref, w2_ref, th)
    for k, out in enumerate(outs):
        o_ref[:, k] = out.reshape(o_ref.shape[0], o_ref.shape[2], o_ref.shape[3])
    _side_run(side, rest[:n_in], rest[n_in + 1:])


def _conv_ffn_layer(x, st, mods, layer, g_mix, w_in, cw, w_out, gf, w1, w2, jobs=(), th=1024):
    d = x.shape[-1]
    prev, nxt = st.halo_specs(d)
    xh = st.halo_view(x)
    side, c_in, c_args, c_out, c_shapes = st.side_plan(jobs, seq=True)
    out, *cast = pl.pallas_call(
        functools.partial(_conv_ffn_kernel, seq_blocks=st.seq_blocks, th=th, side=side),
        grid=st.seq_grid(),
        in_specs=[st.seq_spec(d), prev, nxt, st.seq_mod_spec(layer, d), _resident((1, d)), _resident(w_in.shape),
                  _resident(cw.shape), _resident(w_out.shape), _resident((1, d)), _resident(w1.shape),
                  _resident(w2.shape)] + c_in,
        out_specs=[st.seq_spec(d)] + c_out,
        out_shape=[jax.ShapeDtypeStruct(x.shape, F32)] + c_shapes,
        compiler_params=_cparams(2), name="conv_ffn",
    )(x, xh, xh, mods, g_mix, w_in, cw, w_out, gf, w1, w2, *c_args)
    return out, cast


def kernel(x_prompt, x_sample, state_ssm_re, state_ssm_im, c, c_ctx, w_mod, b_mod, g_mix, g_ffn, ffn_w1, ffn_w2, ssm_w_in, ssm_lam_re, ssm_lam_im, ssm_log_dt, ssm_b_re, ssm_b_im, ssm_c_re, ssm_c_im, ssm_d, ssm_w_out, gmlp_w_in, gmlp_w_s, gmlp_b_s, gmlp_w_out, conv_w_in, conv_w, conv_w_out, g_final):
    n_prompt, len_prompt, d = x_prompt.shape
    n_sample, len_sample, _ = x_sample.shape
    depth = w_mod.shape[0]
    n_mixers = 3
    assert 1 + n_sample <= SUBLANES
    stp = _Stream(len_prompt // SSM_CHUNK, n_prompt, per_batch_cond=False)
    sts = _Stream(len_sample // SSM_CHUNK, n_sample, per_batch_cond=True)
    sts_ffn = _Stream(len_sample // SSM_CHUNK, n_sample, per_batch_cond=True, row_block=FFN_ROW_BLOCK)

    xp, xs = x_prompt, x_sample
    pos = _grid_pos_embed(len_sample, d)

    cond_t = jnp.concatenate([c_ctx[None, :], c, jnp.zeros((SUBLANES - 1 - n_sample, d), F32)], axis=0).T
    n_cond = 1 + n_sample
    table = jnp.array([0] * CTX_ROWS + list(range(1, n_cond)))

    def mod_table(m):
        return m[table].reshape(1, table.shape[0], N_MOD, d)

    assert (depth - 1) % n_mixers == 0, "the last layer's kernel writes the natural-order outputs"

    def layer_weights(i):
        kind, j = i % n_mixers, i // n_mixers
        mixer = {0: [(ssm_w_in, j), (ssm_w_out, j)],
                 1: [(gmlp_w_in, j), (gmlp_w_s.reshape(gmlp_w_s.shape[0], -1, gmlp_w_s.shape[-1]), j), (gmlp_w_out, j)],
                 2: [(conv_w_in, j), (conv_w_out, j)]}[kind]
        return [("cast", stack, layer) for stack, layer in mixer + [(ffn_w1, i), (ffn_w2, i)]]

    def adaln_jobs(layers):
        return [("adaln", cond_t, w_mod, b_mod, l, n_cond) for l in layers]

    first = layer_weights(0)
    wb = [first[0][1][first[0][2]].astype(BF16)] + [None] * (len(first) - 1)
    mods = {0: mod_table(_adaln(cond_t, w_mod, b_mod, n_cond, [0])[0])}
    new_re, new_im = [], []
    for i in range(depth):
        kind, j = i % n_mixers, i // n_mixers
        gm = g_mix[i].reshape(1, d)
        nxt = layer_weights(i + 1) if i + 1 < depth else []
        *wm, w1, w2 = wb
        ffn = (g_ffn[i].reshape(1, d), w1, w2)
        m = mods[i]
        if kind == 0:
            later = adaln_jobs([i + 1]) if i + 1 < depth else []
            xp, xs, f_re, f_im, extra = _ssm_layer(
                xp, xs, stp, sts, m, 0, gm, wm[0], ssm_lam_re[j], ssm_lam_im[j], ssm_log_dt[j],
                ssm_b_re[j], ssm_b_im[j], ssm_c_re[j], ssm_c_im[j], ssm_d[j], wm[1],
                state_ssm_re[:, j], state_ssm_im[:, j], ffn, pos if i == 0 else None,
                g_final.reshape(1, d) if i == depth - 1 else None, nxt + later, first[1:] if i == 0 else ())
            wb = extra[:len(nxt)]
            for job, out in zip(later, extra[len(nxt):]):
                mods[job[4]] = mod_table(out)
            new_re.append(f_re)
            new_im.append(f_im)
        elif kind == 1:
            n_groups, chunk, _ = gmlp_w_s[j].shape
            gd = gmlp_w_out[j].shape[0] // n_groups
            bs = jnp.broadcast_to(gmlp_b_s[j][:, :, None], (n_groups, chunk, gd))
            ws = (wm[0], wm[1].reshape(gmlp_w_s[j].shape), bs, wm[2])
            later = adaln_jobs([i + 1]) if i + 1 < depth else []
            xp, extra = _ffn(_gmlp_layer(xp, stp, m, 0, gm, *ws)[0], stp, m, 0, *ffn, jobs=nxt + later)
            wb = extra[:len(nxt)]
            for job, out in zip(later, extra[len(nxt):]):
                mods[job[4]] = mod_table(out)
            xs, _ = _ffn(_gmlp_layer(xs, sts, m, 0, gm, *ws)[0], sts_ffn, m, 0, *ffn)
        else:
            ws = (wm[0], jnp.pad(conv_w[j], ((0, SUBLANES - CONV_WIDTH), (0, 0))), wm[1])
            later = adaln_jobs([i + 1]) if i + 1 < depth else []
            xp, extra = _conv_ffn_layer(xp, stp, m, 0, gm, *ws, *ffn, jobs=nxt + later)
            wb = extra[:len(nxt)]
            for job, out in zip(later, extra[len(nxt):]):
                mods[job[4]] = mod_table(out)
            xs, _ = _conv_ffn_layer(xs, sts, m, 0, gm, *ws, *ffn)

    return (xp.reshape(x_prompt.shape), xs.reshape(x_sample.shape),
            jnp.stack(new_re, axis=1), jnp.stack(new_im, axis=1))
```

```python
import functools
import math

import jax
import jax.numpy as jnp
from jax import lax
from jax.experimental import pallas as pl
from jax.experimental.pallas import tpu as pltpu

EPS = 1e-6
N_MOD = 6
GRID_W = 64
SSM_GROUP = 16
SSM_CHUNK = 16
CONV_WIDTH = 3
SUBLANES = 8
LANES = 128
ROW_BLOCK = 512
FFN_ROW_BLOCK = 1024
SEQ_CHUNKS = 16
SEQ_PAIR = 2
CTX_ROWS = 4
VMEM_LIMIT = 56 * 1024 * 1024

F32 = jnp.float32
BF16 = jnp.bfloat16
NT = (((1,), (1,)), ((), ()))


def _cparams(n_axes=1, vmem=VMEM_LIMIT, fuse_inputs=None):
    return pltpu.CompilerParams(dimension_semantics=("arbitrary",) * n_axes, vmem_limit_bytes=vmem,
                                allow_input_fusion=fuse_inputs)


def _gelu(x):
    return 0.5 * x * (1.0 + jnp.tanh(math.sqrt(2.0 / math.pi) * (x + 0.044715 * (x * x * x))))


def _norm_mod(x, g, shift, scale):
    y = x * lax.rsqrt(jnp.mean(x * x, axis=-1, keepdims=True) + EPS)
    return (y * g) * (1.0 + scale) + shift


def _cmul(ar, ai, br, bi):
    return ar * br - ai * bi, ar * bi + ai * br


def _resident(shape):
    nd = len(shape)
    return pl.BlockSpec(shape, lambda *_: (0,) * nd, pipeline_mode=pl.Buffered(1))


def _side_plan(jobs, n_steps, step_of):
    kinds, ins, args, outs, shapes = [], [], [], [], []
    for job in jobs:
        if job[0] == "cast":
            _, stack, layer = job
            _, r, c = stack.shape
            blk = r // n_steps
            assert r % n_steps == 0 and blk % (2 * SUBLANES) == 0
            kinds.append(("cast", 1, None))
            ins.append(pl.BlockSpec((None, blk, c), lambda *ids, layer=layer: (layer, step_of(*ids), 0)))
            args.append(stack)
            outs.append(pl.BlockSpec((blk, c), lambda *ids: (step_of(*ids), 0)))
            shapes.append(jax.ShapeDtypeStruct((r, c), BF16))
        else:
            _, cond_t, w_mod, b_mod, layer, n_cond = job
            _, d, n = w_mod.shape
            tn = n // n_steps
            assert n % n_steps == 0 and tn % LANES == 0
            kinds.append(("adaln", 3, n_cond))
            ins += [_resident(cond_t.shape),
                    pl.BlockSpec((None, d, tn), lambda *ids, layer=layer: (layer, 0, step_of(*ids))),
                    pl.BlockSpec((None, 1, tn), lambda *ids, layer=layer: (layer, 0, step_of(*ids)))]
            args += [cond_t, w_mod, b_mod.reshape(b_mod.shape[0], 1, n)]
            outs.append(pl.BlockSpec((SUBLANES, tn), lambda *ids: (0, step_of(*ids))))
            shapes.append(jax.ShapeDtypeStruct((SUBLANES, n), F32))
    return tuple(kinds), ins, args, outs, shapes


def _side_counts(kinds):
    return sum(k[1] for k in kinds), len(kinds)


def _side_run(kinds, in_refs, out_refs):
    i = 0
    for (name, n_in, param), dst in zip(kinds, out_refs):
        if name == "cast":
            dst[...] = in_refs[i][...].astype(BF16)
        else:
            dst[...] = _adaln_rows(in_refs[i][...], in_refs[i + 1][...], param) + in_refs[i + 2][...]
        i += n_in


def _adaln_rows(ct, w, n_cond):
    s = ct * jax.nn.sigmoid(ct)
    rows = [jnp.sum(s[:, r:r + 1] * w, axis=0, keepdims=True) for r in range(n_cond)]
    rows += [jnp.zeros_like(rows[0])] * (SUBLANES - n_cond)
    return jnp.concatenate(rows, axis=0)


class _Stream:
    def __init__(self, n_chunks, n_batch, per_batch_cond, row_block=ROW_BLOCK):
        self.nc, self.nb = n_chunks, n_batch
        self.n_rows = n_chunks * n_batch * SSM_CHUNK
        self.per_batch_cond = per_batch_cond
        self.row_block = row_block
        self.row_chunks = row_block // SSM_CHUNK
        self.seq_blocks = n_chunks // SEQ_CHUNKS
        assert n_chunks % SEQ_CHUNKS == 0 and self.n_rows % row_block == 0 and n_batch % SEQ_PAIR == 0
        assert not per_batch_cond or n_chunks % self.row_chunks == 0

    def row_grid(self):
        return (self.nb, self.nc // self.row_chunks) if self.per_batch_cond else (self.n_rows // self.row_block,)

    def row_view(self, x):
        return x if self.per_batch_cond else x.reshape(self.n_rows, x.shape[-1])

    def row_spec(self, width):
        if self.per_batch_cond:
            return pl.BlockSpec((self.row_chunks, None, SSM_CHUNK, width), lambda b, j: (j, b, 0, 0))
        return pl.BlockSpec((self.row_block, width), lambda i: (i, 0))

    def row_shape(self, width):
        return (self.nc, self.nb, SSM_CHUNK, width) if self.per_batch_cond else (self.n_rows, width)

    def unview(self, x):
        return x.reshape(self.nc, self.nb, SSM_CHUNK, x.shape[-1])

    def nat_view(self, x):
        return x.reshape(self.nb, self.nc, SSM_CHUNK, x.shape[-1])

    def nat_spec(self, width):
        if self.per_batch_cond:
            return pl.BlockSpec((None, self.row_chunks, SSM_CHUNK, width), lambda b, j: (b, j, 0, 0))
        assert self.nb * SSM_CHUNK == self.row_block
        return pl.BlockSpec((self.nb, None, SSM_CHUNK, width), lambda i: (0, i, 0, 0))

    def pos_spec(self, width):
        return pl.BlockSpec((self.row_chunks, SSM_CHUNK, width), lambda b, j: (j, 0, 0))

    def seq_grid(self):
        return (self.nb // SEQ_PAIR, self.seq_blocks)

    def seq_spec(self, width):
        return pl.BlockSpec((SEQ_CHUNKS, SEQ_PAIR, SSM_CHUNK, width), lambda b, j: (j, b, 0, 0))

    def halo_specs(self, width):
        halves = SSM_CHUNK // SUBLANES
        shape = (None, SEQ_PAIR, None, SUBLANES, width)
        prev = pl.BlockSpec(shape, lambda b, j: (jnp.maximum(j * SEQ_CHUNKS - 1, 0), b, halves - 1, 0, 0))
        nxt = pl.BlockSpec(shape, lambda b, j: (jnp.minimum((j + 1) * SEQ_CHUNKS, self.nc - 1), b, 0, 0, 0))
        return prev, nxt

    def halo_view(self, x):
        return x.reshape(self.nc, self.nb, SSM_CHUNK // SUBLANES, SUBLANES, x.shape[-1])

    def mod_spec(self, layer, d):
        if self.per_batch_cond:
            return pl.BlockSpec((None, None, N_MOD, d), lambda b, j: (layer, CTX_ROWS + b, 0, 0))
        return pl.BlockSpec((None, None, N_MOD, d), lambda *_: (layer, 0, 0, 0))

    def seq_mod_spec(self, layer, d):
        if self.per_batch_cond:
            return pl.BlockSpec((None, SEQ_PAIR, N_MOD, d), lambda b, j: (layer, CTX_ROWS // SEQ_PAIR + b, 0, 0))
        return pl.BlockSpec((None, SEQ_PAIR, N_MOD, d), lambda *_: (layer, 0, 0, 0))

    def n_axes(self, seq=False):
        return 2 if (seq or self.per_batch_cond) else 1

    def side_plan(self, jobs, seq=False):
        if seq:
            return _side_plan(jobs, (self.nb // SEQ_PAIR) * self.seq_blocks, lambda b, j: b * self.seq_blocks + j)
        assert not jobs or not self.per_batch_cond
        return _side_plan(jobs, self.n_rows // self.row_block, lambda i: i)


def _grid_pos_embed(n_tokens, d):
    rows = n_tokens // GRID_W
    quarter = d // 4
    freq = 1.0 / (10000.0 ** (jnp.arange(quarter, dtype=F32) / quarter))
    ar = jnp.arange(rows, dtype=F32)[:, None] * freq
    ac = jnp.arange(GRID_W, dtype=F32)[:, None] * freq
    row_part = jnp.concatenate([jnp.sin(ar), jnp.cos(ar)], axis=-1)
    col_part = jnp.concatenate([jnp.sin(ac), jnp.cos(ac)], axis=-1)
    table = jnp.concatenate([jnp.broadcast_to(row_part[:, None], (rows, GRID_W, d // 2)),
                             jnp.broadcast_to(col_part[None], (rows, GRID_W, d // 2))], axis=-1)
    return table.reshape(rows * GRID_W, d)


def _adaln_kernel(ct_ref, w_ref, b_ref, o_ref, *, n_cond):
    o_ref[...] = _adaln_rows(ct_ref[...], w_ref[...], n_cond) + b_ref[...]


def _adaln(cond_t, w_mod, b_mod, n_cond, layers, tn=1024):
    depth, d, n = w_mod.shape
    l0 = layers[0]
    assert list(layers) == list(range(l0, l0 + len(layers)))
    return pl.pallas_call(
        functools.partial(_adaln_kernel, n_cond=n_cond),
        grid=(len(layers), n // tn),
        in_specs=[
            pl.BlockSpec((d, SUBLANES), lambda l, j: (0, 0)),
            pl.BlockSpec((None, d, tn), lambda l, j: (l + l0, 0, j)),
            pl.BlockSpec((None, 1, tn), lambda l, j: (l + l0, 0, j)),
        ],
        out_specs=pl.BlockSpec((None, SUBLANES, tn), lambda l, j: (l, 0, j)),
        out_shape=jax.ShapeDtypeStruct((len(layers), SUBLANES, n), F32),
        compiler_params=_cparams(2), name="adaln",
    )(cond_t, w_mod, b_mod.reshape(depth, 1, n))


def _rows(ref):
    return ref[...].reshape(-1, ref.shape[-1])


def _rms(x, g):
    return (x * lax.rsqrt(jnp.mean(x * x, axis=-1, keepdims=True) + EPS)) * g


def _ffn_blocks(xs, mods, g, w1_ref, w2_ref, th, g_final=None):
    hn = jnp.concatenate([_norm_mod(x, g, m[3:4], m[4:5]).astype(BF16) for x, m in zip(xs, mods)], axis=0)
    acc = jnp.zeros((hn.shape[0], w2_ref.shape[1]), F32)
    for c in range(w1_ref.shape[1] // th):
        h1 = jnp.dot(hn, w1_ref[:, c * th:(c + 1) * th], preferred_element_type=F32)
        h1 = jnp.square(jnp.maximum(h1, 0.0)).astype(BF16)
        acc = acc + jnp.dot(h1, w2_ref[c * th:(c + 1) * th, :], preferred_element_type=F32)
    outs, r0 = [], 0
    for x, m in zip(xs, mods):
        out = x + m[5:6] * acc[r0:r0 + x.shape[0]]
        outs.append(out if g_final is None else _rms(out, g_final))
        r0 += x.shape[0]
    return outs


def _ffn_kernel(x_ref, mod_ref, g_ref, w1_ref, w2_ref, *rest, th, side):
    n_in, _ = _side_counts(side)
    o_ref = rest[n_in]
    out, = _ffn_blocks([_rows(x_ref)], [mod_ref[...]], g_ref[...], w1_ref, w2_ref, th)
    o_ref[...] = out.reshape(o_ref.shape)
    _side_run(side, rest[:n_in], rest[n_in + 1:])


def _ffn(x, st, mods, layer, g, w1, w2, jobs=(), th=1024):
    d = x.shape[-1]
    side, s_in, s_args, s_out, s_shapes = st.side_plan(jobs)
    out, *extra = pl.pallas_call(
        functools.partial(_ffn_kernel, th=th, side=side),
        grid=st.row_grid(),
        in_specs=[st.row_spec(d), st.mod_spec(layer, d), _resident((1, d)), _resident(w1.shape), _resident(w2.shape)]
        + s_in,
        out_specs=[st.row_spec(d)] + s_out,
        out_shape=[jax.ShapeDtypeStruct(st.row_shape(d), F32)] + s_shapes,
        compiler_params=_cparams(st.n_axes()), name="ffn",
    )(st.row_view(x), mods, g, w1, w2, *s_args)
    return st.unview(out), extra


def _x_operands(x, st, pos):
    d = x.shape[-1]
    if x.ndim == 4:
        return [st.row_spec(d)], [st.row_view(x)]
    specs, args = [st.nat_spec(d)], [st.nat_view(x)]
    if pos is not None:
        specs.append(st.pos_spec(d))
        args.append(pos.reshape(st.nc, SSM_CHUNK, d))
    return specs, args


PERM_SPLIT = 2


def _t_major_perm(n_batch):
    rows = ROW_BLOCK // PERM_SPLIT
    n_chunks = rows // (n_batch * SSM_CHUNK)
    assert n_chunks * n_batch == 2 * SUBLANES
    src = jnp.arange(rows)
    b, c, t = src // (n_chunks * SSM_CHUNK), (src // SSM_CHUNK) % n_chunks, src % SSM_CHUNK
    dst = (t * n_chunks + c) * n_batch + b
    return (jnp.arange(rows)[:, None] == dst[None, :]).astype(BF16)


def _ssm_in_kernel(x_ref, *rest, n_batch, natural, n_sub):
    *pos_ref, mod_ref, g_ref, perm_ref, w_ref, o_ref = rest
    g = g_ref[...]
    d = x_ref.shape[-1]
    n_t = o_ref.shape[0]
    group = o_ref.shape[1] // n_sub
    piece = group // PERM_SPLIT
    for h in range(n_sub):
        parts = []
        for s in range(PERM_SPLIT):
            if n_batch == 1:
                mod = mod_ref[...]
                xs = x_ref[s * piece:(s + 1) * piece, h] if natural else x_ref[h, s * piece:(s + 1) * piece]
                hn = _norm_mod(xs.reshape(-1, d), g, mod[0:1], mod[1:2]).astype(BF16)
            else:
                cpp = piece // n_batch
                cs = slice((h * PERM_SPLIT + s) * cpp, (h * PERM_SPLIT + s + 1) * cpp)
                rows = []
                for b in range(n_batch):
                    xb = (x_ref[b, cs] if natural else x_ref[cs, b]).reshape(-1, d)
                    if pos_ref:
                        xb = xb + pos_ref[0][cs].reshape(-1, d)
                    mod = mod_ref[b]
                    rows.append(_norm_mod(xb, g, mod[0:1], mod[1:2]).astype(BF16))
                hn = jnp.concatenate(rows, axis=0)
            parts.append(jnp.dot(perm_ref[...], hn, preferred_element_type=F32).astype(BF16))
        hn = jnp.concatenate([p[t * piece:(t + 1) * piece] for t in range(n_t) for p in parts], axis=0)
        u = jnp.dot(hn, w_ref[...], preferred_element_type=F32)
        o_ref[:, h * group:(h + 1) * group, :] = u.astype(BF16).reshape(n_t, group, o_ref.shape[2])


def _ssm_in(x, st, mods, layer, g, w, pos=None, n_sub=2):
    d, width = x.shape[-1], w.shape[1]
    natural = x.ndim != 4
    group = ROW_BLOCK // SSM_CHUNK
    if not st.per_batch_cond:
        assert st.nb == group and st.nc % n_sub == 0
        cps, n_batch = n_sub, 1
        mod_spec = st.mod_spec(layer, d)
    else:
        assert group % st.nb == 0 and CTX_ROWS % st.nb == 0
        cps, n_batch = n_sub * (group // st.nb), st.nb
        assert st.nc % cps == 0
        mod_spec = pl.BlockSpec((None, st.nb, N_MOD, d), lambda i: (layer, CTX_ROWS // st.nb, 0, 0))
    if natural:
        x_specs, x_args = [pl.BlockSpec((st.nb, cps, SSM_CHUNK, d), lambda i: (0, i, 0, 0))], [st.nat_view(x)]
        if pos is not None:
            x_specs.append(pl.BlockSpec((cps, SSM_CHUNK, d), lambda i: (i, 0, 0)))
            x_args.append(pos.reshape(st.nc, SSM_CHUNK, d))
    else:
        x_specs, x_args = [pl.BlockSpec((cps, st.nb, SSM_CHUNK, d), lambda i: (i, 0, 0, 0))], [x]
    perm = _t_major_perm(min(st.nb, group // PERM_SPLIT))
    return pl.pallas_call(
        functools.partial(_ssm_in_kernel, n_batch=n_batch, natural=natural, n_sub=n_sub),
        grid=(st.nc // cps,),
        in_specs=x_specs + [mod_spec, _resident((1, d)), _resident(perm.shape), _resident(w.shape)],
        out_specs=pl.BlockSpec((SSM_CHUNK, n_sub * group, width), lambda i: (0, i, 0)),
        out_shape=jax.ShapeDtypeStruct((SSM_CHUNK, st.nc * st.nb, width), BF16),
        compiler_params=_cparams(1), name="ssm_in",
    )(*x_args, mods, g, perm, w)


def _ssm_out_ffn_kernel(x_ref, *rest, has_pos, has_final, th, side):
    pos_ref = rest[0] if has_pos else None
    y_ref, mod_ref, wout_ref, gf_ref, w1_ref, w2_ref = rest[has_pos:has_pos + 6]
    n_in = has_pos + 6 + has_final
    n_side, _ = _side_counts(side)
    g_final = rest[n_in - 1][...] if has_final else None
    o_ref = rest[n_in + n_side]
    x = _rows(x_ref) + _rows(pos_ref) if has_pos else _rows(x_ref)
    y = _rows(y_ref)
    mod = mod_ref[...]
    d = x.shape[-1]
    half = x.shape[0] // 2
    mids = []
    for k in range(2):
        rows = slice(k * half, (k + 1) * half)
        ag = jnp.dot(_gelu(y[rows]).astype(BF16), wout_ref[...], preferred_element_type=F32)
        mids.append(x[rows] + mod[2:3] * (ag[:, :d] * jax.nn.sigmoid(ag[:, d:])))
    outs = _ffn_blocks(mids, [mod, mod], gf_ref[...], w1_ref, w2_ref, th, g_final)
    o_ref[...] = jnp.concatenate(outs, axis=0).reshape(o_ref.shape)
    _side_run(side, rest[n_in:n_in + n_side], rest[n_in + n_side + 1:])


def _ssm_out_ffn(x, y, st, mods, layer, w_out, gf, w1, w2, pos=None, g_final=None, jobs=(), th=1024):
    d = x.shape[-1]
    x_specs, x_args = _x_operands(x, st, pos)
    last = g_final is not None
    side, s_in, s_args, s_out, s_shapes = st.side_plan(jobs)
    out, *extra = pl.pallas_call(
        functools.partial(_ssm_out_ffn_kernel, has_pos=len(x_args) == 2, has_final=last, th=th, side=side),
        grid=st.row_grid(),
        in_specs=x_specs + [st.row_spec(d), st.mod_spec(layer, d), _resident(w_out.shape), _resident((1, d)),
                            _resident(w1.shape), _resident(w2.shape)] + ([_resident((1, d))] if last else []) + s_in,
        out_specs=[st.nat_spec(d) if last else st.row_spec(d)] + s_out,
        out_shape=[jax.ShapeDtypeStruct((st.nb, st.nc, SSM_CHUNK, d) if last else st.row_shape(d), F32)] + s_shapes,
        compiler_params=_cparams(st.n_axes()), name="ssm_out_ffn",
    )(*x_args, st.row_view(st.unview(y)), mods, w_out, gf, w1, w2, *([g_final] if last else []), *s_args)
    return (out if last else st.unview(out)), extra


def _ssm_operators(prow, bt, ct):
    nk = SSM_CHUNK
    half = prow.shape[1] // 2
    lr, li = prow[0:1], prow[1:2]
    dt = jnp.exp(prow[2:3])
    mag = jnp.exp(lr * dt)
    ar, ai = mag * jnp.cos(li * dt), mag * jnp.sin(li * dt)
    den = lr * lr + li * li
    nr, ni = _cmul(ar - 1.0, ai, lr, -li)
    bbr, bbi = _cmul(nr / den, ni / den, bt[0:nk], bt[nk:2 * nk])
    ctr, cti = ct[0:nk], ct[nk:2 * nk]

    pr, pi = [jnp.ones_like(ar)], [jnp.zeros_like(ar)]
    for _ in range(nk):
        r, i = _cmul(pr[-1], pi[-1], ar, ai)
        pr.append(r)
        pi.append(i)
    is_fwd = lax.broadcasted_iota(jnp.int32, (1, 2 * half), 1) < half

    def pw(kf, kb):
        return jnp.where(is_fwd, pr[kf], pr[kb]), jnp.where(is_fwd, pi[kf], pi[kb])

    wp_r, wp_i, ca_r, ca_i, wc_r, wc_i = [], [], [], [], [], []
    for t in range(nk):
        r, i = _cmul(bbr, bbi, *pw(nk - 1 - t, t))
        wp_r.append(r)
        wp_i.append(i)
        r, i = _cmul(ctr, cti, *pw(t, nk - 1 - t))
        ca_r.append(r)
        ca_i.append(i)
        r, i = _cmul(ctr, cti, *pw(t + 1, nk - t))
        wc_r.append(r)
        wc_i.append(-i)
    cat = jnp.concatenate
    wp = cat([cat(wp_r, 0), cat(wp_i, 0)], axis=1).astype(BF16)
    ca = cat([cat(ca_r, 0), cat(ca_i, 0)], axis=1)
    wct = cat([cat(wc_r, 0), cat(wc_i, 0)], axis=1).astype(BF16)

    zero = jnp.zeros_like(bbr)
    lhs = cat([cat([jnp.where(is_fwd, bbr, zero), jnp.where(is_fwd, -bbi, zero)], 1),
               cat([jnp.where(is_fwd, zero, bbr), jnp.where(is_fwd, zero, -bbi)], 1)], 0)
    lhs_h, ca_h = lhs.astype(BF16), ca.astype(BF16)
    lhs_l, ca_l = (lhs - lhs_h.astype(F32)).astype(BF16), (ca - ca_h.astype(F32)).astype(BF16)
    kk = (lax.dot_general(lhs_h, ca_h, NT, preferred_element_type=F32)
          + lax.dot_general(lhs_h, ca_l, NT, preferred_element_type=F32)
          + lax.dot_general(lhs_l, ca_h, NT, preferred_element_type=F32))
    width = nk * SSM_GROUP
    lane = lax.broadcasted_iota(jnp.int32, (nk, width), 1)
    row = lax.broadcasted_iota(jnp.int32, (nk, width), 0)
    d_lanes = cat([prow[3:4], jnp.zeros_like(prow[3:4])], axis=1)
    kf = kk[0:nk] + jnp.where(lane == row, d_lanes, 0.0)
    return kf, kk[nk:2 * nk], wp, wct, pr[nk], pi[nk]


def _toeplitz_shifts():
    width = SSM_CHUNK * SSM_GROUP
    src = jnp.arange(width)[:, None]
    dst = jnp.arange(width)[None, :]
    fwd = [dst == src + SSM_GROUP * s for s in range(SSM_CHUNK)]
    bwd = [src == dst + SSM_GROUP * (SSM_CHUNK - 1 - s) for s in range(SSM_CHUNK)]
    return jnp.concatenate(fwd, axis=1).astype(BF16), jnp.concatenate(bwd, axis=1).astype(BF16)


def _ssm_core_kernel(prow_ref, bt_ref, ct_ref, h0r_ref, h0i_ref, shf_ref, shb_ref, up_ref, us_ref, *rest,
                     p_batch, s_batch, side):
    n_in, n_out = _side_counts(side)
    yp_ref, ys_ref, fr_ref, fi_ref = rest[n_in:n_in + 4]
    atp_scr, ats_scr, tg_scr, wp_scr, wct_scr, u_scr, p_scr, s_scr, q_scr = rest[n_in + 4 + n_out:]
    _side_run(side, rest[:n_in], rest[n_in + 4:n_in + 4 + n_out])
    nk = SSM_CHUNK
    n_groups = LANES // SSM_GROUP
    cb_p, cb_s = atp_scr.shape[2], ats_scr.shape[2]
    w2 = prow_ref.shape[2]
    width = nk * SSM_GROUP
    decay, kfs, kbs = [], [], []
    for g in range(n_groups):
        kf, kb, wp, wct, a_r, a_i = _ssm_operators(prow_ref[g], bt_ref[g], ct_ref[g])
        kfs.append(kf)
        kbs.append(kb)
        wp_scr[g] = wp
        wct_scr[g] = wct
        decay.append((a_r, a_i))
    tz = (jnp.dot(jnp.concatenate(kfs, axis=0).astype(BF16), shf_ref[...], preferred_element_type=F32)
          + jnp.dot(jnp.concatenate(kbs, axis=0).astype(BF16), shb_ref[...], preferred_element_type=F32))
    for g in range(n_groups):
        for s in range(nk):
            tg_scr[g, s * SSM_GROUP:(s + 1) * SSM_GROUP, :] = (
                tz[g * SSM_GROUP:(g + 1) * SSM_GROUP, s * width:(s + 1) * width].astype(BF16))

    is_fwd = lax.broadcasted_iota(jnp.int32, (1, w2), 1) < (w2 // 2)

    def scan(n_tiles, rows, src_scr, carries, mults):
        for j in range(n_tiles):
            lo, hi = j * rows, (n_tiles - 1 - j) * rows
            for g in range(n_groups):
                c_r, c_i = carries[g]
                for c, l0 in ((c_r, 0), (c_i, w2)):
                    if lo < hi:
                        s_scr[g, lo:lo + rows, l0:l0 + w2] = c
                        s_scr[g, hi:hi + rows, l0:l0 + w2] = c
                    else:
                        s_scr[g, lo:lo + rows, l0:l0 + w2] = jnp.where(is_fwd, c, s_scr[g, lo:lo + rows, l0:l0 + w2])
                        s_scr[g, hi:hi + rows, l0:l0 + w2] = jnp.where(is_fwd, s_scr[g, hi:hi + rows, l0:l0 + w2], c)
                s_r = jnp.where(is_fwd, src_scr[g, lo:lo + rows, 0:w2], src_scr[g, hi:hi + rows, 0:w2])
                s_i = jnp.where(is_fwd, src_scr[g, lo:lo + rows, w2:2 * w2], src_scr[g, hi:hi + rows, w2:2 * w2])
                n_r, n_i = _cmul(*mults[g], c_r, c_i)
                carries[g] = (n_r + s_r, n_i + s_i)
        return carries

    def prompt_carry(cb):
        z = jnp.zeros((p_batch, w2), F32)
        return scan(cb // p_batch, p_batch, p_scr, [(z, z)] * n_groups, decay)

    def sample_carry(cb):
        assert 2 * s_batch == SUBLANES
        row = lax.broadcasted_iota(jnp.int32, (SUBLANES, w2), 0)
        keep = jnp.where(is_fwd, 1, 0) == jnp.where(row < s_batch, 1, 0)
        carries, mults = [], []
        for g in range(n_groups):
            a_r, a_i = decay[g]
            p_r, p_i = p_scr[g, 0:cb, 0:w2], p_scr[g, 0:cb, w2:2 * w2]
            ap_r, ap_i = _cmul(a_r, a_i, p_r, p_i)
            q_scr[g, 0:cb, 0:w2] = jnp.where(is_fwd, pltpu.roll(p_r, cb - s_batch, 0), pltpu.roll(p_r, s_batch, 0)) + ap_r
            q_scr[g, 0:cb, w2:2 * w2] = jnp.where(is_fwd, pltpu.roll(p_i, cb - s_batch, 0),
                                                  pltpu.roll(p_i, s_batch, 0)) + ap_i
            h_r, h_i = h0r_ref[g], h0i_ref[g]
            e_r = pltpu.roll(jnp.where(is_fwd, p_r[0:SUBLANES], p_r[cb - SUBLANES:cb]), s_batch, 0)
            e_i = pltpu.roll(jnp.where(is_fwd, p_i[0:SUBLANES], p_i[cb - SUBLANES:cb]), s_batch, 0)
            ah_r, ah_i = _cmul(a_r, a_i, h_r, h_i)
            carries.append((jnp.where(keep, h_r, ah_r + e_r), jnp.where(keep, h_i, ah_i + e_i)))
            mults.append(_cmul(a_r, a_i, a_r, a_i))
        return scan(cb // SUBLANES, SUBLANES, q_scr, carries, mults)

    def mix(u_ref, y_ref, at_scr, cb, carry_fn):
        for t in range(nk):
            at_scr[t] = u_ref[t].T
        for g in range(n_groups):
            a = at_scr[:, g * SSM_GROUP:(g + 1) * SSM_GROUP, :].reshape(nk * SSM_GROUP, cb)
            u = a.T
            u_scr[g, 0:cb, :] = u
            p_scr[g, 0:cb, :] = jnp.dot(u, wp_scr[g], preferred_element_type=F32)
        finals = carry_fn(cb)
        for g in range(n_groups):
            s = s_scr[g, 0:cb, :].astype(BF16)
            y = (jnp.dot(u_scr[g, 0:cb, :], tg_scr[g], preferred_element_type=F32)
                 + lax.dot_general(s, wct_scr[g], NT, preferred_element_type=F32))
            at_scr[:, g * SSM_GROUP:(g + 1) * SSM_GROUP, :] = y.astype(BF16).T.reshape(nk, SSM_GROUP, cb)
        for t in range(nk):
            y_ref[pl.ds(t, cb, stride=nk), :] = at_scr[t].T.astype(F32)
        return finals

    finals = mix(up_ref, yp_ref, atp_scr, cb_p, prompt_carry)
    for g in range(n_groups):
        fr_ref[g] = finals[g][0]
        fi_ref[g] = finals[g][1]
    mix(us_ref, ys_ref, ats_scr, cb_s, sample_carry)


def _ssm_core(prow, bt, ct, h0r, h0i, up, us, p_batch, s_batch, jobs=()):
    groups, _, w2 = prow.shape
    gpb = LANES // SSM_GROUP
    cb_p, cb_s = up.shape[1], us.shape[1]
    np_rows, ns_rows = cb_p * SSM_CHUNK, cb_s * SSM_CHUNK
    cb_max = max(cb_p, cb_s)
    width = SSM_CHUNK * SSM_GROUP
    shf, shb = _toeplitz_shifts()

    def gspec(r, c):
        return pl.BlockSpec((gpb, r, c), lambda o: (o, 0, 0))

    def lane_spec(n):
        return pl.BlockSpec((n, LANES), lambda o: (0, o))

    def in_spec(cb):
        return pl.BlockSpec((SSM_CHUNK, cb, LANES), lambda o: (0, 0, o))

    side, c_in, c_args, c_out, c_shapes = _side_plan(jobs, groups // gpb, lambda o: o)
    yp, ys, f_re, f_im, *cast = pl.pallas_call(
        functools.partial(_ssm_core_kernel, p_batch=p_batch, s_batch=s_batch, side=side),
        grid=(groups // gpb,),
        in_specs=[gspec(SUBLANES, w2), gspec(2 * SSM_GROUP, w2), gspec(2 * SSM_GROUP, w2),
                  gspec(SUBLANES, w2), gspec(SUBLANES, w2), _resident(shf.shape), _resident(shb.shape),
                  in_spec(cb_p), in_spec(cb_s)] + c_in,
        out_specs=[lane_spec(np_rows), lane_spec(ns_rows), gspec(p_batch, w2), gspec(p_batch, w2)] + c_out,
        out_shape=[jax.ShapeDtypeStruct((np_rows, up.shape[2]), F32), jax.ShapeDtypeStruct((ns_rows, us.shape[2]), F32),
                   jax.ShapeDtypeStruct((groups, p_batch, w2), F32),
                   jax.ShapeDtypeStruct((groups, p_batch, w2), F32)] + c_shapes,
        scratch_shapes=[pltpu.VMEM((SSM_CHUNK, LANES, cb_p), BF16), pltpu.VMEM((SSM_CHUNK, LANES, cb_s), BF16)]
        + [pltpu.VMEM((gpb, width, width), BF16)] * 3 + [pltpu.VMEM((gpb, cb_max, width), BF16)]
        + [pltpu.VMEM((gpb, cb_max, 2 * w2), F32)] * 2 + [pltpu.VMEM((gpb, cb_s, 2 * w2), F32)],
        compiler_params=_cparams(fuse_inputs=[True] * 5 + [False] * (4 + len(c_args))), name="ssm_core",
    )(prow, bt, ct, h0r, h0i, shf, shb, up, us, *c_args)
    return yp, ys, f_re, f_im, cast


def _ssm_layer(xp, xs, stp, sts, mods, layer, g_mix, w_in, lam_re, lam_im, log_dt, b_re, b_im, c_re, c_im, d_skip,
               w_out, h0_re, h0_im, ffn, pos=None, g_final=None, jobs=(), late=()):
    width = w_in.shape[1]
    groups = width // SSM_GROUP
    n_state = lam_re.shape[-1]
    up = _ssm_in(xp, stp, mods, layer, g_mix, w_in)
    us = _ssm_in(xs, sts, mods, layer, g_mix, w_in, pos)

    def lanes_dir_state(v):
        return v.transpose(1, 0, 2).reshape(groups, 2 * n_state)

    d_rows = jnp.pad(d_skip.reshape(groups, SSM_GROUP), ((0, 0), (0, 2 * n_state - SSM_GROUP)))
    prow = jnp.stack([lanes_dir_state(lam_re), lanes_dir_state(lam_im),
                      lanes_dir_state(jnp.broadcast_to(log_dt[..., None], lam_re.shape)), d_rows], axis=1)
    prow = jnp.pad(prow, ((0, 0), (0, SUBLANES - 4), (0, 0)))
    bt = jnp.concatenate([b_re.transpose(1, 3, 0, 2).reshape(groups, SSM_GROUP, 2 * n_state),
                          b_im.transpose(1, 3, 0, 2).reshape(groups, SSM_GROUP, 2 * n_state)], axis=1)
    ct = jnp.concatenate([c_re.transpose(1, 2, 0, 3).reshape(groups, SSM_GROUP, 2 * n_state),
                          c_im.transpose(1, 2, 0, 3).reshape(groups, SSM_GROUP, 2 * n_state)], axis=1)

    def h0_rows(h):
        h = h.transpose(2, 0, 1, 3).reshape(groups, sts.nb, 2 * n_state)
        return jnp.concatenate([h] * (SUBLANES // sts.nb), axis=1)

    yp, ys, f_re, f_im, conv = _ssm_core(prow, bt, ct, h0_rows(h0_re), h0_rows(h0_im), up, us, stp.nb, sts.nb, late)
    if late:
        w_out, ffn = conv[0], (ffn[0], conv[1], conv[2])
    xp, cast = _ssm_out_ffn(xp, yp, stp, mods, layer, w_out, *ffn, g_final=g_final, jobs=jobs)
    xs, _ = _ssm_out_ffn(xs, ys, sts, mods, layer, w_out, *ffn, pos=pos, g_final=g_final)

    def final(f):
        return f.reshape(groups, stp.nb, 2, n_state).transpose(1, 2, 0, 3)

    return xp, xs, final(f_re), final(f_im), cast


def _seq_rows(ref, k):
    return ref[:, k].reshape(-1, ref.shape[-1])


def _gmlp_kernel(x_ref, mod_ref, g_ref, win_ref, ws_ref, bs_ref, wout_ref, *rest, side):
    n_in, n_out = _side_counts(side)
    o_ref, t_scr = rest[n_in], rest[n_in + 1 + n_out]
    _side_run(side, rest[:n_in], rest[n_in + 1:n_in + 1 + n_out])
    n_groups, chunk, gd = bs_ref.shape
    xs, us, vns = [], [], []
    for k in range(SEQ_PAIR):
        x = _seq_rows(x_ref, k)
        mod = mod_ref[k]
        hn = _norm_mod(x, g_ref[...], mod[0:1], mod[1:2]).astype(BF16)
        z = _gelu(jnp.dot(hn, win_ref[...], preferred_element_type=F32))
        wdt = z.shape[1] // 2
        u, v = z[:, :wdt], z[:, wdt:]
        vc = v - jnp.mean(v, axis=-1, keepdims=True)
        xs.append(x)
        us.append(u)
        vns.append((vc * lax.rsqrt(jnp.mean(vc * vc, axis=-1, keepdims=True) + EPS)).astype(BF16))
    n_chunks = xs[0].shape[0] // chunk
    for g in range(n_groups):
        cols = slice(g * gd, (g + 1) * gd)
        rhs = jnp.concatenate([vn[c * chunk:(c + 1) * chunk, cols] for vn in vns for c in range(n_chunks)], axis=1)
        s = jnp.dot(ws_ref[g], rhs, preferred_element_type=F32)
        for k in range(SEQ_PAIR):
            for c in range(n_chunks):
                rows = slice(c * chunk, (c + 1) * chunk)
                i = k * n_chunks + c
                t_scr[k, rows, cols] = (us[k][rows, cols] * (s[:, i * gd:(i + 1) * gd] + bs_ref[g])).astype(BF16)
    for k in range(SEQ_PAIR):
        out = jnp.dot(t_scr[k], wout_ref[...], preferred_element_type=F32)
        o_ref[:, k] = (xs[k] + mod_ref[k][2:3] * out).reshape(o_ref.shape[0], o_ref.shape[2], o_ref.shape[3])


def _gmlp_layer(x, st, mods, layer, g_mix, w_in, w_s, bs, w_out, jobs=()):
    d = x.shape[-1]
    side, s_in, s_args, s_out, s_shapes = st.side_plan(jobs, seq=True)
    out, *extra = pl.pallas_call(
        functools.partial(_gmlp_kernel, side=side),
        grid=st.seq_grid(),
        in_specs=[st.seq_spec(d), st.seq_mod_spec(layer, d), _resident((1, d)), _resident(w_in.shape),
                  _resident(w_s.shape), _resident(bs.shape), _resident(w_out.shape)] + s_in,
        out_specs=[st.seq_spec(d)] + s_out,
        out_shape=[jax.ShapeDtypeStruct(x.shape, F32)] + s_shapes,
        scratch_shapes=[pltpu.VMEM((SEQ_PAIR, SEQ_CHUNKS * SSM_CHUNK, w_out.shape[0]), BF16)],
        compiler_params=_cparams(2), name="gmlp",
    )(x, mods, g_mix, w_in, w_s, bs, w_out, *s_args)
    return out, extra


def _conv_ffn_kernel(x_ref, xprev_ref, xnext_ref, mod_ref, g_ref, win_ref, cw_ref, wout_ref, gf_ref, w1_ref, w2_ref,
                     *rest, seq_blocks, th, side):
    n_in, _ = _side_counts(side)
    o_ref = rest[n_in]
    j = pl.program_id(1)
    g = g_ref[...]
    cw = cw_ref[...]
    d = x_ref.shape[-1]
    xs = [_seq_rows(x_ref, k) for k in range(SEQ_PAIR)]
    mods = [mod_ref[k] for k in range(SEQ_PAIR)]
    if seq_blocks > 1:
        hh = jnp.concatenate([_norm_mod(jnp.concatenate([xprev_ref[k], xnext_ref[k]], axis=0), g, m[0:1], m[1:2])
                              for k, m in enumerate(mods)], axis=0).astype(BF16)
        zh = jnp.dot(hh, win_ref[:, d:], preferred_element_type=F32)
        th_all = zh[:, :d] * zh[:, d:]
    else:
        th_all = jnp.zeros((2 * SUBLANES * SEQ_PAIR, d), F32)
    mids = []
    for k, (x, mod) in enumerate(zip(xs, mods)):
        tm = x.shape[0]
        hn = _norm_mod(x, g, mod[0:1], mod[1:2]).astype(BF16)
        z = jnp.dot(hn, win_ref[...], preferred_element_type=F32)
        gb, t = z[:, :d], z[:, d:2 * d] * z[:, 2 * d:]
        r0 = 2 * SUBLANES * k
        t_prev = jnp.where(j > 0, th_all[r0 + SUBLANES - 1:r0 + SUBLANES], 0.0)
        t_next = jnp.where(j < seq_blocks - 1, th_all[r0 + SUBLANES:r0 + SUBLANES + 1], 0.0)
        row = lax.broadcasted_iota(jnp.int32, (tm, 1), 0)
        up = jnp.where(row == 0, t_prev, pltpu.roll(t, 1, 0))
        dn = jnp.where(row == tm - 1, t_next, pltpu.roll(t, tm - 1, 0))
        y = cw[0:1] * up + cw[1:2] * t + cw[2:3] * dn
        out = jnp.dot((gb * y).astype(BF16), wout_ref[...], preferred_element_type=F32)
        mids.append(x + mod[2:3] * out)
    outs = _ffn_blocks(mids, mods, gf_ref[...], w1_ref, w2_ref, th)
    for k, out in enumerate(outs):
        o_ref[:, k] = out.reshape(o_ref.shape[0], o_ref.shape[2], o_ref.shape[3])
    _side_run(side, rest[:n_in], rest[n_in + 1:])


def _conv_ffn_layer(x, st, mods, layer, g_mix, w_in, cw, w_out, gf, w1, w2, jobs=(), th=1024):
    d = x.shape[-1]
    prev, nxt = st.halo_specs(d)
    xh = st.halo_view(x)
    side, c_in, c_args, c_out, c_shapes = st.side_plan(jobs, seq=True)
    out, *cast = pl.pallas_call(
        functools.partial(_conv_ffn_kernel, seq_blocks=st.seq_blocks, th=th, side=side),
        grid=st.seq_grid(),
        in_specs=[st.seq_spec(d), prev, nxt, st.seq_mod_spec(layer, d), _resident((1, d)), _resident(w_in.shape),
                  _resident(cw.shape), _resident(w_out.shape), _resident((1, d)), _resident(w1.shape),
                  _resident(w2.shape)] + c_in,
        out_specs=[st.seq_spec(d)] + c_out,
        out_shape=[jax.ShapeDtypeStruct(x.shape, F32)] + c_shapes,
        compiler_params=_cparams(2), name="conv_ffn",
    )(x, xh, xh, mods, g_mix, w_in, cw, w_out, gf, w1, w2, *c_args)
    return out, cast


def kernel(x_prompt, x_sample, state_ssm_re, state_ssm_im, c, c_ctx, w_mod, b_mod, g_mix, g_ffn, ffn_w1, ffn_w2, ssm_w_in, ssm_lam_re, ssm_lam_im, ssm_log_dt, ssm_b_re, ssm_b_im, ssm_c_re, ssm_c_im, ssm_d, ssm_w_out, gmlp_w_in, gmlp_w_s, gmlp_b_s, gmlp_w_out, conv_w_in, conv_w, conv_w_out, g_final):
    n_prompt, len_prompt, d = x_prompt.shape
    n_sample, len_sample, _ = x_sample.shape
    depth = w_mod.shape[0]
    n_mixers = 3
    assert 1 + n_sample <= SUBLANES
    stp = _Stream(len_prompt // SSM_CHUNK, n_prompt, per_batch_cond=False)
    sts = _Stream(len_sample // SSM_CHUNK, n_sample, per_batch_cond=True)
    sts_ffn = _Stream(len_sample // SSM_CHUNK, n_sample, per_batch_cond=True, row_block=FFN_ROW_BLOCK)

    xp, xs = x_prompt, x_sample
    pos = _grid_pos_embed(len_sample, d)

    cond_t = jnp.concatenate([c_ctx[None, :], c, jnp.zeros((SUBLANES - 1 - n_sample, d), F32)], axis=0).T
    n_cond = 1 + n_sample
    table = jnp.array([0] * CTX_ROWS + list(range(1, n_cond)))

    def mod_table(m):
        return m[table].reshape(1, table.shape[0], N_MOD, d)

    assert (depth - 1) % n_mixers == 0, "the last layer's kernel writes the natural-order outputs"

    def layer_weights(i):
        kind, j = i % n_mixers, i // n_mixers
        mixer = {0: [(ssm_w_in, j), (ssm_w_out, j)],
                 1: [(gmlp_w_in, j), (gmlp_w_s.reshape(gmlp_w_s.shape[0], -1, gmlp_w_s.shape[-1]), j), (gmlp_w_out, j)],
                 2: [(conv_w_in, j), (conv_w_out, j)]}[kind]
        return [("cast", stack, layer) for stack, layer in mixer + [(ffn_w1, i), (ffn_w2, i)]]

    def adaln_jobs(layers):
        return [("adaln", cond_t, w_mod, b_mod, l, n_cond) for l in layers]

    first = layer_weights(0)
    wb = [first[0][1][first[0][2]].astype(BF16)] + [None] * (len(first) - 1)
    mods = {0: mod_table(_adaln(cond_t, w_mod, b_mod, n_cond, [0])[0])}
    new_re, new_im = [], []
    for i in range(depth):
        kind, j = i % n_mixers, i // n_mixers
        gm = g_mix[i].reshape(1, d)
        nxt = layer_weights(i + 1) if i + 1 < depth else []
        *wm, w1, w2 = wb
        ffn = (g_ffn[i].reshape(1, d), w1, w2)
        m = mods[i]
        if kind == 0:
            later = adaln_jobs([i + 1]) if i + 1 < depth else []
            xp, xs, f_re, f_im, extra = _ssm_layer(
                xp, xs, stp, sts, m, 0, gm, wm[0], ssm_lam_re[j], ssm_lam_im[j], ssm_log_dt[j],
                ssm_b_re[j], ssm_b_im[j], ssm_c_re[j], ssm_c_im[j], ssm_d[j], wm[1],
                state_ssm_re[:, j], state_ssm_im[:, j], ffn, pos if i == 0 else None,
                g_final.reshape(1, d) if i == depth - 1 else None, nxt + later, first[1:] if i == 0 else ())
            wb = extra[:len(nxt)]
            for job, out in zip(later, extra[len(nxt):]):
                mods[job[4]] = mod_table(out)
            new_re.append(f_re)
            new_im.append(f_im)
        elif kind == 1:
            n_groups, chunk, _ = gmlp_w_s[j].shape
            gd = gmlp_w_out[j].shape[0] // n_groups
            bs = jnp.broadcast_to(gmlp_b_s[j][:, :, None], (n_groups, chunk, gd))
            ws = (wm[0], wm[1].reshape(gmlp_w_s[j].shape), bs, wm[2])
            later = adaln_jobs([i + 1]) if i + 1 < depth else []
            xp, extra = _ffn(_gmlp_layer(xp, stp, m, 0, gm, *ws)[0], stp, m, 0, *ffn, jobs=nxt + later)
            wb = extra[:len(nxt)]
            for job, out in zip(later, extra[len(nxt):]):
                mods[job[4]] = mod_table(out)
            xs, _ = _ffn(_gmlp_layer(xs, sts, m, 0, gm, *ws)[0], sts_ffn, m, 0, *ffn)
        else:
            ws = (wm[0], jnp.pad(conv_w[j], ((0, SUBLANES - CONV_WIDTH), (0, 0))), wm[1])
            later = adaln_jobs([i + 1]) if i + 1 < depth else []
            xp, extra = _conv_ffn_layer(xp, stp, m, 0, gm, *ws, *ffn, jobs=nxt + later)
            wb = extra[:len(nxt)]
            for job, out in zip(later, extra[len(nxt):]):
                mods[job[4]] = mod_table(out)
            xs, _ = _conv_ffn_layer(xs, sts, m, 0, gm, *ws, *ffn)

    return (xp.reshape(x_prompt.shape), xs.reshape(x_sample.shape),
            jnp.stack(new_re, axis=1), jnp.stack(new_im, axis=1))
```

```python
import functools
import math

import jax
import jax.numpy as jnp
from jax import lax
from jax.experimental import pallas as pl
from jax.experimental.pallas import tpu as pltpu

EPS = 1e-6
N_MOD = 6
GRID_W = 64
SSM_GROUP = 16
SSM_CHUNK = 16
CONV_WIDTH = 3
SUBLANES = 8
LANES = 128
ROW_BLOCK = 512
FFN_ROW_BLOCK = 1024
SEQ_CHUNKS = 16
SEQ_PAIR = 2
CTX_ROWS = 4
VMEM_LIMIT = 56 * 1024 * 1024

F32 = jnp.float32
BF16 = jnp.bfloat16
NT = (((1,), (1,)), ((), ()))


def _cparams(n_axes=1, vmem=VMEM_LIMIT):
    return pltpu.CompilerParams(dimension_semantics=("arbitrary",) * n_axes, vmem_limit_bytes=vmem)


def _gelu(x):
    return 0.5 * x * (1.0 + jnp.tanh(math.sqrt(2.0 / math.pi) * (x + 0.044715 * (x * x * x))))


def _norm_mod(x, g, shift, scale):
    y = x * lax.rsqrt(jnp.mean(x * x, axis=-1, keepdims=True) + EPS)
    return (y * g) * (1.0 + scale) + shift


def _cmul(ar, ai, br, bi):
    return ar * br - ai * bi, ar * bi + ai * br


def _resident(shape):
    nd = len(shape)
    return pl.BlockSpec(shape, lambda *_: (0,) * nd, pipeline_mode=pl.Buffered(1))


def _side_plan(jobs, n_steps, step_of):
    kinds, ins, args, outs, shapes = [], [], [], [], []
    for job in jobs:
        if job[0] == "cast":
            _, stack, layer = job
            _, r, c = stack.shape
            blk = r // n_steps
            assert r % n_steps == 0 and blk % (2 * SUBLANES) == 0
            kinds.append(("cast", 1, None))
            ins.append(pl.BlockSpec((None, blk, c), lambda *ids, layer=layer: (layer, step_of(*ids), 0)))
            args.append(stack)
            outs.append(pl.BlockSpec((blk, c), lambda *ids: (step_of(*ids), 0)))
            shapes.append(jax.ShapeDtypeStruct((r, c), BF16))
        else:
            _, cond_t, w_mod, b_mod, layer, n_cond = job
            _, d, n = w_mod.shape
            tn = n // n_steps
            assert n % n_steps == 0 and tn % LANES == 0
            kinds.append(("adaln", 3, n_cond))
            ins += [_resident(cond_t.shape),
                    pl.BlockSpec((None, d, tn), lambda *ids, layer=layer: (layer, 0, step_of(*ids))),
                    pl.BlockSpec((None, 1, tn), lambda *ids, layer=layer: (layer, 0, step_of(*ids)))]
            args += [cond_t, w_mod, b_mod.reshape(b_mod.shape[0], 1, n)]
            outs.append(pl.BlockSpec((SUBLANES, tn), lambda *ids: (0, step_of(*ids))))
            shapes.append(jax.ShapeDtypeStruct((SUBLANES, n), F32))
    return tuple(kinds), ins, args, outs, shapes


def _side_counts(kinds):
    return sum(k[1] for k in kinds), len(kinds)


def _side_run(kinds, in_refs, out_refs):
    i = 0
    for (name, n_in, param), dst in zip(kinds, out_refs):
        if name == "cast":
            dst[...] = in_refs[i][...].astype(BF16)
        else:
            dst[...] = _adaln_rows(in_refs[i][...], in_refs[i + 1][...], param) + in_refs[i + 2][...]
        i += n_in


def _adaln_rows(ct, w, n_cond):
    s = ct * jax.nn.sigmoid(ct)
    rows = [jnp.sum(s[:, r:r + 1] * w, axis=0, keepdims=True) for r in range(n_cond)]
    rows += [jnp.zeros_like(rows[0])] * (SUBLANES - n_cond)
    return jnp.concatenate(rows, axis=0)


class _Stream:
    def __init__(self, n_chunks, n_batch, per_batch_cond, row_block=ROW_BLOCK):
        self.nc, self.nb = n_chunks, n_batch
        self.n_rows = n_chunks * n_batch * SSM_CHUNK
        self.per_batch_cond = per_batch_cond
        self.row_block = row_block
        self.row_chunks = row_block // SSM_CHUNK
        self.seq_blocks = n_chunks // SEQ_CHUNKS
        assert n_chunks % SEQ_CHUNKS == 0 and self.n_rows % row_block == 0 and n_batch % SEQ_PAIR == 0
        assert not per_batch_cond or n_chunks % self.row_chunks == 0

    def row_grid(self):
        return (self.nb, self.nc // self.row_chunks) if self.per_batch_cond else (self.n_rows // self.row_block,)

    def row_view(self, x):
        return x if self.per_batch_cond else x.reshape(self.n_rows, x.shape[-1])

    def row_spec(self, width):
        if self.per_batch_cond:
            return pl.BlockSpec((self.row_chunks, None, SSM_CHUNK, width), lambda b, j: (j, b, 0, 0))
        return pl.BlockSpec((self.row_block, width), lambda i: (i, 0))

    def row_shape(self, width):
        return (self.nc, self.nb, SSM_CHUNK, width) if self.per_batch_cond else (self.n_rows, width)

    def unview(self, x):
        return x.reshape(self.nc, self.nb, SSM_CHUNK, x.shape[-1])

    def nat_view(self, x):
        return x.reshape(self.nb, self.nc, SSM_CHUNK, x.shape[-1])

    def nat_spec(self, width):
        if self.per_batch_cond:
            return pl.BlockSpec((None, self.row_chunks, SSM_CHUNK, width), lambda b, j: (b, j, 0, 0))
        assert self.nb * SSM_CHUNK == self.row_block
        return pl.BlockSpec((self.nb, None, SSM_CHUNK, width), lambda i: (0, i, 0, 0))

    def pos_spec(self, width):
        return pl.BlockSpec((self.row_chunks, SSM_CHUNK, width), lambda b, j: (j, 0, 0))

    def seq_grid(self):
        return (self.nb // SEQ_PAIR, self.seq_blocks)

    def seq_spec(self, width):
        return pl.BlockSpec((SEQ_CHUNKS, SEQ_PAIR, SSM_CHUNK, width), lambda b, j: (j, b, 0, 0))

    def halo_specs(self, width):
        halves = SSM_CHUNK // SUBLANES
        shape = (None, SEQ_PAIR, None, SUBLANES, width)
        prev = pl.BlockSpec(shape, lambda b, j: (jnp.maximum(j * SEQ_CHUNKS - 1, 0), b, halves - 1, 0, 0))
        nxt = pl.BlockSpec(shape, lambda b, j: (jnp.minimum((j + 1) * SEQ_CHUNKS, self.nc - 1), b, 0, 0, 0))
        return prev, nxt

    def halo_view(self, x):
        return x.reshape(self.nc, self.nb, SSM_CHUNK // SUBLANES, SUBLANES, x.shape[-1])

    def mod_spec(self, layer, d):
        if self.per_batch_cond:
            return pl.BlockSpec((None, None, N_MOD, d), lambda b, j: (layer, CTX_ROWS + b, 0, 0))
        return pl.BlockSpec((None, None, N_MOD, d), lambda *_: (layer, 0, 0, 0))

    def seq_mod_spec(self, layer, d):
        if self.per_batch_cond:
            return pl.BlockSpec((None, SEQ_PAIR, N_MOD, d), lambda b, j: (layer, CTX_ROWS // SEQ_PAIR + b, 0, 0))
        return pl.BlockSpec((None, SEQ_PAIR, N_MOD, d), lambda *_: (layer, 0, 0, 0))

    def n_axes(self, seq=False):
        return 2 if (seq or self.per_batch_cond) else 1

    def side_plan(self, jobs, seq=False):
        if seq:
            return _side_plan(jobs, (self.nb // SEQ_PAIR) * self.seq_blocks, lambda b, j: b * self.seq_blocks + j)
        assert not jobs or not self.per_batch_cond
        return _side_plan(jobs, self.n_rows // self.row_block, lambda i: i)


def _grid_pos_embed(n_tokens, d):
    rows = n_tokens // GRID_W
    quarter = d // 4
    freq = 1.0 / (10000.0 ** (jnp.arange(quarter, dtype=F32) / quarter))
    ar = jnp.arange(rows, dtype=F32)[:, None] * freq
    ac = jnp.arange(GRID_W, dtype=F32)[:, None] * freq
    row_part = jnp.concatenate([jnp.sin(ar), jnp.cos(ar)], axis=-1)
    col_part = jnp.concatenate([jnp.sin(ac), jnp.cos(ac)], axis=-1)
    table = jnp.concatenate([jnp.broadcast_to(row_part[:, None], (rows, GRID_W, d // 2)),
                             jnp.broadcast_to(col_part[None], (rows, GRID_W, d // 2))], axis=-1)
    return table.reshape(rows * GRID_W, d)


def _adaln_kernel(ct_ref, w_ref, b_ref, o_ref, *, n_cond):
    o_ref[...] = _adaln_rows(ct_ref[...], w_ref[...], n_cond) + b_ref[...]


def _adaln(cond_t, w_mod, b_mod, n_cond, layers, tn=1024):
    depth, d, n = w_mod.shape
    l0 = layers[0]
    assert list(layers) == list(range(l0, l0 + len(layers)))
    return pl.pallas_call(
        functools.partial(_adaln_kernel, n_cond=n_cond),
        grid=(len(layers), n // tn),
        in_specs=[
            pl.BlockSpec((d, SUBLANES), lambda l, j: (0, 0)),
            pl.BlockSpec((None, d, tn), lambda l, j: (l + l0, 0, j)),
            pl.BlockSpec((None, 1, tn), lambda l, j: (l + l0, 0, j)),
        ],
        out_specs=pl.BlockSpec((None, SUBLANES, tn), lambda l, j: (l, 0, j)),
        out_shape=jax.ShapeDtypeStruct((len(layers), SUBLANES, n), F32),
        compiler_params=_cparams(2), name="adaln",
    )(cond_t, w_mod, b_mod.reshape(depth, 1, n))


def _rows(ref):
    return ref[...].reshape(-1, ref.shape[-1])


def _rms(x, g):
    return (x * lax.rsqrt(jnp.mean(x * x, axis=-1, keepdims=True) + EPS)) * g


def _ffn_blocks(xs, mods, g, w1_ref, w2_ref, th, g_final=None):
    hn = jnp.concatenate([_norm_mod(x, g, m[3:4], m[4:5]).astype(BF16) for x, m in zip(xs, mods)], axis=0)
    acc = jnp.zeros((hn.shape[0], w2_ref.shape[1]), F32)
    for c in range(w1_ref.shape[1] // th):
        h1 = jnp.dot(hn, w1_ref[:, c * th:(c + 1) * th], preferred_element_type=F32)
        h1 = jnp.square(jnp.maximum(h1, 0.0)).astype(BF16)
        acc = acc + jnp.dot(h1, w2_ref[c * th:(c + 1) * th, :], preferred_element_type=F32)
    outs, r0 = [], 0
    for x, m in zip(xs, mods):
        out = x + m[5:6] * acc[r0:r0 + x.shape[0]]
        outs.append(out if g_final is None else _rms(out, g_final))
        r0 += x.shape[0]
    return outs


def _ffn_kernel(x_ref, mod_ref, g_ref, w1_ref, w2_ref, *rest, th, side):
    n_in, _ = _side_counts(side)
    o_ref = rest[n_in]
    out, = _ffn_blocks([_rows(x_ref)], [mod_ref[...]], g_ref[...], w1_ref, w2_ref, th)
    o_ref[...] = out.reshape(o_ref.shape)
    _side_run(side, rest[:n_in], rest[n_in + 1:])


def _ffn(x, st, mods, layer, g, w1, w2, jobs=(), th=1024):
    d = x.shape[-1]
    side, s_in, s_args, s_out, s_shapes = st.side_plan(jobs)
    out, *extra = pl.pallas_call(
        functools.partial(_ffn_kernel, th=th, side=side),
        grid=st.row_grid(),
        in_specs=[st.row_spec(d), st.mod_spec(layer, d), _resident((1, d)), _resident(w1.shape), _resident(w2.shape)]
        + s_in,
        out_specs=[st.row_spec(d)] + s_out,
        out_shape=[jax.ShapeDtypeStruct(st.row_shape(d), F32)] + s_shapes,
        compiler_params=_cparams(st.n_axes()), name="ffn",
    )(st.row_view(x), mods, g, w1, w2, *s_args)
    return st.unview(out), extra


def _x_operands(x, st, pos):
    d = x.shape[-1]
    if x.ndim == 4:
        return [st.row_spec(d)], [st.row_view(x)]
    specs, args = [st.nat_spec(d)], [st.nat_view(x)]
    if pos is not None:
        specs.append(st.pos_spec(d))
        args.append(pos.reshape(st.nc, SSM_CHUNK, d))
    return specs, args


Y_PITCH = SSM_CHUNK + 4
PERM_SPLIT = 2


def _t_major_perm(n_batch):
    rows = ROW_BLOCK // PERM_SPLIT
    n_chunks = rows // (n_batch * SSM_CHUNK)
    assert n_chunks * n_batch == 2 * SUBLANES
    src = jnp.arange(rows)
    b, c, t = src // (n_chunks * SSM_CHUNK), (src // SSM_CHUNK) % n_chunks, src % SSM_CHUNK
    dst = (t * n_chunks + c) * n_batch + b
    return (jnp.arange(rows)[:, None] == dst[None, :]).astype(BF16)


def _ssm_in_kernel(x_ref, *rest, n_batch, natural, n_sub):
    *pos_ref, mod_ref, g_ref, perm_ref, w_ref, o_ref = rest
    g = g_ref[...]
    d = x_ref.shape[-1]
    n_t = o_ref.shape[0]
    group = o_ref.shape[1] // n_sub
    piece = group // PERM_SPLIT
    for h in range(n_sub):
        parts = []
        for s in range(PERM_SPLIT):
            if n_batch == 1:
                mod = mod_ref[...]
                xs = x_ref[s * piece:(s + 1) * piece, h] if natural else x_ref[h, s * piece:(s + 1) * piece]
                hn = _norm_mod(xs.reshape(-1, d), g, mod[0:1], mod[1:2]).astype(BF16)
            else:
                cpp = piece // n_batch
                cs = slice((h * PERM_SPLIT + s) * cpp, (h * PERM_SPLIT + s + 1) * cpp)
                rows = []
                for b in range(n_batch):
                    xb = (x_ref[b, cs] if natural else x_ref[cs, b]).reshape(-1, d)
                    if pos_ref:
                        xb = xb + pos_ref[0][cs].reshape(-1, d)
                    mod = mod_ref[b]
                    rows.append(_norm_mod(xb, g, mod[0:1], mod[1:2]).astype(BF16))
                hn = jnp.concatenate(rows, axis=0)
            parts.append(jnp.dot(perm_ref[...], hn, preferred_element_type=F32).astype(BF16))
        hn = jnp.concatenate([p[t * piece:(t + 1) * piece] for t in range(n_t) for p in parts], axis=0)
        u = jnp.dot(hn, w_ref[...], preferred_element_type=F32)
        o_ref[:, h * group:(h + 1) * group, :] = u.astype(BF16).reshape(n_t, group, o_ref.shape[2])


def _ssm_in(x, st, mods, layer, g, w, pos=None, n_sub=2):
    d, width = x.shape[-1], w.shape[1]
    natural = x.ndim != 4
    group = ROW_BLOCK // SSM_CHUNK
    if not st.per_batch_cond:
        assert st.nb == group and st.nc % n_sub == 0
        cps, n_batch = n_sub, 1
        mod_spec = st.mod_spec(layer, d)
    else:
        assert group % st.nb == 0 and CTX_ROWS % st.nb == 0
        cps, n_batch = n_sub * (group // st.nb), st.nb
        assert st.nc % cps == 0
        mod_spec = pl.BlockSpec((None, st.nb, N_MOD, d), lambda i: (layer, CTX_ROWS // st.nb, 0, 0))
    if natural:
        x_specs, x_args = [pl.BlockSpec((st.nb, cps, SSM_CHUNK, d), lambda i: (0, i, 0, 0))], [st.nat_view(x)]
        if pos is not None:
            x_specs.append(pl.BlockSpec((cps, SSM_CHUNK, d), lambda i: (i, 0, 0)))
            x_args.append(pos.reshape(st.nc, SSM_CHUNK, d))
    else:
        x_specs, x_args = [pl.BlockSpec((cps, st.nb, SSM_CHUNK, d), lambda i: (i, 0, 0, 0))], [x]
    perm = _t_major_perm(min(st.nb, group // PERM_SPLIT))
    return pl.pallas_call(
        functools.partial(_ssm_in_kernel, n_batch=n_batch, natural=natural, n_sub=n_sub),
        grid=(st.nc // cps,),
        in_specs=x_specs + [mod_spec, _resident((1, d)), _resident(perm.shape), _resident(w.shape)],
        out_specs=pl.BlockSpec((SSM_CHUNK, n_sub * group, width), lambda i: (0, i, 0)),
        out_shape=jax.ShapeDtypeStruct((SSM_CHUNK, st.nc * st.nb, width), BF16),
        compiler_params=_cparams(1), name="ssm_in",
    )(*x_args, mods, g, perm, w)


def _ssm_out_ffn_kernel(x_ref, *rest, has_pos, has_final, th, side):
    pos_ref = rest[0] if has_pos else None
    y_ref, mod_ref, wout_ref, gf_ref, w1_ref, w2_ref = rest[has_pos:has_pos + 6]
    n_in = has_pos + 6 + has_final
    n_side, _ = _side_counts(side)
    g_final = rest[n_in - 1][...] if has_final else None
    o_ref = rest[n_in + n_side]
    x = _rows(x_ref) + _rows(pos_ref) if has_pos else _rows(x_ref)
    y = _rows(y_ref)
    mod = mod_ref[...]
    d = x.shape[-1]
    half = x.shape[0] // 2
    mids = []
    for k in range(2):
        rows = slice(k * half, (k + 1) * half)
        ag = jnp.dot(_gelu(y[rows]).astype(BF16), wout_ref[...], preferred_element_type=F32)
        mids.append(x[rows] + mod[2:3] * (ag[:, :d] * jax.nn.sigmoid(ag[:, d:])))
    outs = _ffn_blocks(mids, [mod, mod], gf_ref[...], w1_ref, w2_ref, th, g_final)
    o_ref[...] = jnp.concatenate(outs, axis=0).reshape(o_ref.shape)
    _side_run(side, rest[n_in:n_in + n_side], rest[n_in + n_side + 1:])


def _ssm_out_ffn(x, y, st, mods, layer, w_out, gf, w1, w2, pos=None, g_final=None, jobs=(), th=1024):
    d = x.shape[-1]
    x_specs, x_args = _x_operands(x, st, pos)
    last = g_final is not None
    side, s_in, s_args, s_out, s_shapes = st.side_plan(jobs)
    out, *extra = pl.pallas_call(
        functools.partial(_ssm_out_ffn_kernel, has_pos=len(x_args) == 2, has_final=last, th=th, side=side),
        grid=st.row_grid(),
        in_specs=x_specs + [st.row_spec(d), st.mod_spec(layer, d), _resident(w_out.shape), _resident((1, d)),
                            _resident(w1.shape), _resident(w2.shape)] + ([_resident((1, d))] if last else []) + s_in,
        out_specs=[st.nat_spec(d) if last else st.row_spec(d)] + s_out,
        out_shape=[jax.ShapeDtypeStruct((st.nb, st.nc, SSM_CHUNK, d) if last else st.row_shape(d), F32)] + s_shapes,
        compiler_params=_cparams(st.n_axes()), name="ssm_out_ffn",
    )(*x_args, st.row_view(st.unview(y)), mods, w_out, gf, w1, w2, *([g_final] if last else []), *s_args)
    return (out if last else st.unview(out)), extra


def _ssm_operators(prow, bt, ct):
    nk = SSM_CHUNK
    half = prow.shape[1] // 2
    lr, li = prow[0:1], prow[1:2]
    dt = jnp.exp(prow[2:3])
    mag = jnp.exp(lr * dt)
    ar, ai = mag * jnp.cos(li * dt), mag * jnp.sin(li * dt)
    den = lr * lr + li * li
    nr, ni = _cmul(ar - 1.0, ai, lr, -li)
    bbr, bbi = _cmul(nr / den, ni / den, bt[0:nk], bt[nk:2 * nk])
    ctr, cti = ct[0:nk], ct[nk:2 * nk]

    pr, pi = [jnp.ones_like(ar)], [jnp.zeros_like(ar)]
    for _ in range(nk):
        r, i = _cmul(pr[-1], pi[-1], ar, ai)
        pr.append(r)
        pi.append(i)
    is_fwd = lax.broadcasted_iota(jnp.int32, (1, 2 * half), 1) < half

    def pw(kf, kb):
        return jnp.where(is_fwd, pr[kf], pr[kb]), jnp.where(is_fwd, pi[kf], pi[kb])

    wp_r, wp_i, ca_r, ca_i, wc_r, wc_i = [], [], [], [], [], []
    for t in range(nk):
        r, i = _cmul(bbr, bbi, *pw(nk - 1 - t, t))
        wp_r.append(r)
        wp_i.append(i)
        r, i = _cmul(ctr, cti, *pw(t, nk - 1 - t))
        ca_r.append(r)
        ca_i.append(i)
        r, i = _cmul(ctr, cti, *pw(t + 1, nk - t))
        wc_r.append(r)
        wc_i.append(-i)
    cat = jnp.concatenate
    wp = cat([cat(wp_r, 0), cat(wp_i, 0)], axis=1).astype(BF16)
    ca = cat([cat(ca_r, 0), cat(ca_i, 0)], axis=1)
    wct = cat([cat(wc_r, 0), cat(wc_i, 0)], axis=1).astype(BF16)

    zero = jnp.zeros_like(bbr)
    lhs = cat([cat([jnp.where(is_fwd, bbr, zero), jnp.where(is_fwd, -bbi, zero)], 1),
               cat([jnp.where(is_fwd, zero, bbr), jnp.where(is_fwd, zero, -bbi)], 1)], 0)
    lhs_h, ca_h = lhs.astype(BF16), ca.astype(BF16)
    lhs_l, ca_l = (lhs - lhs_h.astype(F32)).astype(BF16), (ca - ca_h.astype(F32)).astype(BF16)
    kk = (lax.dot_general(lhs_h, ca_h, NT, preferred_element_type=F32)
          + lax.dot_general(lhs_h, ca_l, NT, preferred_element_type=F32)
          + lax.dot_general(lhs_l, ca_h, NT, preferred_element_type=F32))
    width = nk * SSM_GROUP
    lane = lax.broadcasted_iota(jnp.int32, (nk, width), 1)
    row = lax.broadcasted_iota(jnp.int32, (nk, width), 0)
    d_lanes = cat([prow[3:4], jnp.zeros_like(prow[3:4])], axis=1)
    kf = kk[0:nk] + jnp.where(lane == row, d_lanes, 0.0)
    return kf, kk[nk:2 * nk], wp, wct, pr[nk], pi[nk]


def _toeplitz_shifts():
    width = SSM_CHUNK * SSM_GROUP
    src = jnp.arange(width)[:, None]
    dst = jnp.arange(width)[None, :]
    fwd = [dst == src + SSM_GROUP * s for s in range(SSM_CHUNK)]
    bwd = [src == dst + SSM_GROUP * (SSM_CHUNK - 1 - s) for s in range(SSM_CHUNK)]
    return jnp.concatenate(fwd, axis=1).astype(BF16), jnp.concatenate(bwd, axis=1).astype(BF16)


def _ssm_core_kernel(prow_ref, bt_ref, ct_ref, h0r_ref, h0i_ref, shf_ref, shb_ref, up_ref, us_ref, *rest,
                     p_batch, s_batch, side):
    n_in, n_out = _side_counts(side)
    yp_ref, ys_ref, fr_ref, fi_ref = rest[n_in:n_in + 4]
    atp_scr, ats_scr, tg_scr, wp_scr, wct_scr, u_scr, p_scr, s_scr, q_scr, yp_scr = rest[n_in + 4 + n_out:]
    _side_run(side, rest[:n_in], rest[n_in + 4:n_in + 4 + n_out])
    nk = SSM_CHUNK
    n_groups = LANES // SSM_GROUP
    cb_p, cb_s = atp_scr.shape[2], ats_scr.shape[2]
    w2 = prow_ref.shape[2]
    width = nk * SSM_GROUP
    decay, kfs, kbs = [], [], []
    for g in range(n_groups):
        kf, kb, wp, wct, a_r, a_i = _ssm_operators(prow_ref[g], bt_ref[g], ct_ref[g])
        kfs.append(kf)
        kbs.append(kb)
        wp_scr[g] = wp
        wct_scr[g] = wct
        decay.append((a_r, a_i))
    tz = (jnp.dot(jnp.concatenate(kfs, axis=0).astype(BF16), shf_ref[...], preferred_element_type=F32)
          + jnp.dot(jnp.concatenate(kbs, axis=0).astype(BF16), shb_ref[...], preferred_element_type=F32))
    for g in range(n_groups):
        for s in range(nk):
            tg_scr[g, s * SSM_GROUP:(s + 1) * SSM_GROUP, :] = (
                tz[g * SSM_GROUP:(g + 1) * SSM_GROUP, s * width:(s + 1) * width].astype(BF16))

    is_fwd = lax.broadcasted_iota(jnp.int32, (1, w2), 1) < (w2 // 2)

    def scan(n_tiles, rows, src_scr, carries, mults):
        for j in range(n_tiles):
            lo, hi = j * rows, (n_tiles - 1 - j) * rows
            for g in range(n_groups):
                c_r, c_i = carries[g]
                for c, l0 in ((c_r, 0), (c_i, w2)):
                    if lo < hi:
                        s_scr[g, lo:lo + rows, l0:l0 + w2] = c
                        s_scr[g, hi:hi + rows, l0:l0 + w2] = c
                    else:
                        s_scr[g, lo:lo + rows, l0:l0 + w2] = jnp.where(is_fwd, c, s_scr[g, lo:lo + rows, l0:l0 + w2])
                        s_scr[g, hi:hi + rows, l0:l0 + w2] = jnp.where(is_fwd, s_scr[g, hi:hi + rows, l0:l0 + w2], c)
                s_r = jnp.where(is_fwd, src_scr[g, lo:lo + rows, 0:w2], src_scr[g, hi:hi + rows, 0:w2])
                s_i = jnp.where(is_fwd, src_scr[g, lo:lo + rows, w2:2 * w2], src_scr[g, hi:hi + rows, w2:2 * w2])
                n_r, n_i = _cmul(*mults[g], c_r, c_i)
                carries[g] = (n_r + s_r, n_i + s_i)
        return carries

    def prompt_carry(cb):
        z = jnp.zeros((p_batch, w2), F32)
        return scan(cb // p_batch, p_batch, p_scr, [(z, z)] * n_groups, decay)

    def sample_carry(cb):
        assert 2 * s_batch == SUBLANES
        row = lax.broadcasted_iota(jnp.int32, (SUBLANES, w2), 0)
        keep = jnp.where(is_fwd, 1, 0) == jnp.where(row < s_batch, 1, 0)
        carries, mults = [], []
        for g in range(n_groups):
            a_r, a_i = decay[g]
            p_r, p_i = p_scr[g, 0:cb, 0:w2], p_scr[g, 0:cb, w2:2 * w2]
            ap_r, ap_i = _cmul(a_r, a_i, p_r, p_i)
            q_scr[g, 0:cb, 0:w2] = jnp.where(is_fwd, pltpu.roll(p_r, cb - s_batch, 0), pltpu.roll(p_r, s_batch, 0)) + ap_r
            q_scr[g, 0:cb, w2:2 * w2] = jnp.where(is_fwd, pltpu.roll(p_i, cb - s_batch, 0),
                                                  pltpu.roll(p_i, s_batch, 0)) + ap_i
            h_r, h_i = h0r_ref[g], h0i_ref[g]
            e_r = pltpu.roll(jnp.where(is_fwd, p_r[0:SUBLANES], p_r[cb - SUBLANES:cb]), s_batch, 0)
            e_i = pltpu.roll(jnp.where(is_fwd, p_i[0:SUBLANES], p_i[cb - SUBLANES:cb]), s_batch, 0)
            ah_r, ah_i = _cmul(a_r, a_i, h_r, h_i)
            carries.append((jnp.where(keep, h_r, ah_r + e_r), jnp.where(keep, h_i, ah_i + e_i)))
            mults.append(_cmul(a_r, a_i, a_r, a_i))
        return scan(cb // SUBLANES, SUBLANES, q_scr, carries, mults)

    def mix(u_ref, y_ref, at_scr, cb, carry_fn):
        for t in range(nk):
            at_scr[t] = u_ref[t].T
        for g in range(n_groups):
            a = at_scr[:, g * SSM_GROUP:(g + 1) * SSM_GROUP, :].reshape(nk * SSM_GROUP, cb)
            u = a.T
            u_scr[g, 0:cb, :] = u
            p_scr[g, 0:cb, :] = jnp.dot(u, wp_scr[g], preferred_element_type=F32)
        finals = carry_fn(cb)
        for g in range(n_groups):
            s = s_scr[g, 0:cb, :].astype(BF16)
            y = (jnp.dot(u_scr[g, 0:cb, :], tg_scr[g], preferred_element_type=F32)
                 + lax.dot_general(s, wct_scr[g], NT, preferred_element_type=F32))
            at_scr[:, g * SSM_GROUP:(g + 1) * SSM_GROUP, :] = y.astype(BF16).T.reshape(nk, SSM_GROUP, cb)
        for t in range(nk):
            yp_scr[pl.ds(t, cb, stride=Y_PITCH), :] = at_scr[t].T.astype(F32)
        for r in range(cb):
            y_ref[r * nk:(r + 1) * nk, :] = yp_scr[r * Y_PITCH:r * Y_PITCH + nk, :]
        return finals

    finals = mix(up_ref, yp_ref, atp_scr, cb_p, prompt_carry)
    for g in range(n_groups):
        fr_ref[g] = finals[g][0]
        fi_ref[g] = finals[g][1]
    mix(us_ref, ys_ref, ats_scr, cb_s, sample_carry)


def _ssm_core(prow, bt, ct, h0r, h0i, up, us, p_batch, s_batch, jobs=()):
    groups, _, w2 = prow.shape
    gpb = LANES // SSM_GROUP
    cb_p, cb_s = up.shape[1], us.shape[1]
    np_rows, ns_rows = cb_p * SSM_CHUNK, cb_s * SSM_CHUNK
    cb_max = max(cb_p, cb_s)
    width = SSM_CHUNK * SSM_GROUP
    shf, shb = _toeplitz_shifts()

    def gspec(r, c):
        return pl.BlockSpec((gpb, r, c), lambda o: (o, 0, 0))

    def lane_spec(n):
        return pl.BlockSpec((n, LANES), lambda o: (0, o))

    def in_spec(cb):
        return pl.BlockSpec((SSM_CHUNK, cb, LANES), lambda o: (0, 0, o))

    side, c_in, c_args, c_out, c_shapes = _side_plan(jobs, groups // gpb, lambda o: o)
    yp, ys, f_re, f_im, *cast = pl.pallas_call(
        functools.partial(_ssm_core_kernel, p_batch=p_batch, s_batch=s_batch, side=side),
        grid=(groups // gpb,),
        in_specs=[gspec(SUBLANES, w2), gspec(2 * SSM_GROUP, w2), gspec(2 * SSM_GROUP, w2),
                  gspec(SUBLANES, w2), gspec(SUBLANES, w2), _resident(shf.shape), _resident(shb.shape),
                  in_spec(cb_p), in_spec(cb_s)] + c_in,
        out_specs=[lane_spec(np_rows), lane_spec(ns_rows), gspec(p_batch, w2), gspec(p_batch, w2)] + c_out,
        out_shape=[jax.ShapeDtypeStruct((np_rows, up.shape[2]), F32), jax.ShapeDtypeStruct((ns_rows, us.shape[2]), F32),
                   jax.ShapeDtypeStruct((groups, p_batch, w2), F32),
                   jax.ShapeDtypeStruct((groups, p_batch, w2), F32)] + c_shapes,
        scratch_shapes=[pltpu.VMEM((SSM_CHUNK, LANES, cb_p), BF16), pltpu.VMEM((SSM_CHUNK, LANES, cb_s), BF16)]
        + [pltpu.VMEM((gpb, width, width), BF16)] * 3 + [pltpu.VMEM((gpb, cb_max, width), BF16)]
        + [pltpu.VMEM((gpb, cb_max, 2 * w2), F32)] * 2 + [pltpu.VMEM((gpb, cb_s, 2 * w2), F32)]
        + [pltpu.VMEM((cb_max * Y_PITCH, LANES), F32)],
        compiler_params=_cparams(), name="ssm_core",
    )(prow, bt, ct, h0r, h0i, shf, shb, up, us, *c_args)
    return yp, ys, f_re, f_im, cast


def _ssm_layer(xp, xs, stp, sts, mods, layer, g_mix, w_in, lam_re, lam_im, log_dt, b_re, b_im, c_re, c_im, d_skip,
               w_out, h0_re, h0_im, ffn, pos=None, g_final=None, jobs=(), late=()):
    width = w_in.shape[1]
    groups = width // SSM_GROUP
    n_state = lam_re.shape[-1]
    up = _ssm_in(xp, stp, mods, layer, g_mix, w_in)
    us = _ssm_in(xs, sts, mods, layer, g_mix, w_in, pos)

    def lanes_dir_state(v):
        return v.transpose(1, 0, 2).reshape(groups, 2 * n_state)

    d_rows = jnp.pad(d_skip.reshape(groups, SSM_GROUP), ((0, 0), (0, 2 * n_state - SSM_GROUP)))
    prow = jnp.stack([lanes_dir_state(lam_re), lanes_dir_state(lam_im),
                      lanes_dir_state(jnp.broadcast_to(log_dt[..., None], lam_re.shape)), d_rows], axis=1)
    prow = jnp.pad(prow, ((0, 0), (0, SUBLANES - 4), (0, 0)))
    bt = jnp.concatenate([b_re.transpose(1, 3, 0, 2).reshape(groups, SSM_GROUP, 2 * n_state),
                          b_im.transpose(1, 3, 0, 2).reshape(groups, SSM_GROUP, 2 * n_state)], axis=1)
    ct = jnp.concatenate([c_re.transpose(1, 2, 0, 3).reshape(groups, SSM_GROUP, 2 * n_state),
                          c_im.transpose(1, 2, 0, 3).reshape(groups, SSM_GROUP, 2 * n_state)], axis=1)

    def h0_rows(h):
        h = h.transpose(2, 0, 1, 3).reshape(groups, sts.nb, 2 * n_state)
        return jnp.concatenate([h] * (SUBLANES // sts.nb), axis=1)

    yp, ys, f_re, f_im, conv = _ssm_core(prow, bt, ct, h0_rows(h0_re), h0_rows(h0_im), up, us, stp.nb, sts.nb, late)
    if late:
        w_out, ffn = conv[0], (ffn[0], conv[1], conv[2])
    xp, cast = _ssm_out_ffn(xp, yp, stp, mods, layer, w_out, *ffn, g_final=g_final, jobs=jobs)
    xs, _ = _ssm_out_ffn(xs, ys, sts, mods, layer, w_out, *ffn, pos=pos, g_final=g_final)

    def final(f):
        return f.reshape(groups, stp.nb, 2, n_state).transpose(1, 2, 0, 3)

    return xp, xs, final(f_re), final(f_im), cast


def _seq_rows(ref, k):
    return ref[:, k].reshape(-1, ref.shape[-1])


def _gmlp_kernel(x_ref, mod_ref, g_ref, win_ref, ws_ref, bs_ref, wout_ref, *rest, side):
    n_in, n_out = _side_counts(side)
    o_ref, t_scr = rest[n_in], rest[n_in + 1 + n_out]
    _side_run(side, rest[:n_in], rest[n_in + 1:n_in + 1 + n_out])
    n_groups, chunk, gd = bs_ref.shape
    xs, us, vns = [], [], []
    for k in range(SEQ_PAIR):
        x = _seq_rows(x_ref, k)
        mod = mod_ref[k]
        hn = _norm_mod(x, g_ref[...], mod[0:1], mod[1:2]).astype(BF16)
        z = _gelu(jnp.dot(hn, win_ref[...], preferred_element_type=F32))
        wdt = z.shape[1] // 2
        u, v = z[:, :wdt], z[:, wdt:]
        vc = v - jnp.mean(v, axis=-1, keepdims=True)
        xs.append(x)
        us.append(u)
        vns.append((vc * lax.rsqrt(jnp.mean(vc * vc, axis=-1, keepdims=True) + EPS)).astype(BF16))
    n_chunks = xs[0].shape[0] // chunk
    for g in range(n_groups):
        cols = slice(g * gd, (g + 1) * gd)
        rhs = jnp.concatenate([vn[c * chunk:(c + 1) * chunk, cols] for vn in vns for c in range(n_chunks)], axis=1)
        s = jnp.dot(ws_ref[g], rhs, preferred_element_type=F32)
        for k in range(SEQ_PAIR):
            for c in range(n_chunks):
                rows = slice(c * chunk, (c + 1) * chunk)
                i = k * n_chunks + c
                t_scr[k, rows, cols] = (us[k][rows, cols] * (s[:, i * gd:(i + 1) * gd] + bs_ref[g])).astype(BF16)
    for k in range(SEQ_PAIR):
        out = jnp.dot(t_scr[k], wout_ref[...], preferred_element_type=F32)
        o_ref[:, k] = (xs[k] + mod_ref[k][2:3] * out).reshape(o_ref.shape[0], o_ref.shape[2], o_ref.shape[3])


def _gmlp_layer(x, st, mods, layer, g_mix, w_in, w_s, bs, w_out, jobs=()):
    d = x.shape[-1]
    side, s_in, s_args, s_out, s_shapes = st.side_plan(jobs, seq=True)
    out, *extra = pl.pallas_call(
        functools.partial(_gmlp_kernel, side=side),
        grid=st.seq_grid(),
        in_specs=[st.seq_spec(d), st.seq_mod_spec(layer, d), _resident((1, d)), _resident(w_in.shape),
                  _resident(w_s.shape), _resident(bs.shape), _resident(w_out.shape)] + s_in,
        out_specs=[st.seq_spec(d)] + s_out,
        out_shape=[jax.ShapeDtypeStruct(x.shape, F32)] + s_shapes,
        scratch_shapes=[pltpu.VMEM((SEQ_PAIR, SEQ_CHUNKS * SSM_CHUNK, w_out.shape[0]), BF16)],
        compiler_params=_cparams(2), name="gmlp",
    )(x, mods, g_mix, w_in, w_s, bs, w_out, *s_args)
    return out, extra


def _conv_ffn_kernel(x_ref, xprev_ref, xnext_ref, mod_ref, g_ref, win_ref, cw_ref, wout_ref, gf_ref, w1_ref, w2_ref,
                     *rest, seq_blocks, th, side):
    n_in, _ = _side_counts(side)
    o_ref = rest[n_in]
    j = pl.program_id(1)
    g = g_ref[...]
    cw = cw_ref[...]
    d = x_ref.shape[-1]
    xs = [_seq_rows(x_ref, k) for k in range(SEQ_PAIR)]
    mods = [mod_ref[k] for k in range(SEQ_PAIR)]
    if seq_blocks > 1:
        hh = jnp.concatenate([_norm_mod(jnp.concatenate([xprev_ref[k], xnext_ref[k]], axis=0), g, m[0:1], m[1:2])
                              for k, m in enumerate(mods)], axis=0).astype(BF16)
        zh = jnp.dot(hh, win_ref[:, d:], preferred_element_type=F32)
        th_all = zh[:, :d] * zh[:, d:]
    else:
        th_all = jnp.zeros((2 * SUBLANES * SEQ_PAIR, d), F32)
    mids = []
    for k, (x, mod) in enumerate(zip(xs, mods)):
        tm = x.shape[0]
        hn = _norm_mod(x, g, mod[0:1], mod[1:2]).astype(BF16)
        z = jnp.dot(hn, win_ref[...], preferred_element_type=F32)
        gb, t = z[:, :d], z[:, d:2 * d] * z[:, 2 * d:]
        r0 = 2 * SUBLANES * k
        t_prev = jnp.where(j > 0, th_all[r0 + SUBLANES - 1:r0 + SUBLANES], 0.0)
        t_next = jnp.where(j < seq_blocks - 1, th_all[r0 + SUBLANES:r0 + SUBLANES + 1], 0.0)
        row = lax.broadcasted_iota(jnp.int32, (tm, 1), 0)
        up = jnp.where(row == 0, t_prev, pltpu.roll(t, 1, 0))
        dn = jnp.where(row == tm - 1, t_next, pltpu.roll(t, tm - 1, 0))
        y = cw[0:1] * up + cw[1:2] * t + cw[2:3] * dn
        out = jnp.dot((gb * y).astype(BF16), wout_ref[...], preferred_element_type=F32)
        mids.append(x + mod[2:3] * out)
    outs = _ffn_blocks(mids, mods, gf_ref[...], w1_ref, w2_ref, th)
    for k, out in enumerate(outs):
        o_ref[:, k] = out.reshape(o_ref.shape[0], o_ref.shape[2], o_ref.shape[3])
    _side_run(side, rest[:n_in], rest[n_in + 1:])


def _conv_ffn_layer(x, st, mods, layer, g_mix, w_in, cw, w_out, gf, w1, w2, jobs=(), th=1024):
    d = x.shape[-1]
    prev, nxt = st.halo_specs(d)
    xh = st.halo_view(x)
    side, c_in, c_args, c_out, c_shapes = st.side_plan(jobs, seq=True)
    out, *cast = pl.pallas_call(
        functools.partial(_conv_ffn_kernel, seq_blocks=st.seq_blocks, th=th, side=side),
        grid=st.seq_grid(),
        in_specs=[st.seq_spec(d), prev, nxt, st.seq_mod_spec(layer, d), _resident((1, d)), _resident(w_in.shape),
                  _resident(cw.shape), _resident(w_out.shape), _resident((1, d)), _resident(w1.shape),
                  _resident(w2.shape)] + c_in,
        out_specs=[st.seq_spec(d)] + c_out,
        out_shape=[jax.ShapeDtypeStruct(x.shape, F32)] + c_shapes,
        compiler_params=_cparams(2), name="conv_ffn",
    )(x, xh, xh, mods, g_mix, w_in, cw, w_out, gf, w1, w2, *c_args)
    return out, cast


def kernel(x_prompt, x_sample, state_ssm_re, state_ssm_im, c, c_ctx, w_mod, b_mod, g_mix, g_ffn, ffn_w1, ffn_w2, ssm_w_in, ssm_lam_re, ssm_lam_im, ssm_log_dt, ssm_b_re, ssm_b_im, ssm_c_re, ssm_c_im, ssm_d, ssm_w_out, gmlp_w_in, gmlp_w_s, gmlp_b_s, gmlp_w_out, conv_w_in, conv_w, conv_w_out, g_final):
    n_prompt, len_prompt, d = x_prompt.shape
    n_sample, len_sample, _ = x_sample.shape
    depth = w_mod.shape[0]
    n_mixers = 3
    assert 1 + n_sample <= SUBLANES
    stp = _Stream(len_prompt // SSM_CHUNK, n_prompt, per_batch_cond=False)
    sts = _Stream(len_sample // SSM_CHUNK, n_sample, per_batch_cond=True)
    sts_ffn = _Stream(len_sample // SSM_CHUNK, n_sample, per_batch_cond=True, row_block=FFN_ROW_BLOCK)

    xp, xs = x_prompt, x_sample
    pos = _grid_pos_embed(len_sample, d)

    cond_t = jnp.concatenate([c_ctx[None, :], c, jnp.zeros((SUBLANES - 1 - n_sample, d), F32)], axis=0).T
    n_cond = 1 + n_sample
    table = jnp.array([0] * CTX_ROWS + list(range(1, n_cond)))

    def mod_table(m):
        return m[table].reshape(1, table.shape[0], N_MOD, d)

    assert (depth - 1) % n_mixers == 0, "the last layer's kernel writes the natural-order outputs"

    def layer_weights(i):
        kind, j = i % n_mixers, i // n_mixers
        mixer = {0: [(ssm_w_in, j), (ssm_w_out, j)],
                 1: [(gmlp_w_in, j), (gmlp_w_s.reshape(gmlp_w_s.shape[0], -1, gmlp_w_s.shape[-1]), j), (gmlp_w_out, j)],
                 2: [(conv_w_in, j), (conv_w_out, j)]}[kind]
        return [("cast", stack, layer) for stack, layer in mixer + [(ffn_w1, i), (ffn_w2, i)]]

    def adaln_jobs(layers):
        return [("adaln", cond_t, w_mod, b_mod, l, n_cond) for l in layers]

    first = layer_weights(0)
    wb = [first[0][1][first[0][2]].astype(BF16)] + [None] * (len(first) - 1)
    mods = {0: mod_table(_adaln(cond_t, w_mod, b_mod, n_cond, [0])[0])}
    new_re, new_im = [], []
    for i in range(depth):
        kind, j = i % n_mixers, i // n_mixers
        gm = g_mix[i].reshape(1, d)
        nxt = layer_weights(i + 1) if i + 1 < depth else []
        *wm, w1, w2 = wb
        ffn = (g_ffn[i].reshape(1, d), w1, w2)
        m = mods[i]
        if kind == 0:
            later = adaln_jobs([i + 1]) if i + 1 < depth else []
            xp, xs, f_re, f_im, extra = _ssm_layer(
                xp, xs, stp, sts, m, 0, gm, wm[0], ssm_lam_re[j], ssm_lam_im[j], ssm_log_dt[j],
                ssm_b_re[j], ssm_b_im[j], ssm_c_re[j], ssm_c_im[j], ssm_d[j], wm[1],
                state_ssm_re[:, j], state_ssm_im[:, j], ffn, pos if i == 0 else None,
                g_final.reshape(1, d) if i == depth - 1 else None, nxt + later, first[1:] if i == 0 else ())
            wb = extra[:len(nxt)]
            for job, out in zip(later, extra[len(nxt):]):
                mods[job[4]] = mod_table(out)
            new_re.append(f_re)
            new_im.append(f_im)
        elif kind == 1:
            n_groups, chunk, _ = gmlp_w_s[j].shape
            gd = gmlp_w_out[j].shape[0] // n_groups
            bs = jnp.broadcast_to(gmlp_b_s[j][:, :, None], (n_groups, chunk, gd))
            ws = (wm[0], wm[1].reshape(gmlp_w_s[j].shape), bs, wm[2])
            later = adaln_jobs([i + 1]) if i + 1 < depth else []
            xp, extra = _ffn(_gmlp_layer(xp, stp, m, 0, gm, *ws)[0], stp, m, 0, *ffn, jobs=nxt + later)
            wb = extra[:len(nxt)]
            for job, out in zip(later, extra[len(nxt):]):
                mods[job[4]] = mod_table(out)
            xs, _ = _ffn(_gmlp_layer(xs, sts, m, 0, gm, *ws)[0], sts_ffn, m, 0, *ffn)
        else:
            ws = (wm[0], jnp.pad(conv_w[j], ((0, SUBLANES - CONV_WIDTH), (0, 0))), wm[1])
            later = adaln_jobs([i + 1]) if i + 1 < depth else []
            xp, extra = _conv_ffn_layer(xp, stp, m, 0, gm, *ws, *ffn, jobs=nxt + later)
            wb = extra[:len(nxt)]
            for job, out in zip(later, extra[len(nxt):]):
                mods[job[4]] = mod_table(out)
            xs, _ = _conv_ffn_layer(xs, sts, m, 0, gm, *ws, *ffn)

    return (xp.reshape(x_prompt.shape), xs.reshape(x_sample.shape),
            jnp.stack(new_re, axis=1), jnp.stack(new_im, axis=1))
```
